```python
import math
import jax, jax.numpy as jnp
from jax import lax
import numpy as np

D_MODEL = 1024
BATCH = 4
SEQ = 8192
DEPTH = 1

RET_HEADS = 4
RET_QK_DIM = 64
RET_V_DIM = 128
RET_CHUNK = 128
RET_WIDTH = RET_HEADS * RET_V_DIM
MOBA_HEADS = 8
MOBA_HEAD_DIM = 64
MOBA_WIDTH = MOBA_HEADS * MOBA_HEAD_DIM
MOBA_BLOCK = 256
MOBA_TOPK = 3
MOBA_QBLOCK = 64
REL_BUCKETS = 32
REL_MAX_DIST = 128
D_FF = -(-8 * D_MODEL // (3 * 256)) * 256
MIX_WIDTH = RET_WIDTH + MOBA_WIDTH
IN_COLS = 2 * RET_HEADS * RET_QK_DIM + 2 * RET_WIDTH + 3 * MOBA_WIDTH
EPS = 1e-6
ROPE_BASE = 10000.0

kernel_name = "hybrid_retention_moba_sandwich_block"


def rmsnorm(x, g):
    xf = x.astype(jnp.float32)
    y = xf * lax.rsqrt(jnp.mean(xf * xf, axis=-1, keepdims=True) + EPS)
    return (y * g.astype(jnp.float32)).astype(x.dtype)


def head_rmsnorm(x):
    xf = x.astype(jnp.float32)
    return (xf * lax.rsqrt(jnp.mean(xf * xf, axis=-1, keepdims=True) + EPS)).astype(x.dtype)


def rotary(x):
    s, d = x.shape[1], x.shape[-1]
    half = d // 2
    inv_freq = ROPE_BASE ** (-jnp.arange(half, dtype=jnp.float32) / half)
    ang = jnp.arange(s, dtype=jnp.float32)[:, None] * inv_freq[None, :]
    cos = jnp.cos(ang)[None, :, None, :]
    sin = jnp.sin(ang)[None, :, None, :]
    xf = x.astype(jnp.float32)
    x1, x2 = xf[..., :half], xf[..., half:]
    return jnp.concatenate([x1 * cos - x2 * sin, x2 * cos + x1 * sin], axis=-1).astype(x.dtype)


def t5_bucket(rel):
    n = jnp.maximum(rel, 0)
    max_exact = REL_BUCKETS // 2
    is_small = n < max_exact
    nf = jnp.maximum(n, 1).astype(jnp.float32)
    large = max_exact + (jnp.log(nf / max_exact) / math.log(REL_MAX_DIST / max_exact)
                         * (REL_BUCKETS - max_exact)).astype(jnp.int32)
    large = jnp.minimum(large, REL_BUCKETS - 1)
    return jnp.where(is_small, n, large)


def retention(q, k, v):
    b, s, h, dk = q.shape
    dv = v.shape[-1]
    c = RET_CHUNK
    n = s // c

    def to_chunks(t):
        return t.astype(jnp.float32).reshape(b, n, c, h, t.shape[-1]).transpose(1, 0, 3, 2, 4)

    qc, kc, vc = to_chunks(q), to_chunks(k), to_chunks(v)
    log_gamma = jnp.log1p(-jnp.exp(jnp.linspace(math.log(1.0 / 32), math.log(1.0 / 512), RET_HEADS)))
    log_gamma = log_gamma.astype(jnp.float32)
    idx = jnp.arange(c, dtype=jnp.float32)
    diff = idx[:, None] - idx[None, :]
    intra_decay = jnp.where(diff >= 0, jnp.exp(jnp.maximum(diff, 0.0)[None] * log_gamma[:, None, None]), 0.0)
    q_decay = jnp.exp((idx + 1.0)[None, :] * log_gamma[:, None])
    k_decay = jnp.exp((c - 1.0 - idx)[None, :] * log_gamma[:, None])
    chunk_decay = jnp.exp(c * log_gamma)

    def step(state, inp):
        qi, ki, vi = inp
        scores = jnp.einsum('bhid,bhjd->bhij', qi, ki) * intra_decay[None]
        o = jnp.einsum('bhij,bhje->bhie', scores, vi)
        o = o + jnp.einsum('bhid,bhde->bhie', qi * q_decay[None, :, :, None], state)
        new_state = state * chunk_decay[None, :, None, None] + jnp.einsum(
            'bhjd,bhje->bhde', ki * k_decay[None, :, :, None], vi)
        return new_state, o

    state0 = jnp.zeros((b, h, dk, dv), jnp.float32)
    _, o = lax.scan(step, state0, (qc, kc, vc))
    return o.transpose(1, 0, 3, 2, 4).reshape(b, s, h, dv).astype(v.dtype)


def moba_attention(q, k, v, rel_bias):
    b, s, h, d = q.shape
    bs = MOBA_BLOCK
    nb = -(-s // bs)
    sp = nb * bs
    pad = sp - s

    def prep(t):
        return jnp.pad(t, ((0, 0), (0, pad), (0, 0), (0, 0))).transpose(0, 2, 1, 3)

    q, k, v = prep(q), prep(k), prep(v)
    kb = k.reshape(b, h, nb, bs, d)
    vb = v.reshape(b, h, nb, bs, d)
    k_mean = jnp.mean(kb.astype(jnp.float32), axis=3)
    gate = jnp.einsum('bhsd,bhnd->bhsn', q.astype(jnp.float32), k_mean)
    q_blk = jnp.arange(sp) // bs
    fully_past = jnp.arange(nb)[None, :] < q_blk[:, None]
    gate = jnp.where(fully_past[None, None], gate, -jnp.inf)
    topk = min(MOBA_TOPK, nb)
    vals, sel = lax.top_k(gate, topk)
    valid = jnp.isfinite(vals)
    sel = jnp.where(valid, sel, 0)

    bias_t = rel_bias.T.astype(jnp.float32)
    scale = d ** -0.5
    bi = jnp.arange(b)[:, None, None, None]
    hi = jnp.arange(h)[None, :, None, None]
    kidx = jnp.arange(bs)

    def query_block(start):
        qs = lax.dynamic_slice_in_dim(q, start, MOBA_QBLOCK, axis=2)
        sel_b = lax.dynamic_slice_in_dim(sel, start, MOBA_QBLOCK, axis=2)
        valid_b = lax.dynamic_slice_in_dim(valid, start, MOBA_QBLOCK, axis=2)
        qp = start + jnp.arange(MOBA_QBLOCK)
        k_sel = kb[bi, hi, sel_b]
        v_sel = vb[bi, hi, sel_b]
        s_sel = jnp.einsum('bhqd,bhqnkd->bhqnk', qs, k_sel).astype(jnp.float32) * scale
        kp_sel = sel_b[..., None] * bs + kidx
        bias_sel = bias_t[hi[..., None], t5_bucket(qp[None, None, :, None, None] - kp_sel)]
        s_sel = jnp.where(valid_b[..., None], s_sel + bias_sel, -jnp.inf)
        own = start // bs
        k_own = lax.dynamic_index_in_dim(kb, own, axis=2, keepdims=False)
        v_own = lax.dynamic_index_in_dim(vb, own, axis=2, keepdims=False)
        s_own = jnp.einsum('bhqd,bhkd->bhqk', qs, k_own).astype(jnp.float32) * scale
        rel_own = qp[:, None] - (own * bs + kidx)[None, :]
        bias_own = bias_t[:, t5_bucket(rel_own)]
        s_own = jnp.where(rel_own[None, None] >= 0, s_own + bias_own[None], -jnp.inf)
        logits = jnp.concatenate([s_sel.reshape(b, h, MOBA_QBLOCK, topk * bs), s_own], axis=-1)
        p = jax.nn.softmax(logits, axis=-1).astype(v.dtype)
        p_sel = p[..., :topk * bs].reshape(b, h, MOBA_QBLOCK, topk, bs)
        p_own = p[..., topk * bs:]
        return (jnp.einsum('bhqnk,bhqnkd->bhqd', p_sel, v_sel)
                + jnp.einsum('bhqk,bhkd->bhqd', p_own, v_own))

    starts = jnp.arange(sp // MOBA_QBLOCK) * MOBA_QBLOCK
    o = lax.map(query_block, starts)
    o = o.transpose(1, 2, 0, 3, 4).reshape(b, h, sp, d)[:, :, :s]
    return o.transpose(0, 2, 1, 3)


def setup_inputs(seed: int = 0) -> dict:
    key = jax.random.key(seed)
    ks = jax.random.split(key, 12)
    f32 = jnp.float32

    def w(k, shape, fan_in):
        return jax.random.normal(k, shape, f32) * fan_in ** -0.5

    def gain(k):
        return 1.0 + 0.02 * jax.random.normal(k, (DEPTH, D_MODEL), f32)

    return {
        "x": jax.random.normal(ks[0], (BATCH, SEQ, D_MODEL), f32),
        "w_in": w(ks[1], (DEPTH, D_MODEL, IN_COLS), D_MODEL),
        "w_out": w(ks[2], (DEPTH, MIX_WIDTH, D_MODEL), MIX_WIDTH),
        "pre_mix_norm": gain(ks[3]),
        "post_mix_norm": gain(ks[4]),
        "pre_ffn_norm": gain(ks[5]),
        "post_ffn_norm": gain(ks[6]),
        "rel_bias": 0.5 * jax.random.normal(ks[7], (REL_BUCKETS, MOBA_HEADS), f32),
        "w_gate": w(ks[8], (DEPTH, D_MODEL, D_FF), D_MODEL),
        "w_up": w(ks[9], (DEPTH, D_MODEL, D_FF), D_MODEL),
        "w_down": w(ks[10], (DEPTH, D_FF, D_MODEL), D_FF),
    }


def reference(x, w_in, w_out, pre_mix_norm, post_mix_norm, pre_ffn_norm, post_ffn_norm,
              rel_bias, w_gate, w_up, w_down):
    b, s, _ = x.shape
    sizes = [RET_HEADS * RET_QK_DIM, RET_HEADS * RET_QK_DIM, RET_WIDTH, RET_WIDTH,
             MOBA_WIDTH, MOBA_WIDTH, MOBA_WIDTH]
    split_at = np.cumsum(sizes)[:-1].tolist()
    for layer in range(DEPTH):
        h = rmsnorm(x, pre_mix_norm[layer])
        proj = h @ w_in[layer]
        rq, rk, rv, rg, mq, mk, mv = jnp.split(proj, split_at, axis=-1)
        rq = rotary(rq.reshape(b, s, RET_HEADS, RET_QK_DIM))
        rk = rotary(rk.reshape(b, s, RET_HEADS, RET_QK_DIM)) * (RET_QK_DIM ** -0.5)
        rv = rv.reshape(b, s, RET_HEADS, RET_V_DIM)
        ret = head_rmsnorm(retention(rq, rk, rv)).reshape(b, s, RET_WIDTH)
        ret = jax.nn.silu(rg) * ret
        moba = moba_attention(mq.reshape(b, s, MOBA_HEADS, MOBA_HEAD_DIM),
                              mk.reshape(b, s, MOBA_HEADS, MOBA_HEAD_DIM),
                              mv.reshape(b, s, MOBA_HEADS, MOBA_HEAD_DIM),
                              rel_bias).reshape(b, s, MOBA_WIDTH)
        mix = jnp.concatenate([ret, moba], axis=-1) @ w_out[layer]
        x = x + rmsnorm(mix, post_mix_norm[layer])
        h = rmsnorm(x, pre_ffn_norm[layer])
        f = (jax.nn.silu(h @ w_gate[layer]) * (h @ w_up[layer])) @ w_down[layer]
        x = x + rmsnorm(f, post_ffn_norm[layer])
    return x
```

```python
import functools
import math

import jax
import jax.numpy as jnp
import numpy as np
from jax import lax
from jax.experimental import pallas as pl
from jax.experimental.pallas import tpu as pltpu

F32 = jnp.float32
BF16 = jnp.bfloat16

D_MODEL = 1024
RET_HEADS = 4
RET_QK_DIM = 64
RET_V_DIM = 128
RET_CHUNK = 128
RET_QK_WIDTH = RET_HEADS * RET_QK_DIM
RET_WIDTH = RET_HEADS * RET_V_DIM
MOBA_HEADS = 8
MOBA_HEAD_DIM = 64
MOBA_WIDTH = MOBA_HEADS * MOBA_HEAD_DIM
MOBA_BLOCK = 256
MOBA_TOPK = 3
REL_BUCKETS = 32
REL_MAX_DIST = 128
D_FF = 2816
EPS = 1e-6
ROPE_BASE = 10000.0

NEG = -1e30

IN_TOKENS = 512
RET_TOKENS = 1024
FFN_TOKENS = 512
FFN_CHUNK = 1408
VMEM_LIMIT = 56 * 1024 * 1024

NAT_WIDTH = RET_QK_WIDTH + 2 * RET_WIDTH + MOBA_WIDTH
TR_WIDTH = RET_QK_WIDTH + 2 * MOBA_WIDTH


def _nt_dot(a, b):
    return lax.dot_general(a, b, (((1,), (1,)), ((), ())), preferred_element_type=F32)


def _dot(a, b):
    return jnp.dot(a, b, preferred_element_type=F32)


def _in_proj_kernel(x_ref, g_ref, wn_ref, wt_ref, cq_ref, sq_ref, ckt_ref, skt_ref,
                    rq_ref, rv_ref, sg_ref, mk_ref, kmean_ref, rkt_ref, mqt_ref, mvt_ref):
    tm = x_ref.shape[1]
    x = x_ref[0]
    ms = jnp.mean(x * x, axis=-1, keepdims=True)
    h = (x * lax.rsqrt(ms + EPS) * g_ref[...]).astype(BF16)

    pn = _dot(h, wn_ref[...])
    pt = _nt_dot(wt_ref[...], h)

    rq = pn[:, :RET_QK_WIDTH]
    lane = lax.broadcasted_iota(jnp.int32, rq.shape, 1)
    first_half = (lane % RET_QK_DIM) < (RET_QK_DIM // 2)
    partner = jnp.where(first_half,
                        pltpu.roll(rq, RET_QK_WIDTH - RET_QK_DIM // 2, 1),
                        pltpu.roll(rq, RET_QK_DIM // 2, 1))
    rq_ref[0] = (rq * cq_ref[...] + partner * sq_ref[...]).astype(BF16)

    rv_ref[0] = pn[:, RET_QK_WIDTH:RET_QK_WIDTH + RET_WIDTH].astype(BF16)
    rg = pn[:, RET_QK_WIDTH + RET_WIDTH:RET_QK_WIDTH + 2 * RET_WIDTH]
    sg_ref[0] = (rg * jax.nn.sigmoid(rg)).astype(BF16)

    mk = pn[:, RET_QK_WIDTH + 2 * RET_WIDTH:]
    mk_ref[0] = mk.astype(BF16)
    for blk in range(tm // MOBA_BLOCK):
        kmean_ref[0, 0, blk:blk + 1, :] = jnp.mean(
            mk[blk * MOBA_BLOCK:(blk + 1) * MOBA_BLOCK], axis=0, keepdims=True)

    half = RET_QK_DIM // 2
    cos_t = ckt_ref[...]
    sin_t = skt_ref[...]
    parts = []
    for hd in range(RET_HEADS):
        x1 = pt[hd * RET_QK_DIM:hd * RET_QK_DIM + half]
        x2 = pt[hd * RET_QK_DIM + half:(hd + 1) * RET_QK_DIM]
        parts.append(x1 * cos_t - x2 * sin_t)
        parts.append(x2 * cos_t + x1 * sin_t)
    rkt = (jnp.concatenate(parts, axis=0) * (RET_QK_DIM ** -0.5)).astype(BF16)
    for c in range(tm // RET_CHUNK):
        rkt_ref[0, c] = rkt[:, c * RET_CHUNK:(c + 1) * RET_CHUNK]

    mqt = (pt[RET_QK_WIDTH:RET_QK_WIDTH + MOBA_WIDTH] * (MOBA_HEAD_DIM ** -0.5)).astype(BF16)
    mvt = pt[RET_QK_WIDTH + MOBA_WIDTH:].astype(BF16)
    for blk in range(tm // MOBA_BLOCK):
        mqt_ref[0, blk] = mqt[:, blk * MOBA_BLOCK:(blk + 1) * MOBA_BLOCK]
        mvt_ref[0, blk] = mvt[:, blk * MOBA_BLOCK:(blk + 1) * MOBA_BLOCK]


def _in_proj(x, gain, w_nat, w_tr, cq, sq, ckt, skt):
    b, s, d = x.shape
    tm = IN_TOKENS
    ns = s // tm
    bpt = tm // MOBA_BLOCK
    cpt = tm // RET_CHUNK
    const = lambda si, bi: (0, 0)
    out_shape = (
        jax.ShapeDtypeStruct((b, s, RET_QK_WIDTH), BF16),
        jax.ShapeDtypeStruct((b, s, RET_WIDTH), BF16),
        jax.ShapeDtypeStruct((b, s, RET_WIDTH), BF16),
        jax.ShapeDtypeStruct((b, s, MOBA_WIDTH), BF16),
        jax.ShapeDtypeStruct((b, ns, bpt, MOBA_WIDTH), F32),
        jax.ShapeDtypeStruct((b, s // RET_CHUNK, RET_QK_WIDTH, RET_CHUNK), BF16),
        jax.ShapeDtypeStruct((b, s // MOBA_BLOCK, MOBA_WIDTH, MOBA_BLOCK), BF16),
        jax.ShapeDtypeStruct((b, s // MOBA_BLOCK, MOBA_WIDTH, MOBA_BLOCK), BF16),
    )
    tok = lambda w: pl.BlockSpec((1, tm, w), lambda si, bi: (bi, si, 0))
    return pl.pallas_call(
        _in_proj_kernel,
        grid=(ns, b),
        in_specs=[
            tok(d),
            pl.BlockSpec((1, d), const),
            pl.BlockSpec((d, NAT_WIDTH), const),
            pl.BlockSpec((TR_WIDTH, d), const),
            pl.BlockSpec((tm, RET_QK_WIDTH), lambda si, bi: (si, 0)),
            pl.BlockSpec((tm, RET_QK_WIDTH), lambda si, bi: (si, 0)),
            pl.BlockSpec((RET_QK_DIM // 2, tm), lambda si, bi: (0, si)),
            pl.BlockSpec((RET_QK_DIM // 2, tm), lambda si, bi: (0, si)),
        ],
        out_specs=(
            tok(RET_QK_WIDTH), tok(RET_WIDTH), tok(RET_WIDTH), tok(MOBA_WIDTH),
            pl.BlockSpec((1, 1, bpt, MOBA_WIDTH), lambda si, bi: (bi, si, 0, 0)),
            pl.BlockSpec((1, cpt, RET_QK_WIDTH, RET_CHUNK), lambda si, bi: (bi, si, 0, 0)),
            pl.BlockSpec((1, bpt, MOBA_WIDTH, MOBA_BLOCK), lambda si, bi: (bi, si, 0, 0)),
            pl.BlockSpec((1, bpt, MOBA_WIDTH, MOBA_BLOCK), lambda si, bi: (bi, si, 0, 0)),
        ),
        out_shape=out_shape,
        compiler_params=pltpu.CompilerParams(
            dimension_semantics=("arbitrary", "arbitrary"), vmem_limit_bytes=VMEM_LIMIT),
        name="in_proj",
    )(x, gain, w_nat, w_tr, cq, sq, ckt, skt)


def _retention_kernel(rq_ref, rkt_ref, rv_ref, sg_ref, dmat_ref, qd_ref, kd_ref, cd_ref,
                      out_ref, state_ref):
    @pl.when(pl.program_id(1) == 0)
    def _():
        state_ref[...] = jnp.zeros_like(state_ref)

    n_chunks = rq_ref.shape[1] // RET_CHUNK

    def chunk(c, carry):
        r0 = pl.multiple_of(c * RET_CHUNK, RET_CHUNK)
        rows = pl.ds(r0, RET_CHUNK)
        for hd in range(RET_HEADS):
            q = rq_ref[0, rows, hd * RET_QK_DIM:(hd + 1) * RET_QK_DIM]
            kt = rkt_ref[0, c, hd * RET_QK_DIM:(hd + 1) * RET_QK_DIM, :]
            v = rv_ref[0, rows, hd * RET_V_DIM:(hd + 1) * RET_V_DIM]
            state = state_ref[hd]
            scores = _dot(q, kt) * dmat_ref[hd]
            o = _dot(scores.astype(BF16), v) + qd_ref[hd] * _dot(q, state.astype(BF16))
            kts = (kt.astype(F32) * kd_ref[hd]).astype(BF16)
            state_ref[hd] = state * cd_ref[hd] + _dot(kts, v)
            ms = jnp.mean(o * o, axis=-1, keepdims=True)
            gate = sg_ref[0, rows, hd * RET_V_DIM:(hd + 1) * RET_V_DIM].astype(F32)
            out_ref[0, rows, hd * RET_V_DIM:(hd + 1) * RET_V_DIM] = (
                o * lax.rsqrt(ms + EPS) * gate).astype(BF16)
        return carry

    lax.fori_loop(0, n_chunks, chunk, 0)


def _retention(rq, rkt, rv, sg, dmat, qd, kd, cd):
    b, s, _ = rq.shape
    tc = min(RET_TOKENS, s)
    tok = lambda w: pl.BlockSpec((1, tc, w), lambda bi, si: (bi, si, 0))
    tab = lambda a: pl.BlockSpec(a.shape, lambda bi, si: (0,) * a.ndim)
    return pl.pallas_call(
        _retention_kernel,
        grid=(b, s // tc),
        in_specs=[
            tok(RET_QK_WIDTH),
            pl.BlockSpec((1, tc // RET_CHUNK, RET_QK_WIDTH, RET_CHUNK), lambda bi, si: (bi, si, 0, 0)),
            tok(RET_WIDTH), tok(RET_WIDTH),
            tab(dmat), tab(qd), tab(kd), tab(cd),
        ],
        out_specs=tok(RET_WIDTH),
        out_shape=jax.ShapeDtypeStruct((b, s, RET_WIDTH), BF16),
        scratch_shapes=[pltpu.VMEM((RET_HEADS, RET_QK_DIM, RET_V_DIM), F32)],
        compiler_params=pltpu.CompilerParams(
            dimension_semantics=("arbitrary", "arbitrary"), vmem_limit_bytes=VMEM_LIMIT),
        name="retention",
    )(rq, rkt, rv, sg, dmat, qd, kd, cd)


ONES_ROWS = 16


def _moba_kernel(cfar_ref, qt_ref, k_ref, vt_ref, km_ref, town_ref, tprev_ref,
                 out_ref, seladd_ref, m_ref, acc_ref):
    pair = pl.program_id(1)
    i = pl.program_id(2)
    d = MOBA_HEAD_DIM
    nb = km_ref.shape[2]
    qt2 = qt_ref[0, 0]
    prow = lax.broadcasted_iota(jnp.int32, qt2.shape, 0)
    ones = jnp.ones((ONES_ROWS, MOBA_BLOCK), BF16)

    def scores(e, j):
        kblk = k_ref[0, pl.ds(pl.multiple_of(j * MOBA_BLOCK, MOBA_BLOCK), MOBA_BLOCK), :]
        qpad = jnp.where((prow < d) == (e == 0), qt2, jnp.zeros_like(qt2))
        return _dot(kblk, qpad)

    def vaug(e, j):
        return jnp.concatenate([vt_ref[0, j, e * d:(e + 1) * d, :], ones], axis=0)

    for e in range(2):
        qt = qt2[e * d:(e + 1) * d]
        km = km_ref[0, e]
        km_hi = km.astype(BF16)
        km_lo = (km - km_hi.astype(F32)).astype(BF16)
        gate = _dot(km_hi, qt) + _dot(km_lo, qt)
        blk = lax.broadcasted_iota(jnp.int32, gate.shape, 0)
        gate = jnp.where(blk < i, gate, NEG)
        blk_f = blk.astype(F32)
        chosen = jnp.zeros(gate.shape, F32)
        for _ in range(MOBA_TOPK):
            best = jnp.max(gate, axis=0, keepdims=True)
            idx = jnp.min(jnp.where(gate == best, blk_f, float(nb)), axis=0, keepdims=True)
            hit = blk_f == idx
            chosen = jnp.where(hit & (best > 0.5 * NEG), 1.0, chosen)
            gate = jnp.where(hit, NEG, gate)
        seladd_ref[e] = jnp.where(chosen > 0.5, 0.0, NEG)

        st = scores(e, i) + town_ref[e]
        m = jnp.max(st, axis=0, keepdims=True)
        p = jnp.exp(st - m)
        m_ref[e] = m
        acc_ref[e] = _dot(vaug(e, i), p.astype(BF16))

    def update(e, j, st):
        m_old = m_ref[e]
        m_new = jnp.maximum(m_old, jnp.max(st, axis=0, keepdims=True))
        alpha = jnp.exp(m_old - m_new)
        p = jnp.exp(st - m_new)
        acc_ref[e] = acc_ref[e] * alpha + _dot(vaug(e, j), p.astype(BF16))
        m_ref[e] = m_new

    @pl.when(i >= 1)
    def _():
        j = i - 1
        for e in range(2):
            update(e, j, scores(e, j) + tprev_ref[e] + seladd_ref[e, pl.ds(j, 1), :])

    def far(j, carry):
        for e in range(2):
            add = seladd_ref[e, pl.ds(j, 1), :] + cfar_ref[pair * 2 + e]
            update(e, j, scores(e, j) + add)
        return carry

    lax.fori_loop(0, i - 1, far, 0)

    outs = []
    for e in range(2):
        acc = acc_ref[e]
        o = acc[:d] / acc[d:d + 1]
        outs.append(o.T)
    out_ref[0] = jnp.concatenate(outs, axis=1).astype(BF16)


def _moba(cfar, mqt, mk, mvt, kmean, town, tprev):
    b, s, _ = mk.shape
    nb = s // MOBA_BLOCK
    npair = MOBA_HEADS // 2
    pw = 2 * MOBA_HEAD_DIM
    return pl.pallas_call(
        _moba_kernel,
        grid=(b, npair, nb),
        in_specs=[
            pl.BlockSpec(memory_space=pltpu.SMEM),
            pl.BlockSpec((1, 1, pw, MOBA_BLOCK), lambda bi, p, i: (bi, i, p, 0)),
            pl.BlockSpec((1, s, pw), lambda bi, p, i: (bi, 0, p)),
            pl.BlockSpec((1, nb, pw, MOBA_BLOCK), lambda bi, p, i: (bi, 0, p, 0)),
            pl.BlockSpec((1, 2, nb, MOBA_HEAD_DIM), lambda bi, p, i: (bi, p, 0, 0)),
            pl.BlockSpec((2, MOBA_BLOCK, MOBA_BLOCK), lambda bi, p, i: (p, 0, 0)),
            pl.BlockSpec((2, MOBA_BLOCK, MOBA_BLOCK), lambda bi, p, i: (p, 0, 0)),
        ],
        out_specs=pl.BlockSpec((1, MOBA_BLOCK, pw), lambda bi, p, i: (bi, i, p)),
        out_shape=jax.ShapeDtypeStruct((b, s, MOBA_WIDTH), BF16),
        scratch_shapes=[
            pltpu.VMEM((2, nb, MOBA_BLOCK), F32),
            pltpu.VMEM((2, 1, MOBA_BLOCK), F32),
            pltpu.VMEM((2, MOBA_HEAD_DIM + ONES_ROWS, MOBA_BLOCK), F32),
        ],
        compiler_params=pltpu.CompilerParams(
            dimension_semantics=("arbitrary", "arbitrary", "arbitrary"),
            vmem_limit_bytes=VMEM_LIMIT),
        name="moba",
    )(cfar, mqt, mk, mvt, kmean, town, tprev)


def _rms(x, g):
    return x * lax.rsqrt(jnp.mean(x * x, axis=-1, keepdims=True) + EPS) * g


def _out_ffn_kernel(x_ref, ret_ref, moba_ref, wo_ref, wg_ref, wu_ref, wd_ref,
                    g_post_mix_ref, g_pre_ffn_ref, g_post_ffn_ref, out_ref):
    mix_in = jnp.concatenate([ret_ref[...], moba_ref[...]], axis=1)
    x1 = x_ref[...] + _rms(_dot(mix_in, wo_ref[...]), g_post_mix_ref[...])
    h = _rms(x1, g_pre_ffn_ref[...]).astype(BF16)
    f = None
    for c in range(D_FF // FFN_CHUNK):
        cols = slice(c * FFN_CHUNK, (c + 1) * FFN_CHUNK)
        gate = _dot(h, wg_ref[:, cols])
        up = _dot(h, wu_ref[:, cols])
        act = (gate * jax.nn.sigmoid(gate) * up).astype(BF16)
        part = _dot(act, wd_ref[cols, :])
        f = part if f is None else f + part
    out_ref[...] = x1 + _rms(f, g_post_ffn_ref[...])


def _out_ffn(x2, ret2, moba2, wo, wg, wu, wd, g_post_mix, g_pre_ffn, g_post_ffn):
    n, d = x2.shape
    tm = min(FFN_TOKENS, n)
    tok = lambda w: pl.BlockSpec((tm, w), lambda t: (t, 0))
    resident = lambda a: pl.BlockSpec(a.shape, lambda t: (0, 0), pipeline_mode=pl.Buffered(1))
    return pl.pallas_call(
        _out_ffn_kernel,
        grid=(n // tm,),
        in_specs=[tok(d), tok(RET_WIDTH), tok(MOBA_WIDTH),
                  resident(wo), resident(wg), resident(wu), resident(wd),
                  resident(g_post_mix), resident(g_pre_ffn), resident(g_post_ffn)],
        out_specs=tok(d),
        out_shape=jax.ShapeDtypeStruct((n, d), F32),
        compiler_params=pltpu.CompilerParams(
            dimension_semantics=("arbitrary",), vmem_limit_bytes=VMEM_LIMIT),
        name="out_ffn",
    )(x2, ret2, moba2, wo, wg, wu, wd, g_post_mix, g_pre_ffn, g_post_ffn)


def _rotary_tables(s):
    half = RET_QK_DIM // 2
    inv_freq = ROPE_BASE ** (-jnp.arange(half, dtype=F32) / half)
    ang = jnp.arange(s, dtype=F32)[:, None] * inv_freq[None, :]
    cos, sin = jnp.cos(ang), jnp.sin(ang)
    cq = jnp.tile(jnp.concatenate([cos, cos], axis=1), (1, RET_HEADS))
    sq = jnp.tile(jnp.concatenate([-sin, sin], axis=1), (1, RET_HEADS))
    return cq, sq, cos.T, sin.T


def _retention_tables():
    c = RET_CHUNK
    log_gamma = jnp.log1p(-jnp.exp(jnp.linspace(math.log(1.0 / 32), math.log(1.0 / 512), RET_HEADS)))
    log_gamma = log_gamma.astype(F32)
    idx = jnp.arange(c, dtype=F32)
    diff = idx[:, None] - idx[None, :]
    dmat = jnp.where(diff >= 0, jnp.exp(jnp.maximum(diff, 0.0)[None] * log_gamma[:, None, None]), 0.0)
    q_decay = jnp.exp((idx + 1.0)[None, :] * log_gamma[:, None])
    k_decay = jnp.exp((c - 1.0 - idx)[None, :] * log_gamma[:, None])
    chunk_decay = jnp.exp(c * log_gamma)
    qd = jnp.broadcast_to(q_decay[:, :, None], (RET_HEADS, c, RET_V_DIM))
    kd = k_decay[:, None, :]
    cd = jnp.broadcast_to(chunk_decay[:, None, None], (RET_HEADS, 1, RET_V_DIM))
    return dmat, qd, kd, cd


def _t5_bucket(rel):
    n = jnp.maximum(rel, 0)
    max_exact = REL_BUCKETS // 2
    is_small = n < max_exact
    nf = jnp.maximum(n, 1).astype(F32)
    large = max_exact + (jnp.log(nf / max_exact) / math.log(REL_MAX_DIST / max_exact)
                         * (REL_BUCKETS - max_exact)).astype(jnp.int32)
    large = jnp.minimum(large, REL_BUCKETS - 1)
    return jnp.where(is_small, n, large)


def _bias_tables(rel_bias):
    bias_t = rel_bias.T.astype(F32)
    kk = jnp.arange(MOBA_BLOCK)[:, None]
    qq = jnp.arange(MOBA_BLOCK)[None, :]
    rel_own = qq - kk
    town = jnp.where(rel_own[None] >= 0, bias_t[:, _t5_bucket(rel_own)], NEG)
    tprev = bias_t[:, _t5_bucket(rel_own + MOBA_BLOCK)]
    cfar = bias_t[:, REL_BUCKETS - 1]
    return town, tprev, cfar


def kernel(x, w_in, w_out, pre_mix_norm, post_mix_norm, pre_ffn_norm, post_ffn_norm,
           rel_bias, w_gate, w_up, w_down):
    b, s, d = x.shape
    assert d == D_MODEL and s % IN_TOKENS == 0 and MOBA_BLOCK + 1 >= REL_MAX_DIST
    depth = w_in.shape[0]

    cq, sq, ckt, skt = _rotary_tables(s)
    dmat, qd, kd, cd = _retention_tables()
    town, tprev, cfar = _bias_tables(rel_bias)

    sizes = [RET_QK_WIDTH, RET_QK_WIDTH, RET_WIDTH, RET_WIDTH, MOBA_WIDTH, MOBA_WIDTH, MOBA_WIDTH]
    o_rq, o_rk, o_rv, o_rg, o_mq, o_mk, o_mv, _ = np.cumsum([0] + sizes).tolist()

    for layer in range(depth):
        w = w_in[layer]
        w_nat = jnp.concatenate(
            [w[:, o_rq:o_rk], w[:, o_rv:o_mq], w[:, o_mk:o_mv]], axis=1).astype(BF16)
        w_tr = jnp.concatenate([w[:, o_rk:o_rv], w[:, o_mq:o_mk], w[:, o_mv:]], axis=1).T.astype(BF16)

        rq, rv, sg, mk, kmean, rkt, mqt, mvt = _in_proj(
            x, pre_mix_norm[layer][None, :], w_nat, w_tr, cq, sq, ckt, skt)

        ret = _retention(rq, rkt, rv, sg, dmat, qd, kd, cd)

        nb = s // MOBA_BLOCK
        kmean = kmean.reshape(b, nb, MOBA_HEADS, MOBA_HEAD_DIM).transpose(0, 2, 1, 3)
        moba = _moba(cfar, mqt, mk, mvt, kmean, town, tprev)

        x = _out_ffn(
            x.reshape(b * s, d), ret.reshape(b * s, RET_WIDTH), moba.reshape(b * s, MOBA_WIDTH),
            w_out[layer].astype(BF16), w_gate[layer].astype(BF16), w_up[layer].astype(BF16),
            w_down[layer].astype(BF16), post_mix_norm[layer][None, :], pre_ffn_norm[layer][None, :],
            post_ffn_norm[layer][None, :]).reshape(b, s, d)
    return x
```

```python
import functools
import math

import jax
import jax.numpy as jnp
import numpy as np
from jax import lax
from jax.experimental import pallas as pl
from jax.experimental.pallas import tpu as pltpu

F32 = jnp.float32
BF16 = jnp.bfloat16

D_MODEL = 1024
RET_HEADS = 4
RET_QK_DIM = 64
RET_V_DIM = 128
RET_CHUNK = 128
RET_QK_WIDTH = RET_HEADS * RET_QK_DIM
RET_WIDTH = RET_HEADS * RET_V_DIM
MOBA_HEADS = 8
MOBA_HEAD_DIM = 64
MOBA_WIDTH = MOBA_HEADS * MOBA_HEAD_DIM
MOBA_BLOCK = 256
MOBA_TOPK = 3
REL_BUCKETS = 32
REL_MAX_DIST = 128
D_FF = 2816
EPS = 1e-6
ROPE_BASE = 10000.0

NEG = -1e30
LOG2E = math.log2(math.e)

IN_TOKENS = 512
RET_TOKENS = 1024
FFN_TOKENS = 512
FFN_CHUNK = 1408
VMEM_LIMIT = 56 * 1024 * 1024

NAT_WIDTH = RET_QK_WIDTH + 2 * RET_WIDTH + MOBA_WIDTH
TR_WIDTH = RET_QK_WIDTH + 2 * MOBA_WIDTH


def _nt_dot(a, b):
    return lax.dot_general(a, b, (((1,), (1,)), ((), ())), preferred_element_type=F32)


def _dot(a, b):
    return jnp.dot(a, b, preferred_element_type=F32)


def _in_proj_kernel(x_ref, g_ref, wn_ref, wt_ref, cq_ref, sq_ref, ckt_ref, skt_ref,
                    rq_ref, rv_ref, sg_ref, mk_ref, kmean_ref, rkt_ref, mqt_ref, mvt_ref):
    tm = x_ref.shape[1]
    x = x_ref[0]
    ms = jnp.mean(x * x, axis=-1, keepdims=True)
    h = (x * lax.rsqrt(ms + EPS) * g_ref[...]).astype(BF16)

    pn = _dot(h, wn_ref[...])
    pt = _nt_dot(wt_ref[...], h)

    rq = pn[:, :RET_QK_WIDTH]
    lane = lax.broadcasted_iota(jnp.int32, rq.shape, 1)
    first_half = (lane % RET_QK_DIM) < (RET_QK_DIM // 2)
    partner = jnp.where(first_half,
                        pltpu.roll(rq, RET_QK_WIDTH - RET_QK_DIM // 2, 1),
                        pltpu.roll(rq, RET_QK_DIM // 2, 1))
    rq_ref[0] = (rq * cq_ref[...] + partner * sq_ref[...]).astype(BF16)

    rv_ref[0] = pn[:, RET_QK_WIDTH:RET_QK_WIDTH + RET_WIDTH].astype(BF16)
    rg = pn[:, RET_QK_WIDTH + RET_WIDTH:RET_QK_WIDTH + 2 * RET_WIDTH]
    sg_ref[0] = (rg * jax.nn.sigmoid(rg)).astype(BF16)

    mk = pn[:, RET_QK_WIDTH + 2 * RET_WIDTH:]
    mk_ref[0] = mk.astype(BF16)
    for blk in range(tm // MOBA_BLOCK):
        kmean_ref[0, 0, blk:blk + 1, :] = jnp.mean(
            mk[blk * MOBA_BLOCK:(blk + 1) * MOBA_BLOCK], axis=0, keepdims=True)

    half = RET_QK_DIM // 2
    cos_t = ckt_ref[...]
    sin_t = skt_ref[...]
    parts = []
    for hd in range(RET_HEADS):
        x1 = pt[hd * RET_QK_DIM:hd * RET_QK_DIM + half]
        x2 = pt[hd * RET_QK_DIM + half:(hd + 1) * RET_QK_DIM]
        parts.append(x1 * cos_t - x2 * sin_t)
        parts.append(x2 * cos_t + x1 * sin_t)
    rkt = (jnp.concatenate(parts, axis=0) * (RET_QK_DIM ** -0.5)).astype(BF16)
    for c in range(tm // RET_CHUNK):
        rkt_ref[0, c] = rkt[:, c * RET_CHUNK:(c + 1) * RET_CHUNK]

    mqt = (pt[RET_QK_WIDTH:RET_QK_WIDTH + MOBA_WIDTH] * (MOBA_HEAD_DIM ** -0.5 * LOG2E)).astype(BF16)
    mvt = pt[RET_QK_WIDTH + MOBA_WIDTH:].astype(BF16)
    for blk in range(tm // MOBA_BLOCK):
        mqt_ref[0, blk] = mqt[:, blk * MOBA_BLOCK:(blk + 1) * MOBA_BLOCK]
        mvt_ref[0, blk] = mvt[:, blk * MOBA_BLOCK:(blk + 1) * MOBA_BLOCK]


def _in_proj(x, gain, w_nat, w_tr, cq, sq, ckt, skt):
    b, s, d = x.shape
    tm = IN_TOKENS
    ns = s // tm
    bpt = tm // MOBA_BLOCK
    cpt = tm // RET_CHUNK
    const = lambda si, bi: (0, 0)
    out_shape = (
        jax.ShapeDtypeStruct((b, s, RET_QK_WIDTH), BF16),
        jax.ShapeDtypeStruct((b, s, RET_WIDTH), BF16),
        jax.ShapeDtypeStruct((b, s, RET_WIDTH), BF16),
        jax.ShapeDtypeStruct((b, s, MOBA_WIDTH), BF16),
        jax.ShapeDtypeStruct((b, ns, bpt, MOBA_WIDTH), F32),
        jax.ShapeDtypeStruct((b, s // RET_CHUNK, RET_QK_WIDTH, RET_CHUNK), BF16),
        jax.ShapeDtypeStruct((b, s // MOBA_BLOCK, MOBA_WIDTH, MOBA_BLOCK), BF16),
        jax.ShapeDtypeStruct((b, s // MOBA_BLOCK, MOBA_WIDTH, MOBA_BLOCK), BF16),
    )
    tok = lambda w: pl.BlockSpec((1, tm, w), lambda si, bi: (bi, si, 0))
    return pl.pallas_call(
        _in_proj_kernel,
        grid=(ns, b),
        in_specs=[
            tok(d),
            pl.BlockSpec((1, d), const),
            pl.BlockSpec((d, NAT_WIDTH), const),
            pl.BlockSpec((TR_WIDTH, d), const),
            pl.BlockSpec((tm, RET_QK_WIDTH), lambda si, bi: (si, 0)),
            pl.BlockSpec((tm, RET_QK_WIDTH), lambda si, bi: (si, 0)),
            pl.BlockSpec((RET_QK_DIM // 2, tm), lambda si, bi: (0, si)),
            pl.BlockSpec((RET_QK_DIM // 2, tm), lambda si, bi: (0, si)),
        ],
        out_specs=(
            tok(RET_QK_WIDTH), tok(RET_WIDTH), tok(RET_WIDTH), tok(MOBA_WIDTH),
            pl.BlockSpec((1, 1, bpt, MOBA_WIDTH), lambda si, bi: (bi, si, 0, 0)),
            pl.BlockSpec((1, cpt, RET_QK_WIDTH, RET_CHUNK), lambda si, bi: (bi, si, 0, 0)),
            pl.BlockSpec((1, bpt, MOBA_WIDTH, MOBA_BLOCK), lambda si, bi: (bi, si, 0, 0)),
            pl.BlockSpec((1, bpt, MOBA_WIDTH, MOBA_BLOCK), lambda si, bi: (bi, si, 0, 0)),
        ),
        out_shape=out_shape,
        compiler_params=pltpu.CompilerParams(
            dimension_semantics=("arbitrary", "arbitrary"), vmem_limit_bytes=VMEM_LIMIT),
        name="in_proj",
    )(x, gain, w_nat, w_tr, cq, sq, ckt, skt)


def _retention_kernel(rq_ref, rkt_ref, rv_ref, sg_ref, dmat_ref, qd_ref, kd_ref, cd_ref,
                      out_ref, state_ref):
    @pl.when(pl.program_id(1) == 0)
    def _():
        state_ref[...] = jnp.zeros_like(state_ref)

    n_chunks = rq_ref.shape[1] // RET_CHUNK

    def chunk(c, carry):
        r0 = pl.multiple_of(c * RET_CHUNK, RET_CHUNK)
        rows = pl.ds(r0, RET_CHUNK)
        for hd in range(RET_HEADS):
            q = rq_ref[0, rows, hd * RET_QK_DIM:(hd + 1) * RET_QK_DIM]
            kt = rkt_ref[0, c, hd * RET_QK_DIM:(hd + 1) * RET_QK_DIM, :]
            v = rv_ref[0, rows, hd * RET_V_DIM:(hd + 1) * RET_V_DIM]
            state = state_ref[hd]
            scores = _dot(q, kt) * dmat_ref[hd]
            o = _dot(scores.astype(BF16), v) + qd_ref[hd] * _dot(q, state.astype(BF16))
            kts = (kt.astype(F32) * kd_ref[hd]).astype(BF16)
            state_ref[hd] = state * cd_ref[hd] + _dot(kts, v)
            ms = jnp.mean(o * o, axis=-1, keepdims=True)
            gate = sg_ref[0, rows, hd * RET_V_DIM:(hd + 1) * RET_V_DIM].astype(F32)
            out_ref[0, rows, hd * RET_V_DIM:(hd + 1) * RET_V_DIM] = (
                o * lax.rsqrt(ms + EPS) * gate).astype(BF16)
        return carry

    lax.fori_loop(0, n_chunks, chunk, 0)


def _retention(rq, rkt, rv, sg, dmat, qd, kd, cd):
    b, s, _ = rq.shape
    tc = min(RET_TOKENS, s)
    tok = lambda w: pl.BlockSpec((1, tc, w), lambda bi, si: (bi, si, 0))
    tab = lambda a: pl.BlockSpec(a.shape, lambda bi, si: (0,) * a.ndim)
    return pl.pallas_call(
        _retention_kernel,
        grid=(b, s // tc),
        in_specs=[
            tok(RET_QK_WIDTH),
            pl.BlockSpec((1, tc // RET_CHUNK, RET_QK_WIDTH, RET_CHUNK), lambda bi, si: (bi, si, 0, 0)),
            tok(RET_WIDTH), tok(RET_WIDTH),
            tab(dmat), tab(qd), tab(kd), tab(cd),
        ],
        out_specs=tok(RET_WIDTH),
        out_shape=jax.ShapeDtypeStruct((b, s, RET_WIDTH), BF16),
        scratch_shapes=[pltpu.VMEM((RET_HEADS, RET_QK_DIM, RET_V_DIM), F32)],
        compiler_params=pltpu.CompilerParams(
            dimension_semantics=("arbitrary", "arbitrary"), vmem_limit_bytes=VMEM_LIMIT),
        name="retention",
    )(rq, rkt, rv, sg, dmat, qd, kd, cd)


ONES_ROWS = 16
MOBA_STEP_HEADS = 4
PAIR_WIDTH = 2 * MOBA_HEAD_DIM


def _moba_gate_kernel(cfar_ref, qt_ref, km_ref, selfar_ref, selprev_ref):
    i = pl.program_id(1)
    d = MOBA_HEAD_DIM
    nb = km_ref.shape[2]
    for h in range(MOBA_HEADS):
        qt = qt_ref[0, 0, h * d:(h + 1) * d, :]
        km = km_ref[0, h]
        km_hi = km.astype(BF16)
        km_lo = (km - km_hi.astype(F32)).astype(BF16)
        gate = _dot(km_hi, qt) + _dot(km_lo, qt)
        blk = lax.broadcasted_iota(jnp.int32, gate.shape, 0)
        gate = jnp.where(blk < i, gate, NEG)
        blk_f = blk.astype(F32)
        chosen = jnp.zeros(gate.shape, F32)
        for _ in range(MOBA_TOPK):
            best = jnp.max(gate, axis=0, keepdims=True)
            idx = jnp.min(jnp.where(gate == best, blk_f, float(nb)), axis=0, keepdims=True)
            hit = blk_f == idx
            chosen = jnp.where(hit & (best > 0.5 * NEG), 1.0, chosen)
            gate = jnp.where(hit, NEG, gate)
        picked = chosen > 0.5
        selfar_ref[0, h] = jnp.where(picked & (blk < i - 1), cfar_ref[h], NEG).astype(BF16)
        prev_hit = jnp.max(jnp.where(picked & (blk == i - 1), 1.0, 0.0), axis=0, keepdims=True)
        selprev_ref[0, h] = jnp.where(prev_hit > 0.5, 0.0, NEG)


def _moba_gate(cfar, mqt, kmean):
    b, nb, _, _ = mqt.shape
    s = nb * MOBA_BLOCK
    return pl.pallas_call(
        _moba_gate_kernel,
        grid=(b, nb),
        in_specs=[
            pl.BlockSpec(memory_space=pltpu.SMEM),
            pl.BlockSpec((1, 1, MOBA_WIDTH, MOBA_BLOCK), lambda bi, i: (bi, i, 0, 0)),
            pl.BlockSpec((1, MOBA_HEADS, nb, MOBA_HEAD_DIM), lambda bi, i: (bi, 0, 0, 0)),
        ],
        out_specs=(
            pl.BlockSpec((1, MOBA_HEADS, nb, MOBA_BLOCK), lambda bi, i: (bi, 0, 0, i)),
            pl.BlockSpec((1, MOBA_HEADS, 1, MOBA_BLOCK), lambda bi, i: (bi, 0, 0, i)),
        ),
        out_shape=(
            jax.ShapeDtypeStruct((b, MOBA_HEADS, nb, s), BF16),
            jax.ShapeDtypeStruct((b, MOBA_HEADS, 1, s), F32),
        ),
        compiler_params=pltpu.CompilerParams(
            dimension_semantics=("arbitrary", "arbitrary"), vmem_limit_bytes=VMEM_LIMIT),
        name="moba_gate",
    )(cfar, mqt, kmean)


def _moba_kernel(qt_ref, k_ref, vt_ref, selfar_ref, selprev_ref, tnear_ref,
                 out_ref, kaug_ref, vaug_ref, qaug_ref, s_ref, m_ref, acc_ref):
    i = pl.program_id(2)
    d = MOBA_HEAD_DIM
    nb = vt_ref.shape[1]
    s_len = k_ref.shape[1]
    blk_rows = MOBA_BLOCK

    @pl.when(i == 0)
    def _():
        row_blk = lax.broadcasted_iota(jnp.int32, (s_len, PAIR_WIDTH), 0) // MOBA_BLOCK
        lane = lax.broadcasted_iota(jnp.int32, (s_len, PAIR_WIDTH), 1)
        block_one_hot = jnp.where(row_blk == lane, 1.0, 0.0).astype(BF16)
        for pe in range(MOBA_STEP_HEADS // 2):
            kaug_ref[pe, :, 0:PAIR_WIDTH] = k_ref[0, :, pe * PAIR_WIDTH:(pe + 1) * PAIR_WIDTH]
            kaug_ref[pe, :, PAIR_WIDTH:] = block_one_hot
        for e in range(MOBA_STEP_HEADS):
            vaug_ref[e, :, 0:d, :] = vt_ref[0, :, e * d:(e + 1) * d, :]
            vaug_ref[e, :, d:, :] = jnp.ones((nb, ONES_ROWS, MOBA_BLOCK), BF16)

    prow = lax.broadcasted_iota(jnp.int32, (PAIR_WIDTH, MOBA_BLOCK), 0)
    for e in range(MOBA_STEP_HEADS):
        pe, he = divmod(e, 2)
        qt2 = qt_ref[0, 0, pe * PAIR_WIDTH:(pe + 1) * PAIR_WIDTH, :]
        qaug_ref[e, 0:PAIR_WIDTH, :] = jnp.where(
            (prow >= he * d) & (prow < (he + 1) * d), qt2, jnp.zeros_like(qt2))
        qaug_ref[e, PAIR_WIDTH:PAIR_WIDTH + nb, :] = selfar_ref[0, e]
        qaug_ref[e, PAIR_WIDTH + nb:, :] = jnp.zeros((PAIR_WIDTH - nb, MOBA_BLOCK), BF16)

    def qk(j, slot, near):
        rows = pl.ds(pl.multiple_of(j * blk_rows, blk_rows), blk_rows)
        for e in range(MOBA_STEP_HEADS):
            if near:
                s_ref[slot, e] = _dot(kaug_ref[e // 2, rows, 0:PAIR_WIDTH], qaug_ref[e, 0:PAIR_WIDTH, :])
            else:
                s_ref[slot, e] = _dot(kaug_ref[e // 2, rows, :], qaug_ref[e])

    def sm(j, slot, bias=None, first=False):
        for e in range(MOBA_STEP_HEADS):
            st = s_ref[slot, e]
            if bias is not None:
                st = st + bias(e)
            mj = jnp.max(st, axis=0, keepdims=True)
            if first:
                m_ref[e] = mj
                acc_ref[e] = _dot(vaug_ref[e, j], jnp.exp2(st - mj).astype(BF16))
            else:
                m_old = m_ref[e]
                m_new = jnp.maximum(m_old, mj)
                pv = _dot(vaug_ref[e, j], jnp.exp2(st - m_new).astype(BF16))
                acc_ref[e] = acc_ref[e] * jnp.exp2(m_old - m_new) + pv
                m_ref[e] = m_new

    jp = jnp.maximum(i - 1, 0)
    qk(i, 0, near=True)
    qk(jp, 1, near=True)
    sm(i, 0, bias=lambda e: tnear_ref[e, blk_rows:, :], first=True)
    qk(0, 0, near=False)
    sm(jp, 1, bias=lambda e: tnear_ref[e, 0:blk_rows, :] + selprev_ref[0, e])

    def far(t, carry):
        j = 2 * t
        qk(j + 1, 1, near=False)
        sm(j, 0)
        qk(j + 2, 0, near=False)
        sm(j + 1, 1)
        return carry

    lax.fori_loop(0, lax.shift_right_logical(i, 1), far, 0)

    outs = []
    for e in range(MOBA_STEP_HEADS):
        acc = acc_ref[e]
        outs.append((acc[:d] / acc[d:d + 1]).T)
    out_ref[0] = jnp.concatenate(outs, axis=1).astype(BF16)


def _moba(mqt, mk, mvt, selfar, selprev, tnear):
    b, s, _ = mk.shape
    nb = s // MOBA_BLOCK
    assert nb <= PAIR_WIDTH
    sh = MOBA_STEP_HEADS
    sw = sh * MOBA_HEAD_DIM
    return pl.pallas_call(
        _moba_kernel,
        grid=(b, MOBA_HEADS // sh, nb),
        in_specs=[
            pl.BlockSpec((1, 1, sw, MOBA_BLOCK), lambda bi, hg, i: (bi, i, hg, 0)),
            pl.BlockSpec((1, s, sw), lambda bi, hg, i: (bi, 0, hg)),
            pl.BlockSpec((1, nb, sw, MOBA_BLOCK), lambda bi, hg, i: (bi, 0, hg, 0)),
            pl.BlockSpec((1, sh, nb, MOBA_BLOCK), lambda bi, hg, i: (bi, hg, 0, i)),
            pl.BlockSpec((1, sh, 1, MOBA_BLOCK), lambda bi, hg, i: (bi, hg, 0, i)),
            pl.BlockSpec((sh, 2 * MOBA_BLOCK, MOBA_BLOCK), lambda bi, hg, i: (hg, 0, 0)),
        ],
        out_specs=pl.BlockSpec((1, MOBA_BLOCK, sw), lambda bi, hg, i: (bi, i, hg)),
        out_shape=jax.ShapeDtypeStruct((b, s, MOBA_WIDTH), BF16),
        scratch_shapes=[
            pltpu.VMEM((sh // 2, s, 2 * PAIR_WIDTH), BF16),
            pltpu.VMEM((sh, nb, MOBA_HEAD_DIM + ONES_ROWS, MOBA_BLOCK), BF16),
            pltpu.VMEM((sh, 2 * PAIR_WIDTH, MOBA_BLOCK), BF16),
            pltpu.VMEM((2, sh, MOBA_BLOCK, MOBA_BLOCK), F32),
            pltpu.VMEM((sh, 1, MOBA_BLOCK), F32),
            pltpu.VMEM((sh, MOBA_HEAD_DIM + ONES_ROWS, MOBA_BLOCK), F32),
        ],
        compiler_params=pltpu.CompilerParams(
            dimension_semantics=("arbitrary", "arbitrary", "arbitrary"),
            vmem_limit_bytes=VMEM_LIMIT),
        name="moba",
    )(mqt, mk, mvt, selfar, selprev, tnear)


def _rms(x, g):
    return x * lax.rsqrt(jnp.mean(x * x, axis=-1, keepdims=True) + EPS) * g


def _out_ffn_kernel(x_ref, ret_ref, moba_ref, wo_ref, wg_ref, wu_ref, wd_ref,
                    g_post_mix_ref, g_pre_ffn_ref, g_post_ffn_ref, out_ref):
    mix_in = jnp.concatenate([ret_ref[...], moba_ref[...]], axis=1)
    x1 = x_ref[...] + _rms(_dot(mix_in, wo_ref[...]), g_post_mix_ref[...])
    h = _rms(x1, g_pre_ffn_ref[...]).astype(BF16)
    f = None
    for c in range(D_FF // FFN_CHUNK):
        cols = slice(c * FFN_CHUNK, (c + 1) * FFN_CHUNK)
        gate = _dot(h, wg_ref[:, cols])
        up = _dot(h, wu_ref[:, cols])
        act = (gate * jax.nn.sigmoid(gate) * up).astype(BF16)
        part = _dot(act, wd_ref[cols, :])
        f = part if f is None else f + part
    out_ref[...] = x1 + _rms(f, g_post_ffn_ref[...])


def _out_ffn(x2, ret2, moba2, wo, wg, wu, wd, g_post_mix, g_pre_ffn, g_post_ffn):
    n, d = x2.shape
    tm = min(FFN_TOKENS, n)
    tok = lambda w: pl.BlockSpec((tm, w), lambda t: (t, 0))
    resident = lambda a: pl.BlockSpec(a.shape, lambda t: (0, 0), pipeline_mode=pl.Buffered(1))
    return pl.pallas_call(
        _out_ffn_kernel,
        grid=(n // tm,),
        in_specs=[tok(d), tok(RET_WIDTH), tok(MOBA_WIDTH),
                  resident(wo), resident(wg), resident(wu), resident(wd),
                  resident(g_post_mix), resident(g_pre_ffn), resident(g_post_ffn)],
        out_specs=tok(d),
        out_shape=jax.ShapeDtypeStruct((n, d), F32),
        compiler_params=pltpu.CompilerParams(
            dimension_semantics=("arbitrary",), vmem_limit_bytes=VMEM_LIMIT),
        name="out_ffn",
    )(x2, ret2, moba2, wo, wg, wu, wd, g_post_mix, g_pre_ffn, g_post_ffn)


def _rotary_tables(s):
    half = RET_QK_DIM // 2
    inv_freq = ROPE_BASE ** (-jnp.arange(half, dtype=F32) / half)
    ang = jnp.arange(s, dtype=F32)[:, None] * inv_freq[None, :]
    cos, sin = jnp.cos(ang), jnp.sin(ang)
    cq = jnp.tile(jnp.concatenate([cos, cos], axis=1), (1, RET_HEADS))
    sq = jnp.tile(jnp.concatenate([-sin, sin], axis=1), (1, RET_HEADS))
    return cq, sq, cos.T, sin.T


def _retention_tables():
    c = RET_CHUNK
    log_gamma = jnp.log1p(-jnp.exp(jnp.linspace(math.log(1.0 / 32), math.log(1.0 / 512), RET_HEADS)))
    log_gamma = log_gamma.astype(F32)
    idx = jnp.arange(c, dtype=F32)
    diff = idx[:, None] - idx[None, :]
    dmat = jnp.where(diff >= 0, jnp.exp(jnp.maximum(diff, 0.0)[None] * log_gamma[:, None, None]), 0.0)
    q_decay = jnp.exp((idx + 1.0)[None, :] * log_gamma[:, None])
    k_decay = jnp.exp((c - 1.0 - idx)[None, :] * log_gamma[:, None])
    chunk_decay = jnp.exp(c * log_gamma)
    qd = jnp.broadcast_to(q_decay[:, :, None], (RET_HEADS, c, RET_V_DIM))
    kd = k_decay[:, None, :]
    cd = jnp.broadcast_to(chunk_decay[:, None, None], (RET_HEADS, 1, RET_V_DIM))
    return dmat, qd, kd, cd


def _t5_bucket(rel):
    n = jnp.maximum(rel, 0)
    max_exact = REL_BUCKETS // 2
    is_small = n < max_exact
    nf = jnp.maximum(n, 1).astype(F32)
    large = max_exact + (jnp.log(nf / max_exact) / math.log(REL_MAX_DIST / max_exact)
                         * (REL_BUCKETS - max_exact)).astype(jnp.int32)
    large = jnp.minimum(large, REL_BUCKETS - 1)
    return jnp.where(is_small, n, large)


def _bias_tables(rel_bias):
    bias_t = rel_bias.T.astype(F32) * LOG2E
    kk = jnp.arange(MOBA_BLOCK)[:, None]
    qq = jnp.arange(MOBA_BLOCK)[None, :]
    rel_own = qq - kk

    def lookup(rel):
        one_hot = (_t5_bucket(rel)[..., None] == jnp.arange(REL_BUCKETS)).astype(F32)
        return jnp.einsum("kqn,hn->hkq", one_hot, bias_t, precision=lax.Precision.HIGHEST)

    town = jnp.where(rel_own[None] >= 0, lookup(rel_own), NEG)
    tprev = lookup(rel_own + MOBA_BLOCK)
    cfar = bias_t[:, REL_BUCKETS - 1]
    return jnp.concatenate([tprev, town], axis=1), cfar


def kernel(x, w_in, w_out, pre_mix_norm, post_mix_norm, pre_ffn_norm, post_ffn_norm,
           rel_bias, w_gate, w_up, w_down):
    b, s, d = x.shape
    assert d == D_MODEL and s % IN_TOKENS == 0 and MOBA_BLOCK + 1 >= REL_MAX_DIST
    depth = w_in.shape[0]

    cq, sq, ckt, skt = _rotary_tables(s)
    dmat, qd, kd, cd = _retention_tables()
    tnear, cfar = _bias_tables(rel_bias)

    sizes = [RET_QK_WIDTH, RET_QK_WIDTH, RET_WIDTH, RET_WIDTH, MOBA_WIDTH, MOBA_WIDTH, MOBA_WIDTH]
    o_rq, o_rk, o_rv, o_rg, o_mq, o_mk, o_mv, _ = np.cumsum([0] + sizes).tolist()

    for layer in range(depth):
        w = w_in[layer]
        w_nat = jnp.concatenate(
            [w[:, o_rq:o_rk], w[:, o_rv:o_mq], w[:, o_mk:o_mv]], axis=1).astype(BF16)
        w_tr = jnp.concatenate([w[:, o_rk:o_rv], w[:, o_mq:o_mk], w[:, o_mv:]], axis=1).T.astype(BF16)

        rq, rv, sg, mk, kmean, rkt, mqt, mvt = _in_proj(
            x, pre_mix_norm[layer][None, :], w_nat, w_tr, cq, sq, ckt, skt)

        ret = _retention(rq, rkt, rv, sg, dmat, qd, kd, cd)

        nb = s // MOBA_BLOCK
        kmean = kmean.reshape(b, nb, MOBA_HEADS, MOBA_HEAD_DIM).transpose(0, 2, 1, 3)
        selfar, selprev = _moba_gate(cfar, mqt, kmean)
        moba = _moba(mqt, mk, mvt, selfar, selprev, tnear)

        x = _out_ffn(
            x.reshape(b * s, d), ret.reshape(b * s, RET_WIDTH), moba.reshape(b * s, MOBA_WIDTH),
            w_out[layer].astype(BF16), w_gate[layer].astype(BF16), w_up[layer].astype(BF16),
            w_down[layer].astype(BF16), post_mix_norm[layer][None, :], pre_ffn_norm[layer][None, :],
            post_ffn_norm[layer][None, :]).reshape(b, s, d)
    return x
```

```python
import functools
import math

import jax
import jax.numpy as jnp
import numpy as np
from jax import lax
from jax.experimental import pallas as pl
from jax.experimental.pallas import tpu as pltpu

F32 = jnp.float32
BF16 = jnp.bfloat16

D_MODEL = 1024
RET_HEADS = 4
RET_QK_DIM = 64
RET_V_DIM = 128
RET_CHUNK = 128
RET_QK_WIDTH = RET_HEADS * RET_QK_DIM
RET_WIDTH = RET_HEADS * RET_V_DIM
MOBA_HEADS = 8
MOBA_HEAD_DIM = 64
MOBA_WIDTH = MOBA_HEADS * MOBA_HEAD_DIM
MOBA_BLOCK = 256
MOBA_TOPK = 3
REL_BUCKETS = 32
REL_MAX_DIST = 128
D_FF = 2816
EPS = 1e-6
ROPE_BASE = 10000.0

NEG = -1e30
LOG2E = math.log2(math.e)

IN_TOKENS = 512
RET_TOKENS = 1024
FFN_TOKENS = 512
FFN_CHUNK = 1408
VMEM_LIMIT = 56 * 1024 * 1024

NAT_WIDTH = RET_QK_WIDTH + 2 * RET_WIDTH + MOBA_WIDTH
TR_WIDTH = RET_QK_WIDTH + 2 * MOBA_WIDTH


def _nt_dot(a, b):
    return lax.dot_general(a, b, (((1,), (1,)), ((), ())), preferred_element_type=F32)


def _dot(a, b):
    return jnp.dot(a, b, preferred_element_type=F32)


def _in_proj_kernel(x_ref, g_ref, wn_ref, wt_ref, cq_ref, sq_ref, ckt_ref, skt_ref,
                    rq_ref, rv_ref, sg_ref, mk_ref, kmean_ref, rkt_ref, mqt_ref, mvt_ref):
    tm = x_ref.shape[1]
    x = x_ref[0]
    ms = jnp.mean(x * x, axis=-1, keepdims=True)
    h = (x * lax.rsqrt(ms + EPS) * g_ref[...]).astype(BF16)

    pn = _dot(h, wn_ref[...])
    pt = _nt_dot(wt_ref[...], h)

    rq = pn[:, :RET_QK_WIDTH]
    lane = lax.broadcasted_iota(jnp.int32, rq.shape, 1)
    first_half = (lane % RET_QK_DIM) < (RET_QK_DIM // 2)
    partner = jnp.where(first_half,
                        pltpu.roll(rq, RET_QK_WIDTH - RET_QK_DIM // 2, 1),
                        pltpu.roll(rq, RET_QK_DIM // 2, 1))
    rq_ref[0] = (rq * cq_ref[...] + partner * sq_ref[...]).astype(BF16)

    rv_ref[0] = pn[:, RET_QK_WIDTH:RET_QK_WIDTH + RET_WIDTH].astype(BF16)
    rg = pn[:, RET_QK_WIDTH + RET_WIDTH:RET_QK_WIDTH + 2 * RET_WIDTH]
    sg_ref[0] = (rg * jax.nn.sigmoid(rg)).astype(BF16)

    mk = pn[:, RET_QK_WIDTH + 2 * RET_WIDTH:]
    mk_ref[0] = mk.astype(BF16)
    for blk in range(tm // MOBA_BLOCK):
        kmean_ref[0, 0, blk:blk + 1, :] = jnp.mean(
            mk[blk * MOBA_BLOCK:(blk + 1) * MOBA_BLOCK], axis=0, keepdims=True)

    half = RET_QK_DIM // 2
    cos_t = ckt_ref[...]
    sin_t = skt_ref[...]
    parts = []
    for hd in range(RET_HEADS):
        x1 = pt[hd * RET_QK_DIM:hd * RET_QK_DIM + half]
        x2 = pt[hd * RET_QK_DIM + half:(hd + 1) * RET_QK_DIM]
        parts.append(x1 * cos_t - x2 * sin_t)
        parts.append(x2 * cos_t + x1 * sin_t)
    rkt = (jnp.concatenate(parts, axis=0) * (RET_QK_DIM ** -0.5)).astype(BF16)
    for c in range(tm // RET_CHUNK):
        rkt_ref[0, c] = rkt[:, c * RET_CHUNK:(c + 1) * RET_CHUNK]

    mqt = (pt[RET_QK_WIDTH:RET_QK_WIDTH + MOBA_WIDTH] * (MOBA_HEAD_DIM ** -0.5 * LOG2E)).astype(BF16)
    mvt = pt[RET_QK_WIDTH + MOBA_WIDTH:].astype(BF16)
    for blk in range(tm // MOBA_BLOCK):
        mqt_ref[0, blk] = mqt[:, blk * MOBA_BLOCK:(blk + 1) * MOBA_BLOCK]
        mvt_ref[0, blk] = mvt[:, blk * MOBA_BLOCK:(blk + 1) * MOBA_BLOCK]


def _in_proj(x, gain, w_nat, w_tr, cq, sq, ckt, skt):
    b, s, d = x.shape
    tm = IN_TOKENS
    ns = s // tm
    bpt = tm // MOBA_BLOCK
    cpt = tm // RET_CHUNK
    const = lambda si, bi: (0, 0)
    out_shape = (
        jax.ShapeDtypeStruct((b, s, RET_QK_WIDTH), BF16),
        jax.ShapeDtypeStruct((b, s, RET_WIDTH), BF16),
        jax.ShapeDtypeStruct((b, s, RET_WIDTH), BF16),
        jax.ShapeDtypeStruct((b, s, MOBA_WIDTH), BF16),
        jax.ShapeDtypeStruct((b, ns, bpt, MOBA_WIDTH), F32),
        jax.ShapeDtypeStruct((b, s // RET_CHUNK, RET_QK_WIDTH, RET_CHUNK), BF16),
        jax.ShapeDtypeStruct((b, s // MOBA_BLOCK, MOBA_WIDTH, MOBA_BLOCK), BF16),
        jax.ShapeDtypeStruct((b, s // MOBA_BLOCK, MOBA_WIDTH, MOBA_BLOCK), BF16),
    )
    tok = lambda w: pl.BlockSpec((1, tm, w), lambda si, bi: (bi, si, 0))
    return pl.pallas_call(
        _in_proj_kernel,
        grid=(ns, b),
        in_specs=[
            tok(d),
            pl.BlockSpec((1, d), const),
            pl.BlockSpec((d, NAT_WIDTH), const),
            pl.BlockSpec((TR_WIDTH, d), const),
            pl.BlockSpec((tm, RET_QK_WIDTH), lambda si, bi: (si, 0)),
            pl.BlockSpec((tm, RET_QK_WIDTH), lambda si, bi: (si, 0)),
            pl.BlockSpec((RET_QK_DIM // 2, tm), lambda si, bi: (0, si)),
            pl.BlockSpec((RET_QK_DIM // 2, tm), lambda si, bi: (0, si)),
        ],
        out_specs=(
            tok(RET_QK_WIDTH), tok(RET_WIDTH), tok(RET_WIDTH), tok(MOBA_WIDTH),
            pl.BlockSpec((1, 1, bpt, MOBA_WIDTH), lambda si, bi: (bi, si, 0, 0)),
            pl.BlockSpec((1, cpt, RET_QK_WIDTH, RET_CHUNK), lambda si, bi: (bi, si, 0, 0)),
            pl.BlockSpec((1, bpt, MOBA_WIDTH, MOBA_BLOCK), lambda si, bi: (bi, si, 0, 0)),
            pl.BlockSpec((1, bpt, MOBA_WIDTH, MOBA_BLOCK), lambda si, bi: (bi, si, 0, 0)),
        ),
        out_shape=out_shape,
        compiler_params=pltpu.CompilerParams(
            dimension_semantics=("arbitrary", "arbitrary"), vmem_limit_bytes=VMEM_LIMIT),
        name="in_proj",
    )(x, gain, w_nat, w_tr, cq, sq, ckt, skt)


def _retention_kernel(rq_ref, rkt_ref, rv_ref, sg_ref, dmat_ref, qd_ref, kd_ref, cd_ref,
                      out_ref, state_ref):
    @pl.when(pl.program_id(1) == 0)
    def _():
        state_ref[...] = jnp.zeros_like(state_ref)

    n_chunks = rq_ref.shape[1] // RET_CHUNK

    def chunk(c, carry):
        r0 = pl.multiple_of(c * RET_CHUNK, RET_CHUNK)
        rows = pl.ds(r0, RET_CHUNK)
        for hd in range(RET_HEADS):
            q = rq_ref[0, rows, hd * RET_QK_DIM:(hd + 1) * RET_QK_DIM]
            kt = rkt_ref[0, c, hd * RET_QK_DIM:(hd + 1) * RET_QK_DIM, :]
            v = rv_ref[0, rows, hd * RET_V_DIM:(hd + 1) * RET_V_DIM]
            state = state_ref[hd]
            scores = _dot(q, kt) * dmat_ref[hd]
            o = _dot(scores.astype(BF16), v) + qd_ref[hd] * _dot(q, state.astype(BF16))
            kts = (kt.astype(F32) * kd_ref[hd]).astype(BF16)
            state_ref[hd] = state * cd_ref[hd] + _dot(kts, v)
            ms = jnp.mean(o * o, axis=-1, keepdims=True)
            gate = sg_ref[0, rows, hd * RET_V_DIM:(hd + 1) * RET_V_DIM].astype(F32)
            out_ref[0, rows, hd * RET_V_DIM:(hd + 1) * RET_V_DIM] = (
                o * lax.rsqrt(ms + EPS) * gate).astype(BF16)
        return carry

    lax.fori_loop(0, n_chunks, chunk, 0)


def _retention(rq, rkt, rv, sg, dmat, qd, kd, cd):
    b, s, _ = rq.shape
    tc = min(RET_TOKENS, s)
    tok = lambda w: pl.BlockSpec((1, tc, w), lambda bi, si: (bi, si, 0))
    tab = lambda a: pl.BlockSpec(a.shape, lambda bi, si: (0,) * a.ndim)
    return pl.pallas_call(
        _retention_kernel,
        grid=(b, s // tc),
        in_specs=[
            tok(RET_QK_WIDTH),
            pl.BlockSpec((1, tc // RET_CHUNK, RET_QK_WIDTH, RET_CHUNK), lambda bi, si: (bi, si, 0, 0)),
            tok(RET_WIDTH), tok(RET_WIDTH),
            tab(dmat), tab(qd), tab(kd), tab(cd),
        ],
        out_specs=tok(RET_WIDTH),
        out_shape=jax.ShapeDtypeStruct((b, s, RET_WIDTH), BF16),
        scratch_shapes=[pltpu.VMEM((RET_HEADS, RET_QK_DIM, RET_V_DIM), F32)],
        compiler_params=pltpu.CompilerParams(
            dimension_semantics=("arbitrary", "arbitrary"), vmem_limit_bytes=VMEM_LIMIT),
        name="retention",
    )(rq, rkt, rv, sg, dmat, qd, kd, cd)


ONES_ROWS = 16
PAIR_WIDTH = 2 * MOBA_HEAD_DIM


def _moba_gate_kernel(cfar_ref, qt_ref, km_ref, selfar_ref, selprev_ref):
    i = pl.program_id(1)
    d = MOBA_HEAD_DIM
    nb = km_ref.shape[2]
    for h in range(MOBA_HEADS):
        qt = qt_ref[0, 0, h * d:(h + 1) * d, :]
        km = km_ref[0, h]
        km_hi = km.astype(BF16)
        km_lo = (km - km_hi.astype(F32)).astype(BF16)
        gate = _dot(km_hi, qt) + _dot(km_lo, qt)
        blk = lax.broadcasted_iota(jnp.int32, gate.shape, 0)
        gate = jnp.where(blk < i, gate, NEG)
        blk_f = blk.astype(F32)
        chosen = jnp.zeros(gate.shape, F32)
        for _ in range(MOBA_TOPK):
            best = jnp.max(gate, axis=0, keepdims=True)
            idx = jnp.min(jnp.where(gate == best, blk_f, float(nb)), axis=0, keepdims=True)
            hit = blk_f == idx
            chosen = jnp.where(hit & (best > 0.5 * NEG), 1.0, chosen)
            gate = jnp.where(hit, NEG, gate)
        picked = chosen > 0.5
        selfar_ref[0, h] = jnp.where(picked & (blk < i - 1), cfar_ref[h], NEG).astype(BF16)
        prev_hit = jnp.max(jnp.where(picked & (blk == i - 1), 1.0, 0.0), axis=0, keepdims=True)
        selprev_ref[0, h] = jnp.where(prev_hit > 0.5, 0.0, NEG)


def _moba_gate(cfar, mqt, kmean):
    b, nb, _, _ = mqt.shape
    s = nb * MOBA_BLOCK
    return pl.pallas_call(
        _moba_gate_kernel,
        grid=(b, nb),
        in_specs=[
            pl.BlockSpec(memory_space=pltpu.SMEM),
            pl.BlockSpec((1, 1, MOBA_WIDTH, MOBA_BLOCK), lambda bi, i: (bi, i, 0, 0)),
            pl.BlockSpec((1, MOBA_HEADS, nb, MOBA_HEAD_DIM), lambda bi, i: (bi, 0, 0, 0)),
        ],
        out_specs=(
            pl.BlockSpec((1, MOBA_HEADS, nb, MOBA_BLOCK), lambda bi, i: (bi, 0, 0, i)),
            pl.BlockSpec((1, MOBA_HEADS, 1, MOBA_BLOCK), lambda bi, i: (bi, 0, 0, i)),
        ),
        out_shape=(
            jax.ShapeDtypeStruct((b, MOBA_HEADS, nb, s), BF16),
            jax.ShapeDtypeStruct((b, MOBA_HEADS, 1, s), F32),
        ),
        compiler_params=pltpu.CompilerParams(
            dimension_semantics=("arbitrary", "arbitrary"), vmem_limit_bytes=VMEM_LIMIT),
        name="moba_gate",
    )(cfar, mqt, kmean)


def _moba_kernel(qt_ref, k_ref, vt_ref, selfar_ref, selprev_ref, tnear_ref,
                 out_ref, kaug_ref, vaug_ref, qaug_ref, s_ref, m_ref, acc_ref):
    i = pl.program_id(1)
    d = MOBA_HEAD_DIM
    nb = vaug_ref.shape[1]
    s_len = kaug_ref.shape[1]
    blk_rows = MOBA_BLOCK
    n_pairs = MOBA_HEADS // 2

    @pl.when((pl.program_id(0) == 0) & (i == 0))
    def _():
        row_blk = lax.broadcasted_iota(jnp.int32, (s_len, PAIR_WIDTH), 0) // MOBA_BLOCK
        lane = lax.broadcasted_iota(jnp.int32, (s_len, PAIR_WIDTH), 1)
        block_one_hot = jnp.where(row_blk == lane, 1.0, 0.0).astype(BF16)
        for pe in range(n_pairs):
            kaug_ref[pe, :, PAIR_WIDTH:] = block_one_hot
        for e in range(MOBA_HEADS):
            vaug_ref[e, :, d:, :] = jnp.ones((nb, ONES_ROWS, MOBA_BLOCK), BF16)

    own_rows = pl.ds(pl.multiple_of(i * blk_rows, blk_rows), blk_rows)
    for pe in range(n_pairs):
        kaug_ref[pe, own_rows, 0:PAIR_WIDTH] = k_ref[0, :, pe * PAIR_WIDTH:(pe + 1) * PAIR_WIDTH]
    for e in range(MOBA_HEADS):
        vaug_ref[e, i, 0:d, :] = vt_ref[0, 0, e * d:(e + 1) * d, :]

    prow = lax.broadcasted_iota(jnp.int32, (PAIR_WIDTH, MOBA_BLOCK), 0)
    for e in range(MOBA_HEADS):
        pe, he = divmod(e, 2)
        qt2 = qt_ref[0, 0, pe * PAIR_WIDTH:(pe + 1) * PAIR_WIDTH, :]
        qaug_ref[e, 0:PAIR_WIDTH, :] = jnp.where(
            (prow >= he * d) & (prow < (he + 1) * d), qt2, jnp.zeros_like(qt2))
        qaug_ref[e, PAIR_WIDTH:PAIR_WIDTH + nb, :] = selfar_ref[0, e]
        qaug_ref[e, PAIR_WIDTH + nb:, :] = jnp.zeros((PAIR_WIDTH - nb, MOBA_BLOCK), BF16)

    def qk(j, slot, near=False):
        rows = pl.ds(pl.multiple_of(j * blk_rows, blk_rows), blk_rows)
        for e in range(MOBA_HEADS):
            if near:
                s_ref[slot, e] = _dot(kaug_ref[e // 2, rows, 0:PAIR_WIDTH], qaug_ref[e, 0:PAIR_WIDTH, :])
            else:
                s_ref[slot, e] = _dot(kaug_ref[e // 2, rows, :], qaug_ref[e])

    def sm(j, slot, bias=None, first=False):
        for e in range(MOBA_HEADS):
            st = s_ref[slot, e]
            if bias is not None:
                st = st + bias(e)
            mj = jnp.max(st, axis=0, keepdims=True)
            if first:
                m_ref[e] = mj
                acc_ref[e] = _dot(vaug_ref[e, j], jnp.exp2(st - mj).astype(BF16))
            else:
                m_old = m_ref[e]
                m_new = jnp.maximum(m_old, mj)
                pv = _dot(vaug_ref[e, j], jnp.exp2(st - m_new).astype(BF16))
                acc_ref[e] = acc_ref[e] * jnp.exp2(m_old - m_new) + pv
                m_ref[e] = m_new

    clamp = lambda j: jnp.minimum(j, i)
    jp = jnp.maximum(i - 1, 0)
    qk(i, 0, near=True)
    qk(jp, 1, near=True)
    qk(clamp(0), 2)
    qk(clamp(1), 3)
    sm(i, 0, bias=lambda e: tnear_ref[e, blk_rows:, :], first=True)
    sm(jp, 1, bias=lambda e: tnear_ref[e, 0:blk_rows, :] + selprev_ref[0, e])

    def far(t, carry):
        j = 4 * t
        qk(clamp(j + 2), 0)
        qk(clamp(j + 3), 1)
        sm(clamp(j), 2)
        sm(clamp(j + 1), 3)
        qk(clamp(j + 4), 2)
        qk(clamp(j + 5), 3)
        sm(clamp(j + 2), 0)
        sm(clamp(j + 3), 1)
        return carry

    lax.fori_loop(0, lax.shift_right_logical(i + 2, 2), far, 0)

    outs = []
    for e in range(MOBA_HEADS):
        acc = acc_ref[e]
        outs.append((acc[:d] / acc[d:d + 1]).T)
    out_ref[0] = jnp.concatenate(outs, axis=1).astype(BF16)


def _moba(mqt, mk, mvt, selfar, selprev, tnear):
    b, s, _ = mk.shape
    nb = s // MOBA_BLOCK
    assert nb <= PAIR_WIDTH
    nh = MOBA_HEADS
    return pl.pallas_call(
        _moba_kernel,
        grid=(b, nb),
        in_specs=[
            pl.BlockSpec((1, 1, MOBA_WIDTH, MOBA_BLOCK), lambda bi, i: (bi, i, 0, 0)),
            pl.BlockSpec((1, MOBA_BLOCK, MOBA_WIDTH), lambda bi, i: (bi, i, 0)),
            pl.BlockSpec((1, 1, MOBA_WIDTH, MOBA_BLOCK), lambda bi, i: (bi, i, 0, 0)),
            pl.BlockSpec((1, nh, nb, MOBA_BLOCK), lambda bi, i: (bi, 0, 0, i)),
            pl.BlockSpec((1, nh, 1, MOBA_BLOCK), lambda bi, i: (bi, 0, 0, i)),
            pl.BlockSpec((nh, 2 * MOBA_BLOCK, MOBA_BLOCK), lambda bi, i: (0, 0, 0),
                         pipeline_mode=pl.Buffered(1)),
        ],
        out_specs=pl.BlockSpec((1, MOBA_BLOCK, MOBA_WIDTH), lambda bi, i: (bi, i, 0)),
        out_shape=jax.ShapeDtypeStruct((b, s, MOBA_WIDTH), BF16),
        scratch_shapes=[
            pltpu.VMEM((nh // 2, s, 2 * PAIR_WIDTH), BF16),
            pltpu.VMEM((nh, nb, MOBA_HEAD_DIM + ONES_ROWS, MOBA_BLOCK), BF16),
            pltpu.VMEM((nh, 2 * PAIR_WIDTH, MOBA_BLOCK), BF16),
            pltpu.VMEM((4, nh, MOBA_BLOCK, MOBA_BLOCK), F32),
            pltpu.VMEM((nh, 1, MOBA_BLOCK), F32),
            pltpu.VMEM((nh, MOBA_HEAD_DIM + ONES_ROWS, MOBA_BLOCK), F32),
        ],
        compiler_params=pltpu.CompilerParams(
            dimension_semantics=("arbitrary", "arbitrary"), vmem_limit_bytes=VMEM_LIMIT),
        name="moba",
    )(mqt, mk, mvt, selfar, selprev, tnear)


def _rms(x, g):
    return x * lax.rsqrt(jnp.mean(x * x, axis=-1, keepdims=True) + EPS) * g


def _out_ffn_kernel(x_ref, ret_ref, moba_ref, wo_ref, wg_ref, wu_ref, wd_ref,
                    g_post_mix_ref, g_pre_ffn_ref, g_post_ffn_ref, out_ref):
    mix_in = jnp.concatenate([ret_ref[...], moba_ref[...]], axis=1)
    x1 = x_ref[...] + _rms(_dot(mix_in, wo_ref[...]), g_post_mix_ref[...])
    h = _rms(x1, g_pre_ffn_ref[...]).astype(BF16)
    f = None
    for c in range(D_FF // FFN_CHUNK):
        cols = slice(c * FFN_CHUNK, (c + 1) * FFN_CHUNK)
        gate = _dot(h, wg_ref[:, cols])
        up = _dot(h, wu_ref[:, cols])
        act = (gate * jax.nn.sigmoid(gate) * up).astype(BF16)
        part = _dot(act, wd_ref[cols, :])
        f = part if f is None else f + part
    out_ref[...] = x1 + _rms(f, g_post_ffn_ref[...])


def _out_ffn(x2, ret2, moba2, wo, wg, wu, wd, g_post_mix, g_pre_ffn, g_post_ffn):
    n, d = x2.shape
    tm = min(FFN_TOKENS, n)
    tok = lambda w: pl.BlockSpec((tm, w), lambda t: (t, 0))
    resident = lambda a: pl.BlockSpec(a.shape, lambda t: (0, 0), pipeline_mode=pl.Buffered(1))
    return pl.pallas_call(
        _out_ffn_kernel,
        grid=(n // tm,),
        in_specs=[tok(d), tok(RET_WIDTH), tok(MOBA_WIDTH),
                  resident(wo), resident(wg), resident(wu), resident(wd),
                  resident(g_post_mix), resident(g_pre_ffn), resident(g_post_ffn)],
        out_specs=tok(d),
        out_shape=jax.ShapeDtypeStruct((n, d), F32),
        compiler_params=pltpu.CompilerParams(
            dimension_semantics=("arbitrary",), vmem_limit_bytes=VMEM_LIMIT),
        name="out_ffn",
    )(x2, ret2, moba2, wo, wg, wu, wd, g_post_mix, g_pre_ffn, g_post_ffn)


def _rotary_tables(s):
    half = RET_QK_DIM // 2
    inv_freq = ROPE_BASE ** (-jnp.arange(half, dtype=F32) / half)
    ang = jnp.arange(s, dtype=F32)[:, None] * inv_freq[None, :]
    cos, sin = jnp.cos(ang), jnp.sin(ang)
    cq = jnp.tile(jnp.concatenate([cos, cos], axis=1), (1, RET_HEADS))
    sq = jnp.tile(jnp.concatenate([-sin, sin], axis=1), (1, RET_HEADS))
    return cq, sq, cos.T, sin.T


def _retention_tables():
    c = RET_CHUNK
    log_gamma = jnp.log1p(-jnp.exp(jnp.linspace(math.log(1.0 / 32), math.log(1.0 / 512), RET_HEADS)))
    log_gamma = log_gamma.astype(F32)
    idx = jnp.arange(c, dtype=F32)
    diff = idx[:, None] - idx[None, :]
    dmat = jnp.where(diff >= 0, jnp.exp(jnp.maximum(diff, 0.0)[None] * log_gamma[:, None, None]), 0.0)
    q_decay = jnp.exp((idx + 1.0)[None, :] * log_gamma[:, None])
    k_decay = jnp.exp((c - 1.0 - idx)[None, :] * log_gamma[:, None])
    chunk_decay = jnp.exp(c * log_gamma)
    qd = jnp.broadcast_to(q_decay[:, :, None], (RET_HEADS, c, RET_V_DIM))
    kd = k_decay[:, None, :]
    cd = jnp.broadcast_to(chunk_decay[:, None, None], (RET_HEADS, 1, RET_V_DIM))
    return dmat, qd, kd, cd


def _t5_bucket(rel):
    n = jnp.maximum(rel, 0)
    max_exact = REL_BUCKETS // 2
    is_small = n < max_exact
    nf = jnp.maximum(n, 1).astype(F32)
    large = max_exact + (jnp.log(nf / max_exact) / math.log(REL_MAX_DIST / max_exact)
                         * (REL_BUCKETS - max_exact)).astype(jnp.int32)
    large = jnp.minimum(large, REL_BUCKETS - 1)
    return jnp.where(is_small, n, large)


def _bias_tables(rel_bias):
    bias_t = rel_bias.T.astype(F32) * LOG2E
    kk = jnp.arange(MOBA_BLOCK)[:, None]
    qq = jnp.arange(MOBA_BLOCK)[None, :]
    rel_own = qq - kk

    def lookup(rel):
        one_hot = (_t5_bucket(rel)[..., None] == jnp.arange(REL_BUCKETS)).astype(F32)
        return jnp.einsum("kqn,hn->hkq", one_hot, bias_t, precision=lax.Precision.HIGHEST)

    town = jnp.where(rel_own[None] >= 0, lookup(rel_own), NEG)
    tprev = lookup(rel_own + MOBA_BLOCK)
    cfar = bias_t[:, REL_BUCKETS - 1]
    return jnp.concatenate([tprev, town], axis=1), cfar


def kernel(x, w_in, w_out, pre_mix_norm, post_mix_norm, pre_ffn_norm, post_ffn_norm,
           rel_bias, w_gate, w_up, w_down):
    b, s, d = x.shape
    assert d == D_MODEL and s % IN_TOKENS == 0 and MOBA_BLOCK + 1 >= REL_MAX_DIST
    depth = w_in.shape[0]

    cq, sq, ckt, skt = _rotary_tables(s)
    dmat, qd, kd, cd = _retention_tables()
    tnear, cfar = _bias_tables(rel_bias)

    sizes = [RET_QK_WIDTH, RET_QK_WIDTH, RET_WIDTH, RET_WIDTH, MOBA_WIDTH, MOBA_WIDTH, MOBA_WIDTH]
    o_rq, o_rk, o_rv, o_rg, o_mq, o_mk, o_mv, _ = np.cumsum([0] + sizes).tolist()

    for layer in range(depth):
        w = w_in[layer]
        w_nat = jnp.concatenate(
            [w[:, o_rq:o_rk], w[:, o_rv:o_mq], w[:, o_mk:o_mv]], axis=1).astype(BF16)
        w_tr = jnp.concatenate([w[:, o_rk:o_rv], w[:, o_mq:o_mk], w[:, o_mv:]], axis=1).T.astype(BF16)

        rq, rv, sg, mk, kmean, rkt, mqt, mvt = _in_proj(
            x, pre_mix_norm[layer][None, :], w_nat, w_tr, cq, sq, ckt, skt)

        ret = _retention(rq, rkt, rv, sg, dmat, qd, kd, cd)

        nb = s // MOBA_BLOCK
        kmean = kmean.reshape(b, nb, MOBA_HEADS, MOBA_HEAD_DIM).transpose(0, 2, 1, 3)
        selfar, selprev = _moba_gate(cfar, mqt, kmean)
        moba = _moba(mqt, mk, mvt, selfar, selprev, tnear)

        x = _out_ffn(
            x.reshape(b * s, d), ret.reshape(b * s, RET_WIDTH), moba.reshape(b * s, MOBA_WIDTH),
            w_out[layer].astype(BF16), w_gate[layer].astype(BF16), w_up[layer].astype(BF16),
            w_down[layer].astype(BF16), post_mix_norm[layer][None, :], pre_ffn_norm[layer][None, :],
            post_ffn_norm[layer][None, :]).reshape(b, s, d)
    return x
```

```python
import functools
import math

import jax
import jax.numpy as jnp
import numpy as np
from jax import lax
from jax.experimental import pallas as pl
from jax.experimental.pallas import tpu as pltpu

F32 = jnp.float32
BF16 = jnp.bfloat16

D_MODEL = 1024
RET_HEADS = 4
RET_QK_DIM = 64
RET_V_DIM = 128
RET_CHUNK = 128
RET_QK_WIDTH = RET_HEADS * RET_QK_DIM
RET_WIDTH = RET_HEADS * RET_V_DIM
MOBA_HEADS = 8
MOBA_HEAD_DIM = 64
MOBA_WIDTH = MOBA_HEADS * MOBA_HEAD_DIM
MOBA_BLOCK = 256
MOBA_TOPK = 3
REL_BUCKETS = 32
REL_MAX_DIST = 128
D_FF = 2816
EPS = 1e-6
ROPE_BASE = 10000.0

NEG = -1e30
LOG2E = math.log2(math.e)

IN_TOKENS = 512
RET_TOKENS = 1024
FFN_TOKENS = 512
MXU_TILE = 256
FFN_CHUNKS = ((0, 6 * MXU_TILE), (6 * MXU_TILE, D_FF))
VMEM_LIMIT = 56 * 1024 * 1024

NAT_WIDTH = RET_QK_WIDTH + 2 * RET_WIDTH + MOBA_WIDTH
TR_WIDTH = RET_QK_WIDTH + 2 * MOBA_WIDTH


def _nt_dot(a, b):
    return lax.dot_general(a, b, (((1,), (1,)), ((), ())), preferred_element_type=F32)


def _dot(a, b):
    return jnp.dot(a, b, preferred_element_type=F32)


def _in_proj_kernel(x_ref, g_ref, wn_ref, wt_ref, cq_ref, sq_ref, ckt_ref, skt_ref,
                    rq_ref, rv_ref, sg_ref, mk_ref, kmean_ref, rkt_ref, mqt_ref, mvt_ref):
    tm = x_ref.shape[1]
    x = x_ref[0]
    ms = jnp.mean(x * x, axis=-1, keepdims=True)
    h = (x * lax.rsqrt(ms + EPS) * g_ref[...]).astype(BF16)

    pn = _dot(h, wn_ref[...])
    pt = _nt_dot(wt_ref[...], h)

    rq = pn[:, :RET_QK_WIDTH]
    lane = lax.broadcasted_iota(jnp.int32, rq.shape, 1)
    first_half = (lane % RET_QK_DIM) < (RET_QK_DIM // 2)
    partner = jnp.where(first_half,
                        pltpu.roll(rq, RET_QK_WIDTH - RET_QK_DIM // 2, 1),
                        pltpu.roll(rq, RET_QK_DIM // 2, 1))
    rq_ref[0] = (rq * cq_ref[...] + partner * sq_ref[...]).astype(BF16)

    rv_ref[0] = pn[:, RET_QK_WIDTH:RET_QK_WIDTH + RET_WIDTH].astype(BF16)
    rg = pn[:, RET_QK_WIDTH + RET_WIDTH:RET_QK_WIDTH + 2 * RET_WIDTH]
    sg_ref[0] = (rg * jax.nn.sigmoid(rg)).astype(BF16)

    mk = pn[:, RET_QK_WIDTH + 2 * RET_WIDTH:]
    mk_ref[0] = mk.astype(BF16)
    for blk in range(tm // MOBA_BLOCK):
        kmean_ref[0, 0, blk:blk + 1, :] = jnp.mean(
            mk[blk * MOBA_BLOCK:(blk + 1) * MOBA_BLOCK], axis=0, keepdims=True)

    half = RET_QK_DIM // 2
    cos_t = ckt_ref[...]
    sin_t = skt_ref[...]
    parts = []
    for hd in range(RET_HEADS):
        x1 = pt[hd * RET_QK_DIM:hd * RET_QK_DIM + half]
        x2 = pt[hd * RET_QK_DIM + half:(hd + 1) * RET_QK_DIM]
        parts.append(x1 * cos_t - x2 * sin_t)
        parts.append(x2 * cos_t + x1 * sin_t)
    rkt = (jnp.concatenate(parts, axis=0) * (RET_QK_DIM ** -0.5)).astype(BF16)
    for c in range(tm // RET_CHUNK):
        rkt_ref[0, c] = rkt[:, c * RET_CHUNK:(c + 1) * RET_CHUNK]

    mqt = (pt[RET_QK_WIDTH:RET_QK_WIDTH + MOBA_WIDTH] * (MOBA_HEAD_DIM ** -0.5 * LOG2E)).astype(BF16)
    mvt = pt[RET_QK_WIDTH + MOBA_WIDTH:].astype(BF16)
    for blk in range(tm // MOBA_BLOCK):
        mqt_ref[0, blk] = mqt[:, blk * MOBA_BLOCK:(blk + 1) * MOBA_BLOCK]
        mvt_ref[0, blk] = mvt[:, blk * MOBA_BLOCK:(blk + 1) * MOBA_BLOCK]


def _in_proj(x, gain, w_nat, w_tr, cq, sq, ckt, skt):
    b, s, d = x.shape
    tm = IN_TOKENS
    ns = s // tm
    bpt = tm // MOBA_BLOCK
    cpt = tm // RET_CHUNK
    const = lambda si, bi: (0, 0)
    out_shape = (
        jax.ShapeDtypeStruct((b, s, RET_QK_WIDTH), BF16),
        jax.ShapeDtypeStruct((b, s, RET_WIDTH), BF16),
        jax.ShapeDtypeStruct((b, s, RET_WIDTH), BF16),
        jax.ShapeDtypeStruct((b, s, MOBA_WIDTH), BF16),
        jax.ShapeDtypeStruct((b, ns, bpt, MOBA_WIDTH), F32),
        jax.ShapeDtypeStruct((b, s // RET_CHUNK, RET_QK_WIDTH, RET_CHUNK), BF16),
        jax.ShapeDtypeStruct((b, s // MOBA_BLOCK, MOBA_WIDTH, MOBA_BLOCK), BF16),
        jax.ShapeDtypeStruct((b, s // MOBA_BLOCK, MOBA_WIDTH, MOBA_BLOCK), BF16),
    )
    tok = lambda w: pl.BlockSpec((1, tm, w), lambda si, bi: (bi, si, 0))
    return pl.pallas_call(
        _in_proj_kernel,
        grid=(ns, b),
        in_specs=[
            tok(d),
            pl.BlockSpec((1, d), const),
            pl.BlockSpec((d, NAT_WIDTH), const),
            pl.BlockSpec((TR_WIDTH, d), const),
            pl.BlockSpec((tm, RET_QK_WIDTH), lambda si, bi: (si, 0)),
            pl.BlockSpec((tm, RET_QK_WIDTH), lambda si, bi: (si, 0)),
            pl.BlockSpec((RET_QK_DIM // 2, tm), lambda si, bi: (0, si)),
            pl.BlockSpec((RET_QK_DIM // 2, tm), lambda si, bi: (0, si)),
        ],
        out_specs=(
            tok(RET_QK_WIDTH), tok(RET_WIDTH), tok(RET_WIDTH), tok(MOBA_WIDTH),
            pl.BlockSpec((1, 1, bpt, MOBA_WIDTH), lambda si, bi: (bi, si, 0, 0)),
            pl.BlockSpec((1, cpt, RET_QK_WIDTH, RET_CHUNK), lambda si, bi: (bi, si, 0, 0)),
            pl.BlockSpec((1, bpt, MOBA_WIDTH, MOBA_BLOCK), lambda si, bi: (bi, si, 0, 0)),
            pl.BlockSpec((1, bpt, MOBA_WIDTH, MOBA_BLOCK), lambda si, bi: (bi, si, 0, 0)),
        ),
        out_shape=out_shape,
        compiler_params=pltpu.CompilerParams(
            dimension_semantics=("arbitrary", "arbitrary"), vmem_limit_bytes=VMEM_LIMIT),
        name="in_proj",
    )(x, gain, w_nat, w_tr, cq, sq, ckt, skt)


def _retention_kernel(rq_ref, rkt_ref, rv_ref, sg_ref, dmat_ref, qd_ref, kd_ref, cd_ref,
                      out_ref, state_ref):
    @pl.when(pl.program_id(1) == 0)
    def _():
        state_ref[...] = jnp.zeros_like(state_ref)

    n_chunks = rq_ref.shape[1] // RET_CHUNK

    def chunk(c, carry):
        r0 = pl.multiple_of(c * RET_CHUNK, RET_CHUNK)
        rows = pl.ds(r0, RET_CHUNK)
        for hd in range(RET_HEADS):
            q = rq_ref[0, rows, hd * RET_QK_DIM:(hd + 1) * RET_QK_DIM]
            kt = rkt_ref[0, c, hd * RET_QK_DIM:(hd + 1) * RET_QK_DIM, :]
            v = rv_ref[0, rows, hd * RET_V_DIM:(hd + 1) * RET_V_DIM]
            state = state_ref[hd]
            scores = _dot(q, kt) * dmat_ref[hd]
            o = _dot(scores.astype(BF16), v) + qd_ref[hd] * _dot(q, state.astype(BF16))
            kts = (kt.astype(F32) * kd_ref[hd]).astype(BF16)
            state_ref[hd] = state * cd_ref[hd] + _dot(kts, v)
            ms = jnp.mean(o * o, axis=-1, keepdims=True)
            gate = sg_ref[0, rows, hd * RET_V_DIM:(hd + 1) * RET_V_DIM].astype(F32)
            out_ref[0, rows, hd * RET_V_DIM:(hd + 1) * RET_V_DIM] = (
                o * lax.rsqrt(ms + EPS) * gate).astype(BF16)
        return carry

    lax.fori_loop(0, n_chunks, chunk, 0, unroll=8)


def _retention(rq, rkt, rv, sg, dmat, qd, kd, cd):
    b, s, _ = rq.shape
    tc = min(RET_TOKENS, s)
    tok = lambda w: pl.BlockSpec((1, tc, w), lambda bi, si: (bi, si, 0))
    tab = lambda a: pl.BlockSpec(a.shape, lambda bi, si: (0,) * a.ndim)
    return pl.pallas_call(
        _retention_kernel,
        grid=(b, s // tc),
        in_specs=[
            tok(RET_QK_WIDTH),
            pl.BlockSpec((1, tc // RET_CHUNK, RET_QK_WIDTH, RET_CHUNK), lambda bi, si: (bi, si, 0, 0)),
            tok(RET_WIDTH), tok(RET_WIDTH),
            tab(dmat), tab(qd), tab(kd), tab(cd),
        ],
        out_specs=tok(RET_WIDTH),
        out_shape=jax.ShapeDtypeStruct((b, s, RET_WIDTH), BF16),
        scratch_shapes=[pltpu.VMEM((RET_HEADS, RET_QK_DIM, RET_V_DIM), F32)],
        compiler_params=pltpu.CompilerParams(
            dimension_semantics=("arbitrary", "arbitrary"), vmem_limit_bytes=VMEM_LIMIT),
        name="retention",
    )(rq, rkt, rv, sg, dmat, qd, kd, cd)


ONES_ROWS = 16
PAIR_WIDTH = 2 * MOBA_HEAD_DIM
GATE_BLOCKS = 4


def _moba_gate_kernel(cfar_ref, qt_ref, km_ref, selfar_ref, selprev_ref):
    d = MOBA_HEAD_DIM
    nb = km_ref.shape[2]
    n_q = qt_ref.shape[1]
    shape = (nb, n_q * MOBA_BLOCK)
    blk = lax.broadcasted_iota(jnp.int32, shape, 0)
    i = pl.program_id(1) * n_q + lax.broadcasted_iota(jnp.int32, shape, 1) // MOBA_BLOCK
    blk_f = blk.astype(F32)
    for h in range(MOBA_HEADS):
        qt = jnp.concatenate([qt_ref[0, c, h * d:(h + 1) * d, :] for c in range(n_q)], axis=1)
        km = km_ref[0, h]
        km_hi = km.astype(BF16)
        km_lo = (km - km_hi.astype(F32)).astype(BF16)
        gate = _dot(km_hi, qt) + _dot(km_lo, qt)
        gate = jnp.where(blk < i, gate, NEG)
        chosen = jnp.zeros(gate.shape, F32)
        for _ in range(MOBA_TOPK):
            best = jnp.max(gate, axis=0, keepdims=True)
            idx = jnp.min(jnp.where(gate == best, blk_f, float(nb)), axis=0, keepdims=True)
            hit = blk_f == idx
            chosen = jnp.where(hit & (best > 0.5 * NEG), 1.0, chosen)
            gate = jnp.where(hit, NEG, gate)
        picked = chosen > 0.5
        selfar_ref[0, h] = jnp.where(picked & (blk < i - 1), cfar_ref[h], NEG).astype(BF16)
        prev_hit = jnp.max(jnp.where(picked & (blk == i - 1), 1.0, 0.0), axis=0, keepdims=True)
        selprev_ref[0, h] = jnp.where(prev_hit > 0.5, 0.0, NEG)


def _moba_gate(cfar, mqt, kmean):
    b, nb, _, _ = mqt.shape
    s = nb * MOBA_BLOCK
    n_q = math.gcd(GATE_BLOCKS, nb)
    width = n_q * MOBA_BLOCK
    return pl.pallas_call(
        _moba_gate_kernel,
        grid=(b, nb // n_q),
        in_specs=[
            pl.BlockSpec(memory_space=pltpu.SMEM),
            pl.BlockSpec((1, n_q, MOBA_WIDTH, MOBA_BLOCK), lambda bi, i: (bi, i, 0, 0)),
            pl.BlockSpec((1, MOBA_HEADS, nb, MOBA_HEAD_DIM), lambda bi, i: (bi, 0, 0, 0)),
        ],
        out_specs=(
            pl.BlockSpec((1, MOBA_HEADS, nb, width), lambda bi, i: (bi, 0, 0, i)),
            pl.BlockSpec((1, MOBA_HEADS, 1, width), lambda bi, i: (bi, 0, 0, i)),
        ),
        out_shape=(
            jax.ShapeDtypeStruct((b, MOBA_HEADS, nb, s), BF16),
            jax.ShapeDtypeStruct((b, MOBA_HEADS, 1, s), F32),
        ),
        compiler_params=pltpu.CompilerParams(
            dimension_semantics=("arbitrary", "arbitrary"), vmem_limit_bytes=VMEM_LIMIT),
        name="moba_gate",
    )(cfar, mqt, kmean)


def _moba_kernel(qt_ref, k_ref, vt_ref, selfar_ref, selprev_ref, tnear_ref,
                 out_ref, kaug_ref, vaug_ref, qaug_ref, s_ref, m_ref, acc_ref):
    i = pl.program_id(1)
    d = MOBA_HEAD_DIM
    nb = vaug_ref.shape[1]
    s_len = kaug_ref.shape[1]
    blk_rows = MOBA_BLOCK
    n_pairs = MOBA_HEADS // 2

    @pl.when((pl.program_id(0) == 0) & (i == 0))
    def _():
        row_blk = lax.broadcasted_iota(jnp.int32, (s_len, PAIR_WIDTH), 0) // MOBA_BLOCK
        lane = lax.broadcasted_iota(jnp.int32, (s_len, PAIR_WIDTH), 1)
        block_one_hot = jnp.where(row_blk == lane, 1.0, 0.0).astype(BF16)
        for pe in range(n_pairs):
            kaug_ref[pe, :, PAIR_WIDTH:] = block_one_hot
        for e in range(MOBA_HEADS):
            vaug_ref[e, :, d:, :] = jnp.ones((nb, ONES_ROWS, MOBA_BLOCK), BF16)

    own_rows = pl.ds(pl.multiple_of(i * blk_rows, blk_rows), blk_rows)
    for pe in range(n_pairs):
        kaug_ref[pe, own_rows, 0:PAIR_WIDTH] = k_ref[0, :, pe * PAIR_WIDTH:(pe + 1) * PAIR_WIDTH]
    for e in range(MOBA_HEADS):
        vaug_ref[e, i, 0:d, :] = vt_ref[0, 0, e * d:(e + 1) * d, :]

    prow = lax.broadcasted_iota(jnp.int32, (PAIR_WIDTH, MOBA_BLOCK), 0)
    for e in range(MOBA_HEADS):
        pe, he = divmod(e, 2)
        qt2 = qt_ref[0, 0, pe * PAIR_WIDTH:(pe + 1) * PAIR_WIDTH, :]
        qaug_ref[e, 0:PAIR_WIDTH, :] = jnp.where(
            (prow >= he * d) & (prow < (he + 1) * d), qt2, jnp.zeros_like(qt2))
        qaug_ref[e, PAIR_WIDTH:PAIR_WIDTH + nb, :] = selfar_ref[0, e]
        qaug_ref[e, PAIR_WIDTH + nb:, :] = jnp.zeros((PAIR_WIDTH - nb, MOBA_BLOCK), BF16)

    def qk(j, slot, near=False):
        rows = pl.ds(pl.multiple_of(j * blk_rows, blk_rows), blk_rows)
        for e in range(MOBA_HEADS):
            if near:
                s_ref[slot, e] = _dot(kaug_ref[e // 2, rows, 0:PAIR_WIDTH], qaug_ref[e, 0:PAIR_WIDTH, :])
            else:
                s_ref[slot, e] = _dot(kaug_ref[e // 2, rows, :], qaug_ref[e])

    def sm(j, slot, bias=None, first=False):
        for e in range(MOBA_HEADS):
            st = s_ref[slot, e]
            if bias is not None:
                st = st + bias(e)
            mj = jnp.max(st, axis=0, keepdims=True)
            if first:
                m_ref[e] = mj
                acc_ref[e] = _dot(vaug_ref[e, j], jnp.exp2(st - mj).astype(BF16))
            else:
                m_old = m_ref[e]
                m_new = jnp.maximum(m_old, mj)
                pv = _dot(vaug_ref[e, j], jnp.exp2(st - m_new).astype(BF16))
                acc_ref[e] = acc_ref[e] * jnp.exp2(m_old - m_new) + pv
                m_ref[e] = m_new

    clamp = lambda j: jnp.minimum(j, i)
    jp = jnp.maximum(i - 1, 0)
    qk(i, 0, near=True)
    qk(jp, 1, near=True)
    qk(clamp(0), 2)
    qk(clamp(1), 3)
    sm(i, 0, bias=lambda e: tnear_ref[e, blk_rows:, :], first=True)
    sm(jp, 1, bias=lambda e: tnear_ref[e, 0:blk_rows, :] + selprev_ref[0, e])

    def far(t, carry):
        j = 4 * t
        qk(clamp(j + 2), 0)
        qk(clamp(j + 3), 1)
        sm(clamp(j), 2)
        sm(clamp(j + 1), 3)
        qk(clamp(j + 4), 2)
        qk(clamp(j + 5), 3)
        sm(clamp(j + 2), 0)
        sm(clamp(j + 3), 1)
        return carry

    lax.fori_loop(0, lax.shift_right_logical(i + 2, 2), far, 0)

    outs = []
    for e in range(MOBA_HEADS):
        acc = acc_ref[e]
        outs.append((acc[:d] / acc[d:d + 1]).T)
    out_ref[0] = jnp.concatenate(outs, axis=1).astype(BF16)


def _moba(mqt, mk, mvt, selfar, selprev, tnear):
    b, s, _ = mk.shape
    nb = s // MOBA_BLOCK
    assert nb <= PAIR_WIDTH
    nh = MOBA_HEADS
    return pl.pallas_call(
        _moba_kernel,
        grid=(b, nb),
        in_specs=[
            pl.BlockSpec((1, 1, MOBA_WIDTH, MOBA_BLOCK), lambda bi, i: (bi, i, 0, 0)),
            pl.BlockSpec((1, MOBA_BLOCK, MOBA_WIDTH), lambda bi, i: (bi, i, 0)),
            pl.BlockSpec((1, 1, MOBA_WIDTH, MOBA_BLOCK), lambda bi, i: (bi, i, 0, 0)),
            pl.BlockSpec((1, nh, nb, MOBA_BLOCK), lambda bi, i: (bi, 0, 0, i)),
            pl.BlockSpec((1, nh, 1, MOBA_BLOCK), lambda bi, i: (bi, 0, 0, i)),
            pl.BlockSpec((nh, 2 * MOBA_BLOCK, MOBA_BLOCK), lambda bi, i: (0, 0, 0),
                         pipeline_mode=pl.Buffered(1)),
        ],
        out_specs=pl.BlockSpec((1, MOBA_BLOCK, MOBA_WIDTH), lambda bi, i: (bi, i, 0)),
        out_shape=jax.ShapeDtypeStruct((b, s, MOBA_WIDTH), BF16),
        scratch_shapes=[
            pltpu.VMEM((nh // 2, s, 2 * PAIR_WIDTH), BF16),
            pltpu.VMEM((nh, nb, MOBA_HEAD_DIM + ONES_ROWS, MOBA_BLOCK), BF16),
            pltpu.VMEM((nh, 2 * PAIR_WIDTH, MOBA_BLOCK), BF16),
            pltpu.VMEM((4, nh, MOBA_BLOCK, MOBA_BLOCK), F32),
            pltpu.VMEM((nh, 1, MOBA_BLOCK), F32),
            pltpu.VMEM((nh, MOBA_HEAD_DIM + ONES_ROWS, MOBA_BLOCK), F32),
        ],
        compiler_params=pltpu.CompilerParams(
            dimension_semantics=("arbitrary", "arbitrary"), vmem_limit_bytes=VMEM_LIMIT),
        name="moba",
    )(mqt, mk, mvt, selfar, selprev, tnear)


def _rms(x, g):
    return x * lax.rsqrt(jnp.mean(x * x, axis=-1, keepdims=True) + EPS) * g


def _out_ffn_kernel(x_ref, ret_ref, moba_ref, wo_ref, wg_ref, wu_ref, wd_ref,
                    g_post_mix_ref, g_pre_ffn_ref, g_post_ffn_ref, out_ref):
    mix_in = jnp.concatenate([ret_ref[...], moba_ref[...]], axis=1)
    x1 = x_ref[...] + _rms(_dot(mix_in, wo_ref[...]), g_post_mix_ref[...])
    h = _rms(x1, g_pre_ffn_ref[...]).astype(BF16)
    f = None
    for lo, hi in FFN_CHUNKS:
        cols = slice(lo, hi)
        gate = _dot(h, wg_ref[:, cols])
        up = _dot(h, wu_ref[:, cols])
        act = (gate * jax.nn.sigmoid(gate) * up).astype(BF16)
        part = _dot(act, wd_ref[cols, :])
        f = part if f is None else f + part
    out_ref[...] = x1 + _rms(f, g_post_ffn_ref[...])


def _out_ffn(x2, ret2, moba2, wo, wg, wu, wd, g_post_mix, g_pre_ffn, g_post_ffn):
    n, d = x2.shape
    tm = min(FFN_TOKENS, n)
    tok = lambda w: pl.BlockSpec((tm, w), lambda t: (t, 0))
    resident = lambda a: pl.BlockSpec(a.shape, lambda t: (0, 0), pipeline_mode=pl.Buffered(1))
    return pl.pallas_call(
        _out_ffn_kernel,
        grid=(n // tm,),
        in_specs=[tok(d), tok(RET_WIDTH), tok(MOBA_WIDTH),
                  resident(wo), resident(wg), resident(wu), resident(wd),
                  resident(g_post_mix), resident(g_pre_ffn), resident(g_post_ffn)],
        out_specs=tok(d),
        out_shape=jax.ShapeDtypeStruct((n, d), F32),
        compiler_params=pltpu.CompilerParams(
            dimension_semantics=("arbitrary",), vmem_limit_bytes=VMEM_LIMIT),
        name="out_ffn",
    )(x2, ret2, moba2, wo, wg, wu, wd, g_post_mix, g_pre_ffn, g_post_ffn)


def _rotary_tables(s):
    half = RET_QK_DIM // 2
    inv_freq = ROPE_BASE ** (-jnp.arange(half, dtype=F32) / half)
    ang = jnp.arange(s, dtype=F32)[:, None] * inv_freq[None, :]
    cos, sin = jnp.cos(ang), jnp.sin(ang)
    cq = jnp.tile(jnp.concatenate([cos, cos], axis=1), (1, RET_HEADS))
    sq = jnp.tile(jnp.concatenate([-sin, sin], axis=1), (1, RET_HEADS))
    return cq, sq, cos.T, sin.T


def _retention_tables():
    c = RET_CHUNK
    log_gamma = jnp.log1p(-jnp.exp(jnp.linspace(math.log(1.0 / 32), math.log(1.0 / 512), RET_HEADS)))
    log_gamma = log_gamma.astype(F32)
    idx = jnp.arange(c, dtype=F32)
    diff = idx[:, None] - idx[None, :]
    dmat = jnp.where(diff >= 0, jnp.exp(jnp.maximum(diff, 0.0)[None] * log_gamma[:, None, None]), 0.0)
    q_decay = jnp.exp((idx + 1.0)[None, :] * log_gamma[:, None])
    k_decay = jnp.exp((c - 1.0 - idx)[None, :] * log_gamma[:, None])
    chunk_decay = jnp.exp(c * log_gamma)
    qd = jnp.broadcast_to(q_decay[:, :, None], (RET_HEADS, c, RET_V_DIM))
    kd = k_decay[:, None, :]
    cd = jnp.broadcast_to(chunk_decay[:, None, None], (RET_HEADS, 1, RET_V_DIM))
    return dmat, qd, kd, cd


def _t5_bucket(rel):
    n = jnp.maximum(rel, 0)
    max_exact = REL_BUCKETS // 2
    is_small = n < max_exact
    nf = jnp.maximum(n, 1).astype(F32)
    large = max_exact + (jnp.log(nf / max_exact) / math.log(REL_MAX_DIST / max_exact)
                         * (REL_BUCKETS - max_exact)).astype(jnp.int32)
    large = jnp.minimum(large, REL_BUCKETS - 1)
    return jnp.where(is_small, n, large)


def _bias_tables(rel_bias):
    bias_t = rel_bias.T.astype(F32) * LOG2E
    kk = jnp.arange(MOBA_BLOCK)[:, None]
    qq = jnp.arange(MOBA_BLOCK)[None, :]
    rel_own = qq - kk

    def lookup(rel):
        one_hot = (_t5_bucket(rel)[..., None] == jnp.arange(REL_BUCKETS)).astype(F32)
        return jnp.einsum("kqn,hn->hkq", one_hot, bias_t, precision=lax.Precision.HIGHEST)

    town = jnp.where(rel_own[None] >= 0, lookup(rel_own), NEG)
    tprev = lookup(rel_own + MOBA_BLOCK)
    cfar = bias_t[:, REL_BUCKETS - 1]
    return jnp.concatenate([tprev, town], axis=1), cfar


def kernel(x, w_in, w_out, pre_mix_norm, post_mix_norm, pre_ffn_norm, post_ffn_norm,
           rel_bias, w_gate, w_up, w_down):
    b, s, d = x.shape
    assert d == D_MODEL and s % IN_TOKENS == 0 and MOBA_BLOCK + 1 >= REL_MAX_DIST
    depth = w_in.shape[0]

    cq, sq, ckt, skt = _rotary_tables(s)
    dmat, qd, kd, cd = _retention_tables()
    tnear, cfar = _bias_tables(rel_bias)

    sizes = [RET_QK_WIDTH, RET_QK_WIDTH, RET_WIDTH, RET_WIDTH, MOBA_WIDTH, MOBA_WIDTH, MOBA_WIDTH]
    o_rq, o_rk, o_rv, o_rg, o_mq, o_mk, o_mv, _ = np.cumsum([0] + sizes).tolist()

    for layer in range(depth):
        w = w_in[layer]
        w_nat = jnp.concatenate(
            [w[:, o_rq:o_rk], w[:, o_rv:o_mq], w[:, o_mk:o_mv]], axis=1).astype(BF16)
        w_tr = jnp.concatenate([w[:, o_rk:o_rv], w[:, o_mq:o_mk], w[:, o_mv:]], axis=1).T.astype(BF16)

        rq, rv, sg, mk, kmean, rkt, mqt, mvt = _in_proj(
            x, pre_mix_norm[layer][None, :], w_nat, w_tr, cq, sq, ckt, skt)

        ret = _retention(rq, rkt, rv, sg, dmat, qd, kd, cd)

        nb = s // MOBA_BLOCK
        kmean = kmean.reshape(b, nb, MOBA_HEADS, MOBA_HEAD_DIM).transpose(0, 2, 1, 3)
        selfar, selprev = _moba_gate(cfar, mqt, kmean)
        moba = _moba(mqt, mk, mvt, selfar, selprev, tnear)

        x = _out_ffn(
            x.reshape(b * s, d), ret.reshape(b * s, RET_WIDTH), moba.reshape(b * s, MOBA_WIDTH),
            w_out[layer].astype(BF16), w_gate[layer].astype(BF16), w_up[layer].astype(BF16),
            w_down[layer].astype(BF16), post_mix_norm[layer][None, :], pre_ffn_norm[layer][None, :],
            post_ffn_norm[layer][None, :]).reshape(b, s, d)
    return x
```

```python
import functools
import math

import jax
import jax.numpy as jnp
import numpy as np
from jax import lax
from jax.experimental import pallas as pl
from jax.experimental.pallas import tpu as pltpu

F32 = jnp.float32
BF16 = jnp.bfloat16

D_MODEL = 1024
RET_HEADS = 4
RET_QK_DIM = 64
RET_V_DIM = 128
RET_CHUNK = 128
RET_QK_WIDTH = RET_HEADS * RET_QK_DIM
RET_WIDTH = RET_HEADS * RET_V_DIM
MOBA_HEADS = 8
MOBA_HEAD_DIM = 64
MOBA_WIDTH = MOBA_HEADS * MOBA_HEAD_DIM
MOBA_BLOCK = 256
MOBA_TOPK = 3
REL_BUCKETS = 32
REL_MAX_DIST = 128
D_FF = 2816
EPS = 1e-6
ROPE_BASE = 10000.0

NEG = -1e30
LOG2E = math.log2(math.e)

IN_TOKENS = 1024
RET_TOKENS = 1024
FFN_TOKENS = 512
MXU_TILE = 256
FFN_CHUNKS = ((0, 6 * MXU_TILE), (6 * MXU_TILE, D_FF))
VMEM_LIMIT = 56 * 1024 * 1024

NAT_WIDTH = RET_QK_WIDTH + 2 * RET_WIDTH + MOBA_WIDTH
TR_WIDTH = RET_QK_WIDTH + 2 * MOBA_WIDTH


def _nt_dot(a, b):
    return lax.dot_general(a, b, (((1,), (1,)), ((), ())), preferred_element_type=F32)


def _dot(a, b):
    return jnp.dot(a, b, preferred_element_type=F32)


def _in_proj_kernel(x_ref, g_ref, wn_ref, wt_ref, cq_ref, sq_ref, ckt_ref, skt_ref,
                    rq_ref, rv_ref, sg_ref, mk_ref, kmean_ref, rkt_ref, mqt_ref, mvt_ref):
    tm = x_ref.shape[1]
    x = x_ref[0]
    ms = jnp.mean(x * x, axis=-1, keepdims=True)
    h = (x * lax.rsqrt(ms + EPS) * g_ref[...]).astype(BF16)

    pn = _dot(h, wn_ref[...])
    pt = _nt_dot(wt_ref[...], h)

    rq = pn[:, :RET_QK_WIDTH]
    lane = lax.broadcasted_iota(jnp.int32, rq.shape, 1)
    first_half = (lane % RET_QK_DIM) < (RET_QK_DIM // 2)
    partner = jnp.where(first_half,
                        pltpu.roll(rq, RET_QK_WIDTH - RET_QK_DIM // 2, 1),
                        pltpu.roll(rq, RET_QK_DIM // 2, 1))
    rq_ref[0] = (rq * cq_ref[...] + partner * sq_ref[...]).astype(BF16)

    rv_ref[0] = pn[:, RET_QK_WIDTH:RET_QK_WIDTH + RET_WIDTH].astype(BF16)
    rg = pn[:, RET_QK_WIDTH + RET_WIDTH:RET_QK_WIDTH + 2 * RET_WIDTH]
    sg_ref[0] = (rg * jax.nn.sigmoid(rg)).astype(BF16)

    mk = pn[:, RET_QK_WIDTH + 2 * RET_WIDTH:]
    mk_ref[0] = mk.astype(BF16)
    for blk in range(tm // MOBA_BLOCK):
        kmean_ref[0, 0, blk:blk + 1, :] = jnp.mean(
            mk[blk * MOBA_BLOCK:(blk + 1) * MOBA_BLOCK], axis=0, keepdims=True)

    half = RET_QK_DIM // 2
    cos_t = ckt_ref[...]
    sin_t = skt_ref[...]
    parts = []
    for hd in range(RET_HEADS):
        x1 = pt[hd * RET_QK_DIM:hd * RET_QK_DIM + half]
        x2 = pt[hd * RET_QK_DIM + half:(hd + 1) * RET_QK_DIM]
        parts.append(x1 * cos_t - x2 * sin_t)
        parts.append(x2 * cos_t + x1 * sin_t)
    rkt = (jnp.concatenate(parts, axis=0) * (RET_QK_DIM ** -0.5)).astype(BF16)
    for c in range(tm // RET_CHUNK):
        rkt_ref[0, c] = rkt[:, c * RET_CHUNK:(c + 1) * RET_CHUNK]

    mqt = (pt[RET_QK_WIDTH:RET_QK_WIDTH + MOBA_WIDTH] * (MOBA_HEAD_DIM ** -0.5 * LOG2E)).astype(BF16)
    mvt = pt[RET_QK_WIDTH + MOBA_WIDTH:].astype(BF16)
    for blk in range(tm // MOBA_BLOCK):
        mqt_ref[0, blk] = mqt[:, blk * MOBA_BLOCK:(blk + 1) * MOBA_BLOCK]
        mvt_ref[0, blk] = mvt[:, blk * MOBA_BLOCK:(blk + 1) * MOBA_BLOCK]


def _in_proj(x, gain, w_nat, w_tr, cq, sq, ckt, skt):
    b, s, d = x.shape
    tm = IN_TOKENS
    ns = s // tm
    bpt = tm // MOBA_BLOCK
    cpt = tm // RET_CHUNK
    const = lambda si, bi: (0, 0)
    out_shape = (
        jax.ShapeDtypeStruct((b, s, RET_QK_WIDTH), BF16),
        jax.ShapeDtypeStruct((b, s, RET_WIDTH), BF16),
        jax.ShapeDtypeStruct((b, s, RET_WIDTH), BF16),
        jax.ShapeDtypeStruct((b, s, MOBA_WIDTH), BF16),
        jax.ShapeDtypeStruct((b, ns, bpt, MOBA_WIDTH), F32),
        jax.ShapeDtypeStruct((b, s // RET_CHUNK, RET_QK_WIDTH, RET_CHUNK), BF16),
        jax.ShapeDtypeStruct((b, s // MOBA_BLOCK, MOBA_WIDTH, MOBA_BLOCK), BF16),
        jax.ShapeDtypeStruct((b, s // MOBA_BLOCK, MOBA_WIDTH, MOBA_BLOCK), BF16),
    )
    tok = lambda w: pl.BlockSpec((1, tm, w), lambda si, bi: (bi, si, 0))
    return pl.pallas_call(
        _in_proj_kernel,
        grid=(ns, b),
        in_specs=[
            tok(d),
            pl.BlockSpec((1, d), const),
            pl.BlockSpec((d, NAT_WIDTH), const),
            pl.BlockSpec((TR_WIDTH, d), const),
            pl.BlockSpec((tm, RET_QK_WIDTH), lambda si, bi: (si, 0)),
            pl.BlockSpec((tm, RET_QK_WIDTH), lambda si, bi: (si, 0)),
            pl.BlockSpec((RET_QK_DIM // 2, tm), lambda si, bi: (0, si)),
            pl.BlockSpec((RET_QK_DIM // 2, tm), lambda si, bi: (0, si)),
        ],
        out_specs=(
            tok(RET_QK_WIDTH), tok(RET_WIDTH), tok(RET_WIDTH), tok(MOBA_WIDTH),
            pl.BlockSpec((1, 1, bpt, MOBA_WIDTH), lambda si, bi: (bi, si, 0, 0)),
            pl.BlockSpec((1, cpt, RET_QK_WIDTH, RET_CHUNK), lambda si, bi: (bi, si, 0, 0)),
            pl.BlockSpec((1, bpt, MOBA_WIDTH, MOBA_BLOCK), lambda si, bi: (bi, si, 0, 0)),
            pl.BlockSpec((1, bpt, MOBA_WIDTH, MOBA_BLOCK), lambda si, bi: (bi, si, 0, 0)),
        ),
        out_shape=out_shape,
        compiler_params=pltpu.CompilerParams(
            dimension_semantics=("arbitrary", "arbitrary"), vmem_limit_bytes=VMEM_LIMIT),
        name="in_proj",
    )(x, gain, w_nat, w_tr, cq, sq, ckt, skt)


def _retention_kernel(rq_ref, rkt_ref, rv_ref, sg_ref, dmat_ref, qd_ref, kd_ref, cd_ref,
                      out_ref, state_ref):
    @pl.when(pl.program_id(1) == 0)
    def _():
        state_ref[...] = jnp.zeros_like(state_ref)

    n_chunks = rq_ref.shape[1] // RET_CHUNK

    def chunk(c, carry):
        r0 = pl.multiple_of(c * RET_CHUNK, RET_CHUNK)
        rows = pl.ds(r0, RET_CHUNK)
        for hd in range(RET_HEADS):
            q = rq_ref[0, rows, hd * RET_QK_DIM:(hd + 1) * RET_QK_DIM]
            kt = rkt_ref[0, c, hd * RET_QK_DIM:(hd + 1) * RET_QK_DIM, :]
            v = rv_ref[0, rows, hd * RET_V_DIM:(hd + 1) * RET_V_DIM]
            state = state_ref[hd]
            scores = _dot(q, kt) * dmat_ref[hd]
            o = _dot(scores.astype(BF16), v) + qd_ref[hd] * _dot(q, state.astype(BF16))
            kts = (kt.astype(F32) * kd_ref[hd]).astype(BF16)
            state_ref[hd] = state * cd_ref[hd] + _dot(kts, v)
            ms = jnp.mean(o * o, axis=-1, keepdims=True)
            gate = sg_ref[0, rows, hd * RET_V_DIM:(hd + 1) * RET_V_DIM].astype(F32)
            out_ref[0, rows, hd * RET_V_DIM:(hd + 1) * RET_V_DIM] = (
                o * lax.rsqrt(ms + EPS) * gate).astype(BF16)
        return carry

    lax.fori_loop(0, n_chunks, chunk, 0, unroll=8)


def _retention(rq, rkt, rv, sg, dmat, qd, kd, cd):
    b, s, _ = rq.shape
    tc = min(RET_TOKENS, s)
    tok = lambda w: pl.BlockSpec((1, tc, w), lambda bi, si: (bi, si, 0))
    tab = lambda a: pl.BlockSpec(a.shape, lambda bi, si: (0,) * a.ndim)
    return pl.pallas_call(
        _retention_kernel,
        grid=(b, s // tc),
        in_specs=[
            tok(RET_QK_WIDTH),
            pl.BlockSpec((1, tc // RET_CHUNK, RET_QK_WIDTH, RET_CHUNK), lambda bi, si: (bi, si, 0, 0)),
            tok(RET_WIDTH), tok(RET_WIDTH),
            tab(dmat), tab(qd), tab(kd), tab(cd),
        ],
        out_specs=tok(RET_WIDTH),
        out_shape=jax.ShapeDtypeStruct((b, s, RET_WIDTH), BF16),
        scratch_shapes=[pltpu.VMEM((RET_HEADS, RET_QK_DIM, RET_V_DIM), F32)],
        compiler_params=pltpu.CompilerParams(
            dimension_semantics=("arbitrary", "arbitrary"), vmem_limit_bytes=VMEM_LIMIT),
        name="retention",
    )(rq, rkt, rv, sg, dmat, qd, kd, cd)


ONES_ROWS = 16
PAIR_WIDTH = 2 * MOBA_HEAD_DIM
GATE_BLOCKS = 4


def _moba_gate_kernel(cfar_ref, qt_ref, km_ref, selfar_ref, selprev_ref):
    d = MOBA_HEAD_DIM
    nb = km_ref.shape[2]
    n_q = qt_ref.shape[1]
    shape = (nb, n_q * MOBA_BLOCK)
    blk = lax.broadcasted_iota(jnp.int32, shape, 0)
    i = pl.program_id(1) * n_q + lax.broadcasted_iota(jnp.int32, shape, 1) // MOBA_BLOCK
    blk_f = blk.astype(F32)
    for h in range(MOBA_HEADS):
        qt = jnp.concatenate([qt_ref[0, c, h * d:(h + 1) * d, :] for c in range(n_q)], axis=1)
        km = km_ref[0, h]
        km_hi = km.astype(BF16)
        km_lo = (km - km_hi.astype(F32)).astype(BF16)
        gate = _dot(km_hi, qt) + _dot(km_lo, qt)
        gate = jnp.where(blk < i, gate, NEG)
        chosen = jnp.zeros(gate.shape, F32)
        for _ in range(MOBA_TOPK):
            best = jnp.max(gate, axis=0, keepdims=True)
            idx = jnp.min(jnp.where(gate == best, blk_f, float(nb)), axis=0, keepdims=True)
            hit = blk_f == idx
            chosen = jnp.where(hit & (best > 0.5 * NEG), 1.0, chosen)
            gate = jnp.where(hit, NEG, gate)
        picked = chosen > 0.5
        selfar_ref[0, h] = jnp.where(picked & (blk < i - 1), cfar_ref[h], NEG).astype(BF16)
        prev_hit = jnp.max(jnp.where(picked & (blk == i - 1), 1.0, 0.0), axis=0, keepdims=True)
        selprev_ref[0, h] = jnp.where(prev_hit > 0.5, 0.0, NEG)


def _moba_gate(cfar, mqt, kmean):
    b, nb, _, _ = mqt.shape
    s = nb * MOBA_BLOCK
    n_q = math.gcd(GATE_BLOCKS, nb)
    width = n_q * MOBA_BLOCK
    return pl.pallas_call(
        _moba_gate_kernel,
        grid=(b, nb // n_q),
        in_specs=[
            pl.BlockSpec(memory_space=pltpu.SMEM),
            pl.BlockSpec((1, n_q, MOBA_WIDTH, MOBA_BLOCK), lambda bi, i: (bi, i, 0, 0)),
            pl.BlockSpec((1, MOBA_HEADS, nb, MOBA_HEAD_DIM), lambda bi, i: (bi, 0, 0, 0)),
        ],
        out_specs=(
            pl.BlockSpec((1, MOBA_HEADS, nb, width), lambda bi, i: (bi, 0, 0, i)),
            pl.BlockSpec((1, MOBA_HEADS, 1, width), lambda bi, i: (bi, 0, 0, i)),
        ),
        out_shape=(
            jax.ShapeDtypeStruct((b, MOBA_HEADS, nb, s), BF16),
            jax.ShapeDtypeStruct((b, MOBA_HEADS, 1, s), F32),
        ),
        compiler_params=pltpu.CompilerParams(
            dimension_semantics=("arbitrary", "arbitrary"), vmem_limit_bytes=VMEM_LIMIT),
        name="moba_gate",
    )(cfar, mqt, kmean)


def _moba_kernel(qt_ref, k_ref, vt_ref, selfar_ref, selprev_ref, tnear_ref,
                 out_ref, kaug_ref, vaug_ref, qaug_ref, s_ref, m_ref, acc_ref):
    i = pl.program_id(1)
    d = MOBA_HEAD_DIM
    nb = vaug_ref.shape[1]
    s_len = kaug_ref.shape[1]
    blk_rows = MOBA_BLOCK
    n_pairs = MOBA_HEADS // 2

    @pl.when((pl.program_id(0) == 0) & (i == 0))
    def _():
        row_blk = lax.broadcasted_iota(jnp.int32, (s_len, PAIR_WIDTH), 0) // MOBA_BLOCK
        lane = lax.broadcasted_iota(jnp.int32, (s_len, PAIR_WIDTH), 1)
        block_one_hot = jnp.where(row_blk == lane, 1.0, 0.0).astype(BF16)
        for pe in range(n_pairs):
            kaug_ref[pe, :, PAIR_WIDTH:] = block_one_hot
        for e in range(MOBA_HEADS):
            vaug_ref[e, :, d:, :] = jnp.ones((nb, ONES_ROWS, MOBA_BLOCK), BF16)

    own_rows = pl.ds(pl.multiple_of(i * blk_rows, blk_rows), blk_rows)
    for pe in range(n_pairs):
        kaug_ref[pe, own_rows, 0:PAIR_WIDTH] = k_ref[0, :, pe * PAIR_WIDTH:(pe + 1) * PAIR_WIDTH]
    for e in range(MOBA_HEADS):
        vaug_ref[e, i, 0:d, :] = vt_ref[0, 0, e * d:(e + 1) * d, :]

    prow = lax.broadcasted_iota(jnp.int32, (PAIR_WIDTH, MOBA_BLOCK), 0)
    for e in range(MOBA_HEADS):
        pe, he = divmod(e, 2)
        qt2 = qt_ref[0, 0, pe * PAIR_WIDTH:(pe + 1) * PAIR_WIDTH, :]
        qaug_ref[e, 0:PAIR_WIDTH, :] = jnp.where(
            (prow >= he * d) & (prow < (he + 1) * d), qt2, jnp.zeros_like(qt2))
        qaug_ref[e, PAIR_WIDTH:PAIR_WIDTH + nb, :] = selfar_ref[0, e]
        qaug_ref[e, PAIR_WIDTH + nb:, :] = jnp.zeros((PAIR_WIDTH - nb, MOBA_BLOCK), BF16)

    def qk(e, j, slot, near=False):
        rows = pl.ds(pl.multiple_of(j * blk_rows, blk_rows), blk_rows)
        if near:
            s_ref[slot, e] = _dot(kaug_ref[e // 2, rows, 0:PAIR_WIDTH], qaug_ref[e, 0:PAIR_WIDTH, :])
        else:
            s_ref[slot, e] = _dot(kaug_ref[e // 2, rows, :], qaug_ref[e])

    def sm(e, j, slot, bias=None, first=False):
        st = s_ref[slot, e]
        if bias is not None:
            st = st + bias
        mj = jnp.max(st, axis=0, keepdims=True)
        if first:
            m_ref[e] = mj
            acc_ref[e] = _dot(vaug_ref[e, j], jnp.exp2(st - mj).astype(BF16))
        else:
            m_old = m_ref[e]
            m_new = jnp.maximum(m_old, mj)
            pv = _dot(vaug_ref[e, j], jnp.exp2(st - m_new).astype(BF16))
            acc_ref[e] = acc_ref[e] * jnp.exp2(m_old - m_new) + pv
            m_ref[e] = m_new

    clamp = lambda j: jnp.minimum(j, i)
    heads = range(MOBA_HEADS)
    jp = jnp.maximum(i - 1, 0)
    for e in heads:
        qk(e, i, 0, near=True)
        qk(e, jp, 1, near=True)
    for e in heads:
        sm(e, i, 0, bias=tnear_ref[e, blk_rows:, :], first=True)
        qk(e, clamp(0), 2)
    for e in heads:
        sm(e, jp, 1, bias=tnear_ref[e, 0:blk_rows, :] + selprev_ref[0, e])
        qk(e, clamp(1), 3)

    def far(t, carry):
        j = 4 * t
        for done, slot in enumerate((2, 3, 0, 1)):
            for e in heads:
                sm(e, clamp(j + done), slot)
                qk(e, clamp(j + done + 2), (slot + 2) % 4)
        return carry

    lax.fori_loop(0, lax.shift_right_logical(i + 2, 2), far, 0)

    outs = []
    for e in range(MOBA_HEADS):
        acc = acc_ref[e]
        outs.append((acc[:d] / acc[d:d + 1]).T)
    out_ref[0] = jnp.concatenate(outs, axis=1).astype(BF16)


def _moba(mqt, mk, mvt, selfar, selprev, tnear):
    b, s, _ = mk.shape
    nb = s // MOBA_BLOCK
    assert nb <= PAIR_WIDTH
    nh = MOBA_HEADS
    return pl.pallas_call(
        _moba_kernel,
        grid=(b, nb),
        in_specs=[
            pl.BlockSpec((1, 1, MOBA_WIDTH, MOBA_BLOCK), lambda bi, i: (bi, i, 0, 0)),
            pl.BlockSpec((1, MOBA_BLOCK, MOBA_WIDTH), lambda bi, i: (bi, i, 0)),
            pl.BlockSpec((1, 1, MOBA_WIDTH, MOBA_BLOCK), lambda bi, i: (bi, i, 0, 0)),
            pl.BlockSpec((1, nh, nb, MOBA_BLOCK), lambda bi, i: (bi, 0, 0, i)),
            pl.BlockSpec((1, nh, 1, MOBA_BLOCK), lambda bi, i: (bi, 0, 0, i)),
            pl.BlockSpec((nh, 2 * MOBA_BLOCK, MOBA_BLOCK), lambda bi, i: (0, 0, 0),
                         pipeline_mode=pl.Buffered(1)),
        ],
        out_specs=pl.BlockSpec((1, MOBA_BLOCK, MOBA_WIDTH), lambda bi, i: (bi, i, 0)),
        out_shape=jax.ShapeDtypeStruct((b, s, MOBA_WIDTH), BF16),
        scratch_shapes=[
            pltpu.VMEM((nh // 2, s, 2 * PAIR_WIDTH), BF16),
            pltpu.VMEM((nh, nb, MOBA_HEAD_DIM + ONES_ROWS, MOBA_BLOCK), BF16),
            pltpu.VMEM((nh, 2 * PAIR_WIDTH, MOBA_BLOCK), BF16),
            pltpu.VMEM((4, nh, MOBA_BLOCK, MOBA_BLOCK), F32),
            pltpu.VMEM((nh, 1, MOBA_BLOCK), F32),
            pltpu.VMEM((nh, MOBA_HEAD_DIM + ONES_ROWS, MOBA_BLOCK), F32),
        ],
        compiler_params=pltpu.CompilerParams(
            dimension_semantics=("arbitrary", "arbitrary"), vmem_limit_bytes=VMEM_LIMIT),
        name="moba",
    )(mqt, mk, mvt, selfar, selprev, tnear)


def _rms(x, g):
    return x * lax.rsqrt(jnp.mean(x * x, axis=-1, keepdims=True) + EPS) * g


def _out_ffn_kernel(x_ref, ret_ref, moba_ref, wo_ref, wg_ref, wu_ref, wd_ref,
                    g_post_mix_ref, g_pre_ffn_ref, g_post_ffn_ref, out_ref):
    mix_in = jnp.concatenate([ret_ref[...], moba_ref[...]], axis=1)
    x1 = x_ref[...] + _rms(_dot(mix_in, wo_ref[...]), g_post_mix_ref[...])
    h = _rms(x1, g_pre_ffn_ref[...]).astype(BF16)
    f = None
    for lo, hi in FFN_CHUNKS:
        cols = slice(lo, hi)
        gate = _dot(h, wg_ref[:, cols])
        up = _dot(h, wu_ref[:, cols])
        act = (gate * jax.nn.sigmoid(gate) * up).astype(BF16)
        part = _dot(act, wd_ref[cols, :])
        f = part if f is None else f + part
    out_ref[...] = x1 + _rms(f, g_post_ffn_ref[...])


def _out_ffn(x2, ret2, moba2, wo, wg, wu, wd, g_post_mix, g_pre_ffn, g_post_ffn):
    n, d = x2.shape
    tm = min(FFN_TOKENS, n)
    tok = lambda w: pl.BlockSpec((tm, w), lambda t: (t, 0))
    resident = lambda a: pl.BlockSpec(a.shape, lambda t: (0, 0), pipeline_mode=pl.Buffered(1))
    return pl.pallas_call(
        _out_ffn_kernel,
        grid=(n // tm,),
        in_specs=[tok(d), tok(RET_WIDTH), tok(MOBA_WIDTH),
                  resident(wo), resident(wg), resident(wu), resident(wd),
                  resident(g_post_mix), resident(g_pre_ffn), resident(g_post_ffn)],
        out_specs=tok(d),
        out_shape=jax.ShapeDtypeStruct((n, d), F32),
        compiler_params=pltpu.CompilerParams(
            dimension_semantics=("arbitrary",), vmem_limit_bytes=VMEM_LIMIT),
        name="out_ffn",
    )(x2, ret2, moba2, wo, wg, wu, wd, g_post_mix, g_pre_ffn, g_post_ffn)


def _rotary_tables(s):
    half = RET_QK_DIM // 2
    inv_freq = ROPE_BASE ** (-np.arange(half, dtype=np.float64) / half)
    ang = np.arange(s, dtype=np.float64)[:, None] * inv_freq[None, :]
    cos, sin = np.cos(ang), np.sin(ang)
    cq = np.tile(np.concatenate([cos, cos], axis=1), (1, RET_HEADS))
    sq = np.tile(np.concatenate([-sin, sin], axis=1), (1, RET_HEADS))
    return tuple(np.ascontiguousarray(t, dtype=np.float32) for t in (cq, sq, cos.T, sin.T))


def _retention_tables():
    c = RET_CHUNK
    log_gamma = np.log1p(-np.exp(np.linspace(math.log(1.0 / 32), math.log(1.0 / 512), RET_HEADS)))
    idx = np.arange(c, dtype=np.float64)
    diff = idx[:, None] - idx[None, :]
    dmat = np.where(diff >= 0, np.exp(np.maximum(diff, 0.0)[None] * log_gamma[:, None, None]), 0.0)
    q_decay = np.exp((idx + 1.0)[None, :] * log_gamma[:, None])
    k_decay = np.exp((c - 1.0 - idx)[None, :] * log_gamma[:, None])
    chunk_decay = np.exp(c * log_gamma)
    qd = np.broadcast_to(q_decay[:, :, None], (RET_HEADS, c, RET_V_DIM))
    kd = k_decay[:, None, :]
    cd = np.broadcast_to(chunk_decay[:, None, None], (RET_HEADS, 1, RET_V_DIM))
    return tuple(np.ascontiguousarray(t, dtype=np.float32) for t in (dmat, qd, kd, cd))


def _t5_bucket(rel):
    n = np.maximum(rel, 0)
    max_exact = REL_BUCKETS // 2
    large = max_exact + (np.log(np.maximum(n, 1) / max_exact) / math.log(REL_MAX_DIST / max_exact)
                         * (REL_BUCKETS - max_exact)).astype(np.int64)
    return np.where(n < max_exact, n, np.minimum(large, REL_BUCKETS - 1))


def _toeplitz(vec):
    h, two_n = vec.shape
    n = two_n // 2
    ext = jnp.concatenate([vec, jnp.zeros((h, 1), vec.dtype)], axis=1)
    skew = jnp.tile(ext, (1, n))[:, :n * two_n].reshape(h, n, two_n)
    return skew[:, :, n:]


def _bias_tables(rel_bias):
    bias_t = rel_bias.T.astype(F32) * LOG2E
    n = MOBA_BLOCK
    one_hot = (_t5_bucket(np.arange(2 * n))[:, None] == np.arange(REL_BUCKETS)).astype(np.float32)
    by_dist = jnp.einsum("rn,hn->hr", one_hot, bias_t, precision=lax.Precision.HIGHEST)
    tprev = _toeplitz(by_dist)
    own_vec = jnp.concatenate([jnp.full((bias_t.shape[0], n), NEG, F32), by_dist[:, :n]], axis=1)
    town = _toeplitz(own_vec)
    cfar = bias_t[:, REL_BUCKETS - 1]
    return jnp.concatenate([tprev, town], axis=1), cfar


def kernel(x, w_in, w_out, pre_mix_norm, post_mix_norm, pre_ffn_norm, post_ffn_norm,
           rel_bias, w_gate, w_up, w_down):
    b, s, d = x.shape
    assert d == D_MODEL and s % IN_TOKENS == 0 and MOBA_BLOCK + 1 >= REL_MAX_DIST
    depth = w_in.shape[0]

    cq, sq, ckt, skt = _rotary_tables(s)
    dmat, qd, kd, cd = _retention_tables()
    tnear, cfar = _bias_tables(rel_bias)

    sizes = [RET_QK_WIDTH, RET_QK_WIDTH, RET_WIDTH, RET_WIDTH, MOBA_WIDTH, MOBA_WIDTH, MOBA_WIDTH]
    o_rq, o_rk, o_rv, o_rg, o_mq, o_mk, o_mv, _ = np.cumsum([0] + sizes).tolist()

    for layer in range(depth):
        w = w_in[layer]
        w_nat = jnp.concatenate(
            [w[:, o_rq:o_rk], w[:, o_rv:o_mq], w[:, o_mk:o_mv]], axis=1).astype(BF16)
        w_tr = jnp.concatenate([w[:, o_rk:o_rv], w[:, o_mq:o_mk], w[:, o_mv:]], axis=1).T.astype(BF16)

        rq, rv, sg, mk, kmean, rkt, mqt, mvt = _in_proj(
            x, pre_mix_norm[layer][None, :], w_nat, w_tr, cq, sq, ckt, skt)

        ret = _retention(rq, rkt, rv, sg, dmat, qd, kd, cd)

        nb = s // MOBA_BLOCK
        kmean = kmean.reshape(b, nb, MOBA_HEADS, MOBA_HEAD_DIM).transpose(0, 2, 1, 3)
        selfar, selprev = _moba_gate(cfar, mqt, kmean)
        moba = _moba(mqt, mk, mvt, selfar, selprev, tnear)

        x = _out_ffn(
            x.reshape(b * s, d), ret.reshape(b * s, RET_WIDTH), moba.reshape(b * s, MOBA_WIDTH),
            w_out[layer].astype(BF16), w_gate[layer].astype(BF16), w_up[layer].astype(BF16),
            w_down[layer].astype(BF16), post_mix_norm[layer][None, :], pre_ffn_norm[layer][None, :],
            post_ffn_norm[layer][None, :]).reshape(b, s, d)
    return x
```

```python
import functools
import math

import jax
import jax.numpy as jnp
import numpy as np
from jax import lax
from jax.experimental import pallas as pl
from jax.experimental.pallas import tpu as pltpu

F32 = jnp.float32
BF16 = jnp.bfloat16

D_MODEL = 1024
RET_HEADS = 4
RET_QK_DIM = 64
RET_V_DIM = 128
RET_CHUNK = 128
RET_QK_WIDTH = RET_HEADS * RET_QK_DIM
RET_WIDTH = RET_HEADS * RET_V_DIM
MOBA_HEADS = 8
MOBA_HEAD_DIM = 64
MOBA_WIDTH = MOBA_HEADS * MOBA_HEAD_DIM
MOBA_BLOCK = 256
MOBA_TOPK = 3
REL_BUCKETS = 32
REL_MAX_DIST = 128
D_FF = 2816
EPS = 1e-6
ROPE_BASE = 10000.0

NEG = -1e30
LOG2E = math.log2(math.e)

IN_TOKENS = 1024
RET_TOKENS = 1024
FFN_TOKENS = 512
MXU_TILE = 256
FFN_CHUNKS = ((0, 6 * MXU_TILE), (6 * MXU_TILE, D_FF))
VMEM_LIMIT = 56 * 1024 * 1024

NAT_WIDTH = RET_QK_WIDTH + 2 * RET_WIDTH + MOBA_WIDTH
TR_WIDTH = RET_QK_WIDTH + 2 * MOBA_WIDTH


def _nt_dot(a, b):
    return lax.dot_general(a, b, (((1,), (1,)), ((), ())), preferred_element_type=F32)


def _dot(a, b):
    return jnp.dot(a, b, preferred_element_type=F32)


def _in_proj_kernel(x_ref, g_ref, wn_ref, wt_ref, cq_ref, sq_ref, ckt_ref, skt_ref,
                    rq_ref, rv_ref, sg_ref, mk_ref, kmean_ref, rkt_ref, mqt_ref, mvt_ref):
    tm = x_ref.shape[1]
    x = x_ref[0]
    ms = jnp.mean(x * x, axis=-1, keepdims=True)
    h = (x * lax.rsqrt(ms + EPS) * g_ref[...]).astype(BF16)

    pn = _dot(h, wn_ref[...])
    pt = _nt_dot(wt_ref[...], h)

    rq = pn[:, :RET_QK_WIDTH]
    lane = lax.broadcasted_iota(jnp.int32, rq.shape, 1)
    first_half = (lane % RET_QK_DIM) < (RET_QK_DIM // 2)
    partner = jnp.where(first_half,
                        pltpu.roll(rq, RET_QK_WIDTH - RET_QK_DIM // 2, 1),
                        pltpu.roll(rq, RET_QK_DIM // 2, 1))
    rq_ref[0] = (rq * cq_ref[...] + partner * sq_ref[...]).astype(BF16)

    rv_ref[0] = pn[:, RET_QK_WIDTH:RET_QK_WIDTH + RET_WIDTH].astype(BF16)
    rg = pn[:, RET_QK_WIDTH + RET_WIDTH:RET_QK_WIDTH + 2 * RET_WIDTH]
    sg_ref[0] = (rg * jax.nn.sigmoid(rg)).astype(BF16)

    mk = pn[:, RET_QK_WIDTH + 2 * RET_WIDTH:]
    mk_ref[0] = mk.astype(BF16)
    for blk in range(tm // MOBA_BLOCK):
        kmean_ref[0, 0, blk:blk + 1, :] = jnp.mean(
            mk[blk * MOBA_BLOCK:(blk + 1) * MOBA_BLOCK], axis=0, keepdims=True)

    half = RET_QK_DIM // 2
    cos_t = ckt_ref[...]
    sin_t = skt_ref[...]
    parts = []
    for hd in range(RET_HEADS):
        x1 = pt[hd * RET_QK_DIM:hd * RET_QK_DIM + half]
        x2 = pt[hd * RET_QK_DIM + half:(hd + 1) * RET_QK_DIM]
        parts.append(x1 * cos_t - x2 * sin_t)
        parts.append(x2 * cos_t + x1 * sin_t)
    rkt = (jnp.concatenate(parts, axis=0) * (RET_QK_DIM ** -0.5)).astype(BF16)
    for c in range(tm // RET_CHUNK):
        rkt_ref[0, c] = rkt[:, c * RET_CHUNK:(c + 1) * RET_CHUNK]

    mqt = (pt[RET_QK_WIDTH:RET_QK_WIDTH + MOBA_WIDTH] * (MOBA_HEAD_DIM ** -0.5 * LOG2E)).astype(BF16)
    mvt = pt[RET_QK_WIDTH + MOBA_WIDTH:].astype(BF16)
    for blk in range(tm // MOBA_BLOCK):
        mqt_ref[0, blk] = mqt[:, blk * MOBA_BLOCK:(blk + 1) * MOBA_BLOCK]
        mvt_ref[0, blk] = mvt[:, blk * MOBA_BLOCK:(blk + 1) * MOBA_BLOCK]


def _in_proj(x, gain, w_nat, w_tr, cq, sq, ckt, skt):
    b, s, d = x.shape
    tm = IN_TOKENS
    ns = s // tm
    bpt = tm // MOBA_BLOCK
    cpt = tm // RET_CHUNK
    const = lambda si, bi: (0, 0)
    out_shape = (
        jax.ShapeDtypeStruct((b, s, RET_QK_WIDTH), BF16),
        jax.ShapeDtypeStruct((b, s, RET_WIDTH), BF16),
        jax.ShapeDtypeStruct((b, s, RET_WIDTH), BF16),
        jax.ShapeDtypeStruct((b, s, MOBA_WIDTH), BF16),
        jax.ShapeDtypeStruct((b, ns, bpt, MOBA_WIDTH), F32),
        jax.ShapeDtypeStruct((b, s // RET_CHUNK, RET_QK_WIDTH, RET_CHUNK), BF16),
        jax.ShapeDtypeStruct((b, s // MOBA_BLOCK, MOBA_WIDTH, MOBA_BLOCK), BF16),
        jax.ShapeDtypeStruct((b, s // MOBA_BLOCK, MOBA_WIDTH, MOBA_BLOCK), BF16),
    )
    tok = lambda w: pl.BlockSpec((1, tm, w), lambda si, bi: (bi, si, 0))
    return pl.pallas_call(
        _in_proj_kernel,
        grid=(ns, b),
        in_specs=[
            tok(d),
            pl.BlockSpec((1, d), const),
            pl.BlockSpec((d, NAT_WIDTH), const),
            pl.BlockSpec((TR_WIDTH, d), const),
            pl.BlockSpec((tm, RET_QK_WIDTH), lambda si, bi: (si, 0)),
            pl.BlockSpec((tm, RET_QK_WIDTH), lambda si, bi: (si, 0)),
            pl.BlockSpec((RET_QK_DIM // 2, tm), lambda si, bi: (0, si)),
            pl.BlockSpec((RET_QK_DIM // 2, tm), lambda si, bi: (0, si)),
        ],
        out_specs=(
            tok(RET_QK_WIDTH), tok(RET_WIDTH), tok(RET_WIDTH), tok(MOBA_WIDTH),
            pl.BlockSpec((1, 1, bpt, MOBA_WIDTH), lambda si, bi: (bi, si, 0, 0)),
            pl.BlockSpec((1, cpt, RET_QK_WIDTH, RET_CHUNK), lambda si, bi: (bi, si, 0, 0)),
            pl.BlockSpec((1, bpt, MOBA_WIDTH, MOBA_BLOCK), lambda si, bi: (bi, si, 0, 0)),
            pl.BlockSpec((1, bpt, MOBA_WIDTH, MOBA_BLOCK), lambda si, bi: (bi, si, 0, 0)),
        ),
        out_shape=out_shape,
        compiler_params=pltpu.CompilerParams(
            dimension_semantics=("arbitrary", "arbitrary"), vmem_limit_bytes=VMEM_LIMIT),
        name="in_proj",
    )(x, gain, w_nat, w_tr, cq, sq, ckt, skt)


def _retention_kernel(rq_ref, rkt_ref, rv_ref, sg_ref, dmat_ref, qd_ref, kd_ref, cd_ref,
                      out_ref, state_ref):
    @pl.when(pl.program_id(1) == 0)
    def _():
        state_ref[...] = jnp.zeros_like(state_ref)

    n_chunks = rq_ref.shape[1] // RET_CHUNK

    def chunk(c, carry):
        r0 = pl.multiple_of(c * RET_CHUNK, RET_CHUNK)
        rows = pl.ds(r0, RET_CHUNK)
        for hd in range(RET_HEADS):
            q = rq_ref[0, rows, hd * RET_QK_DIM:(hd + 1) * RET_QK_DIM]
            kt = rkt_ref[0, c, hd * RET_QK_DIM:(hd + 1) * RET_QK_DIM, :]
            v = rv_ref[0, rows, hd * RET_V_DIM:(hd + 1) * RET_V_DIM]
            state = state_ref[hd]
            scores = _dot(q, kt) * dmat_ref[hd]
            o = _dot(scores.astype(BF16), v) + qd_ref[hd] * _dot(q, state.astype(BF16))
            kts = (kt.astype(F32) * kd_ref[hd]).astype(BF16)
            state_ref[hd] = state * cd_ref[hd] + _dot(kts, v)
            ms = jnp.mean(o * o, axis=-1, keepdims=True)
            gate = sg_ref[0, rows, hd * RET_V_DIM:(hd + 1) * RET_V_DIM].astype(F32)
            out_ref[0, rows, hd * RET_V_DIM:(hd + 1) * RET_V_DIM] = (
                o * lax.rsqrt(ms + EPS) * gate).astype(BF16)
        return carry

    lax.fori_loop(0, n_chunks, chunk, 0, unroll=8)


def _retention(rq, rkt, rv, sg, dmat, qd, kd, cd):
    b, s, _ = rq.shape
    tc = min(RET_TOKENS, s)
    tok = lambda w: pl.BlockSpec((1, tc, w), lambda bi, si: (bi, si, 0))
    tab = lambda a: pl.BlockSpec(a.shape, lambda bi, si: (0,) * a.ndim)
    return pl.pallas_call(
        _retention_kernel,
        grid=(b, s // tc),
        in_specs=[
            tok(RET_QK_WIDTH),
            pl.BlockSpec((1, tc // RET_CHUNK, RET_QK_WIDTH, RET_CHUNK), lambda bi, si: (bi, si, 0, 0)),
            tok(RET_WIDTH), tok(RET_WIDTH),
            tab(dmat), tab(qd), tab(kd), tab(cd),
        ],
        out_specs=tok(RET_WIDTH),
        out_shape=jax.ShapeDtypeStruct((b, s, RET_WIDTH), BF16),
        scratch_shapes=[pltpu.VMEM((RET_HEADS, RET_QK_DIM, RET_V_DIM), F32)],
        compiler_params=pltpu.CompilerParams(
            dimension_semantics=("arbitrary", "arbitrary"), vmem_limit_bytes=VMEM_LIMIT),
        name="retention",
    )(rq, rkt, rv, sg, dmat, qd, kd, cd)


ONES_ROWS = 16
PAIR_WIDTH = 2 * MOBA_HEAD_DIM
GATE_BLOCKS = 4


def _moba_gate_kernel(cfar_ref, qt_ref, km_ref, selfar_ref, selprev_ref):
    d = MOBA_HEAD_DIM
    nb = km_ref.shape[2]
    n_q = qt_ref.shape[1]
    shape = (nb, n_q * MOBA_BLOCK)
    blk = lax.broadcasted_iota(jnp.int32, shape, 0)
    i = pl.program_id(1) * n_q + lax.broadcasted_iota(jnp.int32, shape, 1) // MOBA_BLOCK
    blk_f = blk.astype(F32)
    for h in range(MOBA_HEADS):
        qt = jnp.concatenate([qt_ref[0, c, h * d:(h + 1) * d, :] for c in range(n_q)], axis=1)
        km = km_ref[0, h]
        km_hi = km.astype(BF16)
        km_lo = (km - km_hi.astype(F32)).astype(BF16)
        gate = _dot(km_hi, qt) + _dot(km_lo, qt)
        gate = jnp.where(blk < i, gate, NEG)
        chosen = jnp.zeros(gate.shape, F32)
        for _ in range(MOBA_TOPK):
            best = jnp.max(gate, axis=0, keepdims=True)
            idx = jnp.min(jnp.where(gate == best, blk_f, float(nb)), axis=0, keepdims=True)
            hit = blk_f == idx
            chosen = jnp.where(hit & (best > 0.5 * NEG), 1.0, chosen)
            gate = jnp.where(hit, NEG, gate)
        picked = chosen > 0.5
        selfar_ref[0, h] = jnp.where(picked & (blk < i - 1), cfar_ref[h], NEG).astype(BF16)
        prev_hit = jnp.max(jnp.where(picked & (blk == i - 1), 1.0, 0.0), axis=0, keepdims=True)
        selprev_ref[0, h] = jnp.where(prev_hit > 0.5, 0.0, NEG)


def _moba_gate(cfar, mqt, kmean):
    b, nb, _, _ = mqt.shape
    s = nb * MOBA_BLOCK
    n_q = math.gcd(GATE_BLOCKS, nb)
    width = n_q * MOBA_BLOCK
    return pl.pallas_call(
        _moba_gate_kernel,
        grid=(b, nb // n_q),
        in_specs=[
            pl.BlockSpec(memory_space=pltpu.SMEM),
            pl.BlockSpec((1, n_q, MOBA_WIDTH, MOBA_BLOCK), lambda bi, i: (bi, i, 0, 0)),
            pl.BlockSpec((1, MOBA_HEADS, nb, MOBA_HEAD_DIM), lambda bi, i: (bi, 0, 0, 0)),
        ],
        out_specs=(
            pl.BlockSpec((1, MOBA_HEADS, nb, width), lambda bi, i: (bi, 0, 0, i)),
            pl.BlockSpec((1, MOBA_HEADS, 1, width), lambda bi, i: (bi, 0, 0, i)),
        ),
        out_shape=(
            jax.ShapeDtypeStruct((b, MOBA_HEADS, nb, s), BF16),
            jax.ShapeDtypeStruct((b, MOBA_HEADS, 1, s), F32),
        ),
        compiler_params=pltpu.CompilerParams(
            dimension_semantics=("arbitrary", "arbitrary"), vmem_limit_bytes=VMEM_LIMIT),
        name="moba_gate",
    )(cfar, mqt, kmean)


def _moba_kernel(qt_ref, k_ref, vt_ref, selfar_ref, selprev_ref, tnear_ref,
                 out_ref, kaug_ref, vaug_ref, qaug_ref, s_ref, m_ref, acc_ref):
    i = pl.program_id(1)
    d = MOBA_HEAD_DIM
    nb = vaug_ref.shape[1]
    s_len = kaug_ref.shape[1]
    blk_rows = MOBA_BLOCK
    n_pairs = MOBA_HEADS // 2

    @pl.when((pl.program_id(0) == 0) & (i == 0))
    def _():
        row_blk = lax.broadcasted_iota(jnp.int32, (s_len, PAIR_WIDTH), 0) // MOBA_BLOCK
        lane = lax.broadcasted_iota(jnp.int32, (s_len, PAIR_WIDTH), 1)
        block_one_hot = jnp.where(row_blk == lane, 1.0, 0.0).astype(BF16)
        for pe in range(n_pairs):
            kaug_ref[pe, :, PAIR_WIDTH:] = block_one_hot
        for e in range(MOBA_HEADS):
            vaug_ref[e, :, d:, :] = jnp.ones((nb, ONES_ROWS, MOBA_BLOCK), BF16)

    own_rows = pl.ds(pl.multiple_of(i * blk_rows, blk_rows), blk_rows)
    for pe in range(n_pairs):
        kaug_ref[pe, own_rows, 0:PAIR_WIDTH] = k_ref[0, :, pe * PAIR_WIDTH:(pe + 1) * PAIR_WIDTH]
    for e in range(MOBA_HEADS):
        vaug_ref[e, i, 0:d, :] = vt_ref[0, 0, e * d:(e + 1) * d, :]

    prow = lax.broadcasted_iota(jnp.int32, (PAIR_WIDTH, MOBA_BLOCK), 0)
    for e in range(MOBA_HEADS):
        pe, he = divmod(e, 2)
        qt2 = qt_ref[0, 0, pe * PAIR_WIDTH:(pe + 1) * PAIR_WIDTH, :]
        qaug_ref[e, 0:PAIR_WIDTH, :] = jnp.where(
            (prow >= he * d) & (prow < (he + 1) * d), qt2, jnp.zeros_like(qt2))
        qaug_ref[e, PAIR_WIDTH:PAIR_WIDTH + nb, :] = selfar_ref[0, e]
        qaug_ref[e, PAIR_WIDTH + nb:, :] = jnp.zeros((PAIR_WIDTH - nb, MOBA_BLOCK), BF16)

    def qk(e, j, slot, near=False):
        rows = pl.ds(pl.multiple_of(j * blk_rows, blk_rows), blk_rows)
        if near:
            s_ref[slot, e] = _dot(kaug_ref[e // 2, rows, 0:PAIR_WIDTH], qaug_ref[e, 0:PAIR_WIDTH, :])
        else:
            s_ref[slot, e] = _dot(kaug_ref[e // 2, rows, :], qaug_ref[e])

    def sm(e, j, slot, bias=None, first=False):
        st = s_ref[slot, e]
        if bias is not None:
            st = st + bias
        mj = jnp.max(st, axis=0, keepdims=True)
        if first:
            m_ref[e] = mj
            acc_ref[e] = _dot(vaug_ref[e, j], jnp.exp2(st - mj).astype(BF16))
        else:
            m_old = m_ref[e]
            m_new = jnp.maximum(m_old, mj)
            pv = _dot(vaug_ref[e, j], jnp.exp2(st - m_new).astype(BF16))
            acc_ref[e] = acc_ref[e] * jnp.exp2(m_old - m_new) + pv
            m_ref[e] = m_new

    clamp = lambda j: jnp.minimum(j, i)
    heads = range(MOBA_HEADS)
    jp = jnp.maximum(i - 1, 0)
    for blk, slot, near in ((i, 0, True), (jp, 1, True), (clamp(0), 2, False), (clamp(1), 3, False)):
        for e in heads:
            qk(e, blk, slot, near=near)
    for e in heads:
        sm(e, i, 0, bias=tnear_ref[e, blk_rows:, :], first=True)
    for e in heads:
        sm(e, jp, 1, bias=tnear_ref[e, 0:blk_rows, :] + selprev_ref[0, e])

    def far(t, carry):
        j = 4 * t
        for half in (0, 2):
            for e in heads:
                qk(e, clamp(j + half + 2), half)
            for e in heads:
                qk(e, clamp(j + half + 3), half + 1)
            for e in heads:
                sm(e, clamp(j + half), (half + 2) % 4)
            for e in heads:
                sm(e, clamp(j + half + 1), (half + 3) % 4)
        return carry

    lax.fori_loop(0, lax.shift_right_logical(i + 2, 2), far, 0)

    outs = []
    for e in range(MOBA_HEADS):
        acc = acc_ref[e]
        outs.append((acc[:d] / acc[d:d + 1]).T)
    out_ref[0] = jnp.concatenate(outs, axis=1).astype(BF16)


def _moba(mqt, mk, mvt, selfar, selprev, tnear):
    b, s, _ = mk.shape
    nb = s // MOBA_BLOCK
    assert nb <= PAIR_WIDTH
    nh = MOBA_HEADS
    return pl.pallas_call(
        _moba_kernel,
        grid=(b, nb),
        in_specs=[
            pl.BlockSpec((1, 1, MOBA_WIDTH, MOBA_BLOCK), lambda bi, i: (bi, i, 0, 0)),
            pl.BlockSpec((1, MOBA_BLOCK, MOBA_WIDTH), lambda bi, i: (bi, i, 0)),
            pl.BlockSpec((1, 1, MOBA_WIDTH, MOBA_BLOCK), lambda bi, i: (bi, i, 0, 0)),
            pl.BlockSpec((1, nh, nb, MOBA_BLOCK), lambda bi, i: (bi, 0, 0, i)),
            pl.BlockSpec((1, nh, 1, MOBA_BLOCK), lambda bi, i: (bi, 0, 0, i)),
            pl.BlockSpec((nh, 2 * MOBA_BLOCK, MOBA_BLOCK), lambda bi, i: (0, 0, 0),
                         pipeline_mode=pl.Buffered(1)),
        ],
        out_specs=pl.BlockSpec((1, MOBA_BLOCK, MOBA_WIDTH), lambda bi, i: (bi, i, 0)),
        out_shape=jax.ShapeDtypeStruct((b, s, MOBA_WIDTH), BF16),
        scratch_shapes=[
            pltpu.VMEM((nh // 2, s, 2 * PAIR_WIDTH), BF16),
            pltpu.VMEM((nh, nb, MOBA_HEAD_DIM + ONES_ROWS, MOBA_BLOCK), BF16),
            pltpu.VMEM((nh, 2 * PAIR_WIDTH, MOBA_BLOCK), BF16),
            pltpu.VMEM((4, nh, MOBA_BLOCK, MOBA_BLOCK), F32),
            pltpu.VMEM((nh, 1, MOBA_BLOCK), F32),
            pltpu.VMEM((nh, MOBA_HEAD_DIM + ONES_ROWS, MOBA_BLOCK), F32),
        ],
        compiler_params=pltpu.CompilerParams(
            dimension_semantics=("arbitrary", "arbitrary"), vmem_limit_bytes=VMEM_LIMIT),
        name="moba",
    )(mqt, mk, mvt, selfar, selprev, tnear)


def _rms(x, g):
    return x * lax.rsqrt(jnp.mean(x * x, axis=-1, keepdims=True) + EPS) * g


def _out_ffn_kernel(x_ref, ret_ref, moba_ref, wo_ref, wg_ref, wu_ref, wd_ref,
                    g_post_mix_ref, g_pre_ffn_ref, g_post_ffn_ref, out_ref):
    mix_in = jnp.concatenate([ret_ref[...], moba_ref[...]], axis=1)
    x1 = x_ref[...] + _rms(_dot(mix_in, wo_ref[...]), g_post_mix_ref[...])
    h = _rms(x1, g_pre_ffn_ref[...]).astype(BF16)
    f = None
    for lo, hi in FFN_CHUNKS:
        cols = slice(lo, hi)
        gate = _dot(h, wg_ref[:, cols])
        up = _dot(h, wu_ref[:, cols])
        act = (gate * jax.nn.sigmoid(gate) * up).astype(BF16)
        part = _dot(act, wd_ref[cols, :])
        f = part if f is None else f + part
    out_ref[...] = x1 + _rms(f, g_post_ffn_ref[...])


def _out_ffn(x2, ret2, moba2, wo, wg, wu, wd, g_post_mix, g_pre_ffn, g_post_ffn):
    n, d = x2.shape
    tm = min(FFN_TOKENS, n)
    tok = lambda w: pl.BlockSpec((tm, w), lambda t: (t, 0))
    resident = lambda a: pl.BlockSpec(a.shape, lambda t: (0, 0), pipeline_mode=pl.Buffered(1))
    return pl.pallas_call(
        _out_ffn_kernel,
        grid=(n // tm,),
        in_specs=[tok(d), tok(RET_WIDTH), tok(MOBA_WIDTH),
                  resident(wo), resident(wg), resident(wu), resident(wd),
                  resident(g_post_mix), resident(g_pre_ffn), resident(g_post_ffn)],
        out_specs=tok(d),
        out_shape=jax.ShapeDtypeStruct((n, d), F32),
        compiler_params=pltpu.CompilerParams(
            dimension_semantics=("arbitrary",), vmem_limit_bytes=VMEM_LIMIT),
        name="out_ffn",
    )(x2, ret2, moba2, wo, wg, wu, wd, g_post_mix, g_pre_ffn, g_post_ffn)


def _rotary_tables(s):
    half = RET_QK_DIM // 2
    inv_freq = ROPE_BASE ** (-np.arange(half, dtype=np.float64) / half)
    ang = np.arange(s, dtype=np.float64)[:, None] * inv_freq[None, :]
    cos, sin = np.cos(ang), np.sin(ang)
    cq = np.tile(np.concatenate([cos, cos], axis=1), (1, RET_HEADS))
    sq = np.tile(np.concatenate([-sin, sin], axis=1), (1, RET_HEADS))
    return tuple(np.ascontiguousarray(t, dtype=np.float32) for t in (cq, sq, cos.T, sin.T))


def _retention_tables():
    c = RET_CHUNK
    log_gamma = np.log1p(-np.exp(np.linspace(math.log(1.0 / 32), math.log(1.0 / 512), RET_HEADS)))
    idx = np.arange(c, dtype=np.float64)
    diff = idx[:, None] - idx[None, :]
    dmat = np.where(diff >= 0, np.exp(np.maximum(diff, 0.0)[None] * log_gamma[:, None, None]), 0.0)
    q_decay = np.exp((idx + 1.0)[None, :] * log_gamma[:, None])
    k_decay = np.exp((c - 1.0 - idx)[None, :] * log_gamma[:, None])
    chunk_decay = np.exp(c * log_gamma)
    qd = np.broadcast_to(q_decay[:, :, None], (RET_HEADS, c, RET_V_DIM))
    kd = k_decay[:, None, :]
    cd = np.broadcast_to(chunk_decay[:, None, None], (RET_HEADS, 1, RET_V_DIM))
    return tuple(np.ascontiguousarray(t, dtype=np.float32) for t in (dmat, qd, kd, cd))


def _t5_bucket(rel):
    n = np.maximum(rel, 0)
    max_exact = REL_BUCKETS // 2
    large = max_exact + (np.log(np.maximum(n, 1) / max_exact) / math.log(REL_MAX_DIST / max_exact)
                         * (REL_BUCKETS - max_exact)).astype(np.int64)
    return np.where(n < max_exact, n, np.minimum(large, REL_BUCKETS - 1))


def _toeplitz(vec):
    h, two_n = vec.shape
    n = two_n // 2
    ext = jnp.concatenate([vec, jnp.zeros((h, 1), vec.dtype)], axis=1)
    skew = jnp.tile(ext, (1, n))[:, :n * two_n].reshape(h, n, two_n)
    return skew[:, :, n:]


def _bias_tables(rel_bias):
    bias_t = rel_bias.T.astype(F32) * LOG2E
    n = MOBA_BLOCK
    one_hot = (_t5_bucket(np.arange(2 * n))[:, None] == np.arange(REL_BUCKETS)).astype(np.float32)
    by_dist = jnp.einsum("rn,hn->hr", one_hot, bias_t, precision=lax.Precision.HIGHEST)
    tprev = _toeplitz(by_dist)
    own_vec = jnp.concatenate([jnp.full((bias_t.shape[0], n), NEG, F32), by_dist[:, :n]], axis=1)
    town = _toeplitz(own_vec)
    cfar = bias_t[:, REL_BUCKETS - 1]
    return jnp.concatenate([tprev, town], axis=1), cfar


def kernel(x, w_in, w_out, pre_mix_norm, post_mix_norm, pre_ffn_norm, post_ffn_norm,
           rel_bias, w_gate, w_up, w_down):
    b, s, d = x.shape
    assert d == D_MODEL and s % IN_TOKENS == 0 and MOBA_BLOCK + 1 >= REL_MAX_DIST
    depth = w_in.shape[0]

    cq, sq, ckt, skt = _rotary_tables(s)
    dmat, qd, kd, cd = _retention_tables()
    tnear, cfar = _bias_tables(rel_bias)

    sizes = [RET_QK_WIDTH, RET_QK_WIDTH, RET_WIDTH, RET_WIDTH, MOBA_WIDTH, MOBA_WIDTH, MOBA_WIDTH]
    o_rq, o_rk, o_rv, o_rg, o_mq, o_mk, o_mv, _ = np.cumsum([0] + sizes).tolist()

    for layer in range(depth):
        w = w_in[layer]
        w_nat = jnp.concatenate(
            [w[:, o_rq:o_rk], w[:, o_rv:o_mq], w[:, o_mk:o_mv]], axis=1).astype(BF16)
        w_tr = jnp.concatenate([w[:, o_rk:o_rv], w[:, o_mq:o_mk], w[:, o_mv:]], axis=1).T.astype(BF16)

        rq, rv, sg, mk, kmean, rkt, mqt, mvt = _in_proj(
            x, pre_mix_norm[layer][None, :], w_nat, w_tr, cq, sq, ckt, skt)

        ret = _retention(rq, rkt, rv, sg, dmat, qd, kd, cd)

        nb = s // MOBA_BLOCK
        kmean = kmean.reshape(b, nb, MOBA_HEADS, MOBA_HEAD_DIM).transpose(0, 2, 1, 3)
        selfar, selprev = _moba_gate(cfar, mqt, kmean)
        moba = _moba(mqt, mk, mvt, selfar, selprev, tnear)

        x = _out_ffn(
            x.reshape(b * s, d), ret.reshape(b * s, RET_WIDTH), moba.reshape(b * s, MOBA_WIDTH),
            w_out[layer].astype(BF16), w_gate[layer].astype(BF16), w_up[layer].astype(BF16),
            w_down[layer].astype(BF16), post_mix_norm[layer][None, :], pre_ffn_norm[layer][None, :],
            post_ffn_norm[layer][None, :]).reshape(b, s, d)
    return x
```

```python
import functools
import math

import jax
import jax.numpy as jnp
import numpy as np
from jax import lax
from jax.experimental import pallas as pl
from jax.experimental.pallas import tpu as pltpu

F32 = jnp.float32
BF16 = jnp.bfloat16

D_MODEL = 1024
RET_HEADS = 4
RET_QK_DIM = 64
RET_V_DIM = 128
RET_CHUNK = 128
RET_QK_WIDTH = RET_HEADS * RET_QK_DIM
RET_WIDTH = RET_HEADS * RET_V_DIM
MOBA_HEADS = 8
MOBA_HEAD_DIM = 64
MOBA_WIDTH = MOBA_HEADS * MOBA_HEAD_DIM
MOBA_BLOCK = 256
MOBA_TOPK = 3
REL_BUCKETS = 32
REL_MAX_DIST = 128
D_FF = 2816
EPS = 1e-6
ROPE_BASE = 10000.0

NEG = -1e30
LOG2E = math.log2(math.e)

IN_TOKENS = 1024
RET_TOKENS = 1024
FFN_TOKENS = 1024
FFN_ROW_GROUPS = 4
MXU_TILE = 256
FFN_CHUNKS = ((0, 6 * MXU_TILE), (6 * MXU_TILE, D_FF))
VMEM_LIMIT = 56 * 1024 * 1024

NAT_WIDTH = RET_QK_WIDTH + 2 * RET_WIDTH + MOBA_WIDTH
TR_WIDTH = RET_QK_WIDTH + 2 * MOBA_WIDTH


def _nt_dot(a, b):
    return lax.dot_general(a, b, (((1,), (1,)), ((), ())), preferred_element_type=F32)


def _dot(a, b):
    return jnp.dot(a, b, preferred_element_type=F32)


def _in_proj_kernel(x_ref, g_ref, wn_ref, wt_ref, cq_ref, sq_ref, ckt_ref, skt_ref,
                    rq_ref, rv_ref, sg_ref, mk_ref, kmean_ref, rkt_ref, mqt_ref, mvt_ref):
    tm = x_ref.shape[1]
    x = x_ref[0]
    ms = jnp.mean(x * x, axis=-1, keepdims=True)
    h = (x * lax.rsqrt(ms + EPS) * g_ref[...]).astype(BF16)

    pn = _dot(h, wn_ref[...])
    pt = _nt_dot(wt_ref[...], h)

    rq = pn[:, :RET_QK_WIDTH]
    lane = lax.broadcasted_iota(jnp.int32, rq.shape, 1)
    first_half = (lane % RET_QK_DIM) < (RET_QK_DIM // 2)
    partner = jnp.where(first_half,
                        pltpu.roll(rq, RET_QK_WIDTH - RET_QK_DIM // 2, 1),
                        pltpu.roll(rq, RET_QK_DIM // 2, 1))
    rq_ref[0] = (rq * cq_ref[...] + partner * sq_ref[...]).astype(BF16)

    rv_ref[0] = pn[:, RET_QK_WIDTH:RET_QK_WIDTH + RET_WIDTH].astype(BF16)
    rg = pn[:, RET_QK_WIDTH + RET_WIDTH:RET_QK_WIDTH + 2 * RET_WIDTH]
    sg_ref[0] = (rg * jax.nn.sigmoid(rg)).astype(BF16)

    mk = pn[:, RET_QK_WIDTH + 2 * RET_WIDTH:]
    mk_ref[0] = mk.astype(BF16)
    for blk in range(tm // MOBA_BLOCK):
        kmean_ref[0, 0, blk:blk + 1, :] = jnp.mean(
            mk[blk * MOBA_BLOCK:(blk + 1) * MOBA_BLOCK], axis=0, keepdims=True)

    half = RET_QK_DIM // 2
    cos_t = ckt_ref[...]
    sin_t = skt_ref[...]
    parts = []
    for hd in range(RET_HEADS):
        x1 = pt[hd * RET_QK_DIM:hd * RET_QK_DIM + half]
        x2 = pt[hd * RET_QK_DIM + half:(hd + 1) * RET_QK_DIM]
        parts.append(x1 * cos_t - x2 * sin_t)
        parts.append(x2 * cos_t + x1 * sin_t)
    rkt = (jnp.concatenate(parts, axis=0) * (RET_QK_DIM ** -0.5)).astype(BF16)
    for c in range(tm // RET_CHUNK):
        rkt_ref[0, c] = rkt[:, c * RET_CHUNK:(c + 1) * RET_CHUNK]

    mqt = (pt[RET_QK_WIDTH:RET_QK_WIDTH + MOBA_WIDTH] * (MOBA_HEAD_DIM ** -0.5 * LOG2E)).astype(BF16)
    mvt = pt[RET_QK_WIDTH + MOBA_WIDTH:].astype(BF16)
    for blk in range(tm // MOBA_BLOCK):
        mqt_ref[0, blk] = mqt[:, blk * MOBA_BLOCK:(blk + 1) * MOBA_BLOCK]
        mvt_ref[0, blk] = mvt[:, blk * MOBA_BLOCK:(blk + 1) * MOBA_BLOCK]


def _in_proj(x, gain, w_nat, w_tr, cq, sq, ckt, skt):
    b, s, d = x.shape
    tm = IN_TOKENS
    ns = s // tm
    bpt = tm // MOBA_BLOCK
    cpt = tm // RET_CHUNK
    const = lambda si, bi: (0, 0)
    out_shape = (
        jax.ShapeDtypeStruct((b, s, RET_QK_WIDTH), BF16),
        jax.ShapeDtypeStruct((b, s, RET_WIDTH), BF16),
        jax.ShapeDtypeStruct((b, s, RET_WIDTH), BF16),
        jax.ShapeDtypeStruct((b, s, MOBA_WIDTH), BF16),
        jax.ShapeDtypeStruct((b, ns, bpt, MOBA_WIDTH), F32),
        jax.ShapeDtypeStruct((b, s // RET_CHUNK, RET_QK_WIDTH, RET_CHUNK), BF16),
        jax.ShapeDtypeStruct((b, s // MOBA_BLOCK, MOBA_WIDTH, MOBA_BLOCK), BF16),
        jax.ShapeDtypeStruct((b, s // MOBA_BLOCK, MOBA_WIDTH, MOBA_BLOCK), BF16),
    )
    tok = lambda w: pl.BlockSpec((1, tm, w), lambda si, bi: (bi, si, 0))
    return pl.pallas_call(
        _in_proj_kernel,
        grid=(ns, b),
        in_specs=[
            tok(d),
            pl.BlockSpec((1, d), const),
            pl.BlockSpec((d, NAT_WIDTH), const),
            pl.BlockSpec((TR_WIDTH, d), const),
            pl.BlockSpec((tm, RET_QK_WIDTH), lambda si, bi: (si, 0)),
            pl.BlockSpec((tm, RET_QK_WIDTH), lambda si, bi: (si, 0)),
            pl.BlockSpec((RET_QK_DIM // 2, tm), lambda si, bi: (0, si)),
            pl.BlockSpec((RET_QK_DIM // 2, tm), lambda si, bi: (0, si)),
        ],
        out_specs=(
            tok(RET_QK_WIDTH), tok(RET_WIDTH), tok(RET_WIDTH), tok(MOBA_WIDTH),
            pl.BlockSpec((1, 1, bpt, MOBA_WIDTH), lambda si, bi: (bi, si, 0, 0)),
            pl.BlockSpec((1, cpt, RET_QK_WIDTH, RET_CHUNK), lambda si, bi: (bi, si, 0, 0)),
            pl.BlockSpec((1, bpt, MOBA_WIDTH, MOBA_BLOCK), lambda si, bi: (bi, si, 0, 0)),
            pl.BlockSpec((1, bpt, MOBA_WIDTH, MOBA_BLOCK), lambda si, bi: (bi, si, 0, 0)),
        ),
        out_shape=out_shape,
        compiler_params=pltpu.CompilerParams(
            dimension_semantics=("arbitrary", "arbitrary"), vmem_limit_bytes=VMEM_LIMIT),
        name="in_proj",
    )(x, gain, w_nat, w_tr, cq, sq, ckt, skt)


def _retention_kernel(rq_ref, rkt_ref, rv_ref, sg_ref, dmat_ref, qd_ref, kd_ref, cd_ref,
                      out_ref, state_ref):
    @pl.when(pl.program_id(1) == 0)
    def _():
        state_ref[...] = jnp.zeros_like(state_ref)

    n_chunks = rq_ref.shape[1] // RET_CHUNK

    def chunk(c, carry):
        r0 = pl.multiple_of(c * RET_CHUNK, RET_CHUNK)
        rows = pl.ds(r0, RET_CHUNK)
        for hd in range(RET_HEADS):
            q = rq_ref[0, rows, hd * RET_QK_DIM:(hd + 1) * RET_QK_DIM]
            kt = rkt_ref[0, c, hd * RET_QK_DIM:(hd + 1) * RET_QK_DIM, :]
            v = rv_ref[0, rows, hd * RET_V_DIM:(hd + 1) * RET_V_DIM]
            state = state_ref[hd]
            scores = _dot(q, kt) * dmat_ref[hd]
            o = _dot(scores.astype(BF16), v) + qd_ref[hd] * _dot(q, state.astype(BF16))
            kts = (kt.astype(F32) * kd_ref[hd]).astype(BF16)
            state_ref[hd] = state * cd_ref[hd] + _dot(kts, v)
            ms = jnp.mean(o * o, axis=-1, keepdims=True)
            gate = sg_ref[0, rows, hd * RET_V_DIM:(hd + 1) * RET_V_DIM].astype(F32)
            out_ref[0, rows, hd * RET_V_DIM:(hd + 1) * RET_V_DIM] = (
                o * lax.rsqrt(ms + EPS) * gate).astype(BF16)
        return carry

    lax.fori_loop(0, n_chunks, chunk, 0, unroll=8)


def _retention(rq, rkt, rv, sg, dmat, qd, kd, cd):
    b, s, _ = rq.shape
    tc = min(RET_TOKENS, s)
    tok = lambda w: pl.BlockSpec((1, tc, w), lambda bi, si: (bi, si, 0))
    tab = lambda a: pl.BlockSpec(a.shape, lambda bi, si: (0,) * a.ndim)
    return pl.pallas_call(
        _retention_kernel,
        grid=(b, s // tc),
        in_specs=[
            tok(RET_QK_WIDTH),
            pl.BlockSpec((1, tc // RET_CHUNK, RET_QK_WIDTH, RET_CHUNK), lambda bi, si: (bi, si, 0, 0)),
            tok(RET_WIDTH), tok(RET_WIDTH),
            tab(dmat), tab(qd), tab(kd), tab(cd),
        ],
        out_specs=tok(RET_WIDTH),
        out_shape=jax.ShapeDtypeStruct((b, s, RET_WIDTH), BF16),
        scratch_shapes=[pltpu.VMEM((RET_HEADS, RET_QK_DIM, RET_V_DIM), F32)],
        compiler_params=pltpu.CompilerParams(
            dimension_semantics=("arbitrary", "arbitrary"), vmem_limit_bytes=VMEM_LIMIT),
        name="retention",
    )(rq, rkt, rv, sg, dmat, qd, kd, cd)


ONES_ROWS = 16
PAIR_WIDTH = 2 * MOBA_HEAD_DIM
GATE_BLOCKS = 4


def _moba_gate_kernel(cfar_ref, qt_ref, km_ref, selfar_ref, selprev_ref):
    d = MOBA_HEAD_DIM
    nb = km_ref.shape[2]
    n_q = qt_ref.shape[1]
    shape = (nb, n_q * MOBA_BLOCK)
    blk = lax.broadcasted_iota(jnp.int32, shape, 0)
    i = pl.program_id(1) * n_q + lax.broadcasted_iota(jnp.int32, shape, 1) // MOBA_BLOCK
    blk_f = blk.astype(F32)
    for h in range(MOBA_HEADS):
        qt = jnp.concatenate([qt_ref[0, c, h * d:(h + 1) * d, :] for c in range(n_q)], axis=1)
        km = km_ref[0, h]
        km_hi = km.astype(BF16)
        km_lo = (km - km_hi.astype(F32)).astype(BF16)
        gate = _dot(km_hi, qt) + _dot(km_lo, qt)
        gate = jnp.where(blk < i, gate, NEG)
        chosen = jnp.zeros(gate.shape, F32)
        for _ in range(MOBA_TOPK):
            best = jnp.max(gate, axis=0, keepdims=True)
            idx = jnp.min(jnp.where(gate == best, blk_f, float(nb)), axis=0, keepdims=True)
            hit = blk_f == idx
            chosen = jnp.where(hit & (best > 0.5 * NEG), 1.0, chosen)
            gate = jnp.where(hit, NEG, gate)
        picked = chosen > 0.5
        selfar_ref[0, h] = jnp.where(picked & (blk < i - 1), cfar_ref[h], NEG).astype(BF16)
        prev_hit = jnp.max(jnp.where(picked & (blk == i - 1), 1.0, 0.0), axis=0, keepdims=True)
        selprev_ref[0, h] = jnp.where(prev_hit > 0.5, 0.0, NEG)


def _moba_gate(cfar, mqt, kmean):
    b, nb, _, _ = mqt.shape
    s = nb * MOBA_BLOCK
    n_q = math.gcd(GATE_BLOCKS, nb)
    width = n_q * MOBA_BLOCK
    return pl.pallas_call(
        _moba_gate_kernel,
        grid=(b, nb // n_q),
        in_specs=[
            pl.BlockSpec(memory_space=pltpu.SMEM),
            pl.BlockSpec((1, n_q, MOBA_WIDTH, MOBA_BLOCK), lambda bi, i: (bi, i, 0, 0)),
            pl.BlockSpec((1, MOBA_HEADS, nb, MOBA_HEAD_DIM), lambda bi, i: (bi, 0, 0, 0)),
        ],
        out_specs=(
            pl.BlockSpec((1, MOBA_HEADS, nb, width), lambda bi, i: (bi, 0, 0, i)),
            pl.BlockSpec((1, MOBA_HEADS, 1, width), lambda bi, i: (bi, 0, 0, i)),
        ),
        out_shape=(
            jax.ShapeDtypeStruct((b, MOBA_HEADS, nb, s), BF16),
            jax.ShapeDtypeStruct((b, MOBA_HEADS, 1, s), F32),
        ),
        compiler_params=pltpu.CompilerParams(
            dimension_semantics=("arbitrary", "arbitrary"), vmem_limit_bytes=VMEM_LIMIT),
        name="moba_gate",
    )(cfar, mqt, kmean)


def _moba_kernel(qt_ref, k_ref, vt_ref, selfar_ref, selprev_ref, tnear_ref,
                 out_ref, kaug_ref, vaug_ref, qaug_ref, s_ref, m_ref, acc_ref):
    i = pl.program_id(1)
    d = MOBA_HEAD_DIM
    nb = vaug_ref.shape[1]
    s_len = kaug_ref.shape[1]
    blk_rows = MOBA_BLOCK
    n_pairs = MOBA_HEADS // 2

    @pl.when((pl.program_id(0) == 0) & (i == 0))
    def _():
        row_blk = lax.broadcasted_iota(jnp.int32, (s_len, PAIR_WIDTH), 0) // MOBA_BLOCK
        lane = lax.broadcasted_iota(jnp.int32, (s_len, PAIR_WIDTH), 1)
        block_one_hot = jnp.where(row_blk == lane, 1.0, 0.0).astype(BF16)
        for pe in range(n_pairs):
            kaug_ref[pe, :, PAIR_WIDTH:] = block_one_hot
        for e in range(MOBA_HEADS):
            vaug_ref[e, :, d:, :] = jnp.ones((nb, ONES_ROWS, MOBA_BLOCK), BF16)

    own_rows = pl.ds(pl.multiple_of(i * blk_rows, blk_rows), blk_rows)
    for pe in range(n_pairs):
        kaug_ref[pe, own_rows, 0:PAIR_WIDTH] = k_ref[0, :, pe * PAIR_WIDTH:(pe + 1) * PAIR_WIDTH]
    for e in range(MOBA_HEADS):
        vaug_ref[e, i, 0:d, :] = vt_ref[0, 0, e * d:(e + 1) * d, :]

    prow = lax.broadcasted_iota(jnp.int32, (PAIR_WIDTH, MOBA_BLOCK), 0)
    for e in range(MOBA_HEADS):
        pe, he = divmod(e, 2)
        qt2 = qt_ref[0, 0, pe * PAIR_WIDTH:(pe + 1) * PAIR_WIDTH, :]
        qaug_ref[e, 0:PAIR_WIDTH, :] = jnp.where(
            (prow >= he * d) & (prow < (he + 1) * d), qt2, jnp.zeros_like(qt2))
        qaug_ref[e, PAIR_WIDTH:PAIR_WIDTH + nb, :] = selfar_ref[0, e]
        qaug_ref[e, PAIR_WIDTH + nb:, :] = jnp.zeros((PAIR_WIDTH - nb, MOBA_BLOCK), BF16)

    def qk(e, j, slot, near=False):
        rows = pl.ds(pl.multiple_of(j * blk_rows, blk_rows), blk_rows)
        if near:
            s_ref[slot, e] = _dot(kaug_ref[e // 2, rows, 0:PAIR_WIDTH], qaug_ref[e, 0:PAIR_WIDTH, :])
        else:
            s_ref[slot, e] = _dot(kaug_ref[e // 2, rows, :], qaug_ref[e])

    def sm(e, j, slot, bias=None, first=False):
        st = s_ref[slot, e]
        if bias is not None:
            st = st + bias
        mj = jnp.max(st, axis=0, keepdims=True)
        if first:
            m_ref[e] = mj
            acc_ref[e] = _dot(vaug_ref[e, j], jnp.exp2((st - mj).astype(BF16)))
        else:
            m_old = m_ref[e]
            m_new = jnp.maximum(m_old, mj)
            pv = _dot(vaug_ref[e, j], jnp.exp2((st - m_new).astype(BF16)))
            acc_ref[e] = acc_ref[e] * jnp.exp2(m_old - m_new) + pv
            m_ref[e] = m_new

    clamp = lambda j: jnp.minimum(j, i)
    heads = range(MOBA_HEADS)
    jp = jnp.maximum(i - 1, 0)
    for blk, slot, near in ((i, 0, True), (jp, 1, True), (clamp(0), 2, False), (clamp(1), 3, False)):
        for e in heads:
            qk(e, blk, slot, near=near)
    for e in heads:
        sm(e, i, 0, bias=tnear_ref[e, blk_rows:, :], first=True)
    for e in heads:
        sm(e, jp, 1, bias=tnear_ref[e, 0:blk_rows, :] + selprev_ref[0, e])

    def far(t, carry):
        j = 4 * t
        for half in (0, 2):
            for e in heads:
                qk(e, clamp(j + half + 2), half)
            for e in heads:
                qk(e, clamp(j + half + 3), half + 1)
            for e in heads:
                sm(e, clamp(j + half), (half + 2) % 4)
            for e in heads:
                sm(e, clamp(j + half + 1), (half + 3) % 4)
        return carry

    lax.fori_loop(0, lax.shift_right_logical(i + 2, 2), far, 0)

    outs = []
    for e in range(MOBA_HEADS):
        acc = acc_ref[e]
        outs.append((acc[:d] / acc[d:d + 1]).T)
    out_ref[0] = jnp.concatenate(outs, axis=1).astype(BF16)


def _moba(mqt, mk, mvt, selfar, selprev, tnear):
    b, s, _ = mk.shape
    nb = s // MOBA_BLOCK
    assert nb <= PAIR_WIDTH
    nh = MOBA_HEADS
    return pl.pallas_call(
        _moba_kernel,
        grid=(b, nb),
        in_specs=[
            pl.BlockSpec((1, 1, MOBA_WIDTH, MOBA_BLOCK), lambda bi, i: (bi, i, 0, 0)),
            pl.BlockSpec((1, MOBA_BLOCK, MOBA_WIDTH), lambda bi, i: (bi, i, 0)),
            pl.BlockSpec((1, 1, MOBA_WIDTH, MOBA_BLOCK), lambda bi, i: (bi, i, 0, 0)),
            pl.BlockSpec((1, nh, nb, MOBA_BLOCK), lambda bi, i: (bi, 0, 0, i)),
            pl.BlockSpec((1, nh, 1, MOBA_BLOCK), lambda bi, i: (bi, 0, 0, i)),
            pl.BlockSpec((nh, 2 * MOBA_BLOCK, MOBA_BLOCK), lambda bi, i: (0, 0, 0),
                         pipeline_mode=pl.Buffered(1)),
        ],
        out_specs=pl.BlockSpec((1, MOBA_BLOCK, MOBA_WIDTH), lambda bi, i: (bi, i, 0)),
        out_shape=jax.ShapeDtypeStruct((b, s, MOBA_WIDTH), BF16),
        scratch_shapes=[
            pltpu.VMEM((nh // 2, s, 2 * PAIR_WIDTH), BF16),
            pltpu.VMEM((nh, nb, MOBA_HEAD_DIM + ONES_ROWS, MOBA_BLOCK), BF16),
            pltpu.VMEM((nh, 2 * PAIR_WIDTH, MOBA_BLOCK), BF16),
            pltpu.VMEM((4, nh, MOBA_BLOCK, MOBA_BLOCK), F32),
            pltpu.VMEM((nh, 1, MOBA_BLOCK), F32),
            pltpu.VMEM((nh, MOBA_HEAD_DIM + ONES_ROWS, MOBA_BLOCK), F32),
        ],
        compiler_params=pltpu.CompilerParams(
            dimension_semantics=("arbitrary", "arbitrary"), vmem_limit_bytes=VMEM_LIMIT),
        name="moba",
    )(mqt, mk, mvt, selfar, selprev, tnear)


def _rms(x, g):
    return x * lax.rsqrt(jnp.mean(x * x, axis=-1, keepdims=True) + EPS) * g


def _out_ffn_kernel(x_ref, ret_ref, moba_ref, wo_ref, wg_ref, wu_ref, wd_ref,
                    g_post_mix_ref, g_pre_ffn_ref, g_post_ffn_ref, out_ref):
    tm = x_ref.shape[0]
    groups = [slice(r0, r0 + tm // FFN_ROW_GROUPS) for r0 in range(0, tm, tm // FFN_ROW_GROUPS)]
    mixes = [_dot(jnp.concatenate([ret_ref[rows, :], moba_ref[rows, :]], axis=1), wo_ref[...])
             for rows in groups]
    for rows, mix in zip(groups, mixes):
        x1 = x_ref[rows, :] + _rms(mix, g_post_mix_ref[...])
        h = _rms(x1, g_pre_ffn_ref[...]).astype(BF16)
        f = None
        for lo, hi in FFN_CHUNKS:
            cols = slice(lo, hi)
            gate = _dot(h, wg_ref[:, cols])
            up = _dot(h, wu_ref[:, cols])
            act = (gate * jax.nn.sigmoid(gate) * up).astype(BF16)
            part = _dot(act, wd_ref[cols, :])
            f = part if f is None else f + part
        out_ref[rows, :] = x1 + _rms(f, g_post_ffn_ref[...])


def _out_ffn(x2, ret2, moba2, wo, wg, wu, wd, g_post_mix, g_pre_ffn, g_post_ffn):
    n, d = x2.shape
    tm = min(FFN_TOKENS, n)
    tok = lambda w: pl.BlockSpec((tm, w), lambda t: (t, 0))
    resident = lambda a: pl.BlockSpec(a.shape, lambda t: (0, 0), pipeline_mode=pl.Buffered(1))
    return pl.pallas_call(
        _out_ffn_kernel,
        grid=(n // tm,),
        in_specs=[tok(d), tok(RET_WIDTH), tok(MOBA_WIDTH),
                  resident(wo), resident(wg), resident(wu), resident(wd),
                  resident(g_post_mix), resident(g_pre_ffn), resident(g_post_ffn)],
        out_specs=tok(d),
        out_shape=jax.ShapeDtypeStruct((n, d), F32),
        compiler_params=pltpu.CompilerParams(
            dimension_semantics=("arbitrary",), vmem_limit_bytes=VMEM_LIMIT),
        name="out_ffn",
    )(x2, ret2, moba2, wo, wg, wu, wd, g_post_mix, g_pre_ffn, g_post_ffn)


def _rotary_tables(s):
    half = RET_QK_DIM // 2
    inv_freq = ROPE_BASE ** (-np.arange(half, dtype=np.float64) / half)
    ang = np.arange(s, dtype=np.float64)[:, None] * inv_freq[None, :]
    cos, sin = np.cos(ang), np.sin(ang)
    cq = np.tile(np.concatenate([cos, cos], axis=1), (1, RET_HEADS))
    sq = np.tile(np.concatenate([-sin, sin], axis=1), (1, RET_HEADS))
    return tuple(np.ascontiguousarray(t, dtype=np.float32) for t in (cq, sq, cos.T, sin.T))


def _retention_tables():
    c = RET_CHUNK
    log_gamma = np.log1p(-np.exp(np.linspace(math.log(1.0 / 32), math.log(1.0 / 512), RET_HEADS)))
    idx = np.arange(c, dtype=np.float64)
    diff = idx[:, None] - idx[None, :]
    dmat = np.where(diff >= 0, np.exp(np.maximum(diff, 0.0)[None] * log_gamma[:, None, None]), 0.0)
    q_decay = np.exp((idx + 1.0)[None, :] * log_gamma[:, None])
    k_decay = np.exp((c - 1.0 - idx)[None, :] * log_gamma[:, None])
    chunk_decay = np.exp(c * log_gamma)
    qd = np.broadcast_to(q_decay[:, :, None], (RET_HEADS, c, RET_V_DIM))
    kd = k_decay[:, None, :]
    cd = np.broadcast_to(chunk_decay[:, None, None], (RET_HEADS, 1, RET_V_DIM))
    return tuple(np.ascontiguousarray(t, dtype=np.float32) for t in (dmat, qd, kd, cd))


def _t5_bucket(rel):
    n = np.maximum(rel, 0)
    max_exact = REL_BUCKETS // 2
    large = max_exact + (np.log(np.maximum(n, 1) / max_exact) / math.log(REL_MAX_DIST / max_exact)
                         * (REL_BUCKETS - max_exact)).astype(np.int64)
    return np.where(n < max_exact, n, np.minimum(large, REL_BUCKETS - 1))


def _toeplitz(vec):
    h, two_n = vec.shape
    n = two_n // 2
    ext = jnp.concatenate([vec, jnp.zeros((h, 1), vec.dtype)], axis=1)
    skew = jnp.tile(ext, (1, n))[:, :n * two_n].reshape(h, n, two_n)
    return skew[:, :, n:]


def _bias_tables(rel_bias):
    bias_t = rel_bias.T.astype(F32) * LOG2E
    n = MOBA_BLOCK
    one_hot = (_t5_bucket(np.arange(2 * n))[:, None] == np.arange(REL_BUCKETS)).astype(np.float32)
    by_dist = jnp.einsum("rn,hn->hr", one_hot, bias_t, precision=lax.Precision.HIGHEST)
    tprev = _toeplitz(by_dist)
    own_vec = jnp.concatenate([jnp.full((bias_t.shape[0], n), NEG, F32), by_dist[:, :n]], axis=1)
    town = _toeplitz(own_vec)
    cfar = bias_t[:, REL_BUCKETS - 1]
    return jnp.concatenate([tprev, town], axis=1), cfar


def kernel(x, w_in, w_out, pre_mix_norm, post_mix_norm, pre_ffn_norm, post_ffn_norm,
           rel_bias, w_gate, w_up, w_down):
    b, s, d = x.shape
    assert d == D_MODEL and s % IN_TOKENS == 0 and MOBA_BLOCK + 1 >= REL_MAX_DIST
    depth = w_in.shape[0]

    cq, sq, ckt, skt = _rotary_tables(s)
    dmat, qd, kd, cd = _retention_tables()
    tnear, cfar = _bias_tables(rel_bias)

    sizes = [RET_QK_WIDTH, RET_QK_WIDTH, RET_WIDTH, RET_WIDTH, MOBA_WIDTH, MOBA_WIDTH, MOBA_WIDTH]
    o_rq, o_rk, o_rv, o_rg, o_mq, o_mk, o_mv, _ = np.cumsum([0] + sizes).tolist()

    for layer in range(depth):
        w = w_in[layer]
        w_nat = jnp.concatenate(
            [w[:, o_rq:o_rk], w[:, o_rv:o_mq], w[:, o_mk:o_mv]], axis=1).astype(BF16)
        w_tr = jnp.concatenate([w[:, o_rk:o_rv], w[:, o_mq:o_mk], w[:, o_mv:]], axis=1).T.astype(BF16)

        rq, rv, sg, mk, kmean, rkt, mqt, mvt = _in_proj(
            x, pre_mix_norm[layer][None, :], w_nat, w_tr, cq, sq, ckt, skt)

        ret = _retention(rq, rkt, rv, sg, dmat, qd, kd, cd)

        nb = s // MOBA_BLOCK
        kmean = kmean.reshape(b, nb, MOBA_HEADS, MOBA_HEAD_DIM).transpose(0, 2, 1, 3)
        selfar, selprev = _moba_gate(cfar, mqt, kmean)
        moba = _moba(mqt, mk, mvt, selfar, selprev, tnear)

        x = _out_ffn(
            x.reshape(b * s, d), ret.reshape(b * s, RET_WIDTH), moba.reshape(b * s, MOBA_WIDTH),
            w_out[layer].astype(BF16), w_gate[layer].astype(BF16), w_up[layer].astype(BF16),
            w_down[layer].astype(BF16), post_mix_norm[layer][None, :], pre_ffn_norm[layer][None, :],
            post_ffn_norm[layer][None, :]).reshape(b, s, d)
    return x
```

```python
import functools
import math

import jax
import jax.numpy as jnp
import numpy as np
from jax import lax
from jax.experimental import pallas as pl
from jax.experimental.pallas import tpu as pltpu

F32 = jnp.float32
BF16 = jnp.bfloat16

D_MODEL = 1024
RET_HEADS = 4
RET_QK_DIM = 64
RET_V_DIM = 128
RET_CHUNK = 128
RET_QK_WIDTH = RET_HEADS * RET_QK_DIM
RET_WIDTH = RET_HEADS * RET_V_DIM
MOBA_HEADS = 8
MOBA_HEAD_DIM = 64
MOBA_WIDTH = MOBA_HEADS * MOBA_HEAD_DIM
MOBA_BLOCK = 256
MOBA_TOPK = 3
REL_BUCKETS = 32
REL_MAX_DIST = 128
D_FF = 2816
EPS = 1e-6
ROPE_BASE = 10000.0

NEG = -1e30
LOG2E = math.log2(math.e)

IN_TOKENS = 1024
RET_TOKENS = 1024
FFN_TOKENS = 1024
FFN_ROW_GROUPS = 4
MXU_TILE = 256
FFN_CHUNKS = ((0, 6 * MXU_TILE), (6 * MXU_TILE, D_FF))
VMEM_LIMIT = 56 * 1024 * 1024

NAT_WIDTH = RET_QK_WIDTH + 2 * RET_WIDTH + MOBA_WIDTH
TR_WIDTH = RET_QK_WIDTH + 2 * MOBA_WIDTH


def _nt_dot(a, b):
    return lax.dot_general(a, b, (((1,), (1,)), ((), ())), preferred_element_type=F32)


def _dot(a, b):
    return jnp.dot(a, b, preferred_element_type=F32)


def _in_proj_kernel(x_ref, g_ref, wn_ref, wt_ref, cq_ref, sq_ref, ckt_ref, skt_ref,
                    rq_ref, rv_ref, sg_ref, mk_ref, kmean_ref, rkt_ref, mqt_ref, mvt_ref):
    tm = x_ref.shape[1]
    x = x_ref[0]
    ms = jnp.mean(x * x, axis=-1, keepdims=True)
    h = (x * lax.rsqrt(ms + EPS) * g_ref[...]).astype(BF16)

    pn = _dot(h, wn_ref[...])
    pt = _nt_dot(wt_ref[...], h)

    rq = pn[:, :RET_QK_WIDTH]
    lane = lax.broadcasted_iota(jnp.int32, rq.shape, 1)
    first_half = (lane % RET_QK_DIM) < (RET_QK_DIM // 2)
    partner = jnp.where(first_half,
                        pltpu.roll(rq, RET_QK_WIDTH - RET_QK_DIM // 2, 1),
                        pltpu.roll(rq, RET_QK_DIM // 2, 1))
    rq_ref[0] = (rq * cq_ref[...] + partner * sq_ref[...]).astype(BF16)

    rv_ref[0] = pn[:, RET_QK_WIDTH:RET_QK_WIDTH + RET_WIDTH].astype(BF16)
    rg = pn[:, RET_QK_WIDTH + RET_WIDTH:RET_QK_WIDTH + 2 * RET_WIDTH]
    sg_ref[0] = (rg * jax.nn.sigmoid(rg)).astype(BF16)

    mk = pn[:, RET_QK_WIDTH + 2 * RET_WIDTH:]
    mk_ref[0] = mk.astype(BF16)
    for blk in range(tm // MOBA_BLOCK):
        kmean_ref[0, 0, blk:blk + 1, :] = jnp.mean(
            mk[blk * MOBA_BLOCK:(blk + 1) * MOBA_BLOCK], axis=0, keepdims=True)

    half = RET_QK_DIM // 2
    cos_t = ckt_ref[...]
    sin_t = skt_ref[...]
    parts = []
    for hd in range(RET_HEADS):
        x1 = pt[hd * RET_QK_DIM:hd * RET_QK_DIM + half]
        x2 = pt[hd * RET_QK_DIM + half:(hd + 1) * RET_QK_DIM]
        parts.append(x1 * cos_t - x2 * sin_t)
        parts.append(x2 * cos_t + x1 * sin_t)
    rkt = (jnp.concatenate(parts, axis=0) * (RET_QK_DIM ** -0.5)).astype(BF16)
    for c in range(tm // RET_CHUNK):
        rkt_ref[0, c] = rkt[:, c * RET_CHUNK:(c + 1) * RET_CHUNK]

    mqt = (pt[RET_QK_WIDTH:RET_QK_WIDTH + MOBA_WIDTH] * (MOBA_HEAD_DIM ** -0.5 * LOG2E)).astype(BF16)
    mvt = pt[RET_QK_WIDTH + MOBA_WIDTH:].astype(BF16)
    for blk in range(tm // MOBA_BLOCK):
        mqt_ref[0, blk] = mqt[:, blk * MOBA_BLOCK:(blk + 1) * MOBA_BLOCK]
        mvt_ref[0, blk] = mvt[:, blk * MOBA_BLOCK:(blk + 1) * MOBA_BLOCK]


def _in_proj(x, gain, w_nat, w_tr, cq, sq, ckt, skt):
    b, s, d = x.shape
    tm = IN_TOKENS
    ns = s // tm
    bpt = tm // MOBA_BLOCK
    cpt = tm // RET_CHUNK
    const = lambda si, bi: (0, 0)
    out_shape = (
        jax.ShapeDtypeStruct((b, s, RET_QK_WIDTH), BF16),
        jax.ShapeDtypeStruct((b, s, RET_WIDTH), BF16),
        jax.ShapeDtypeStruct((b, s, RET_WIDTH), BF16),
        jax.ShapeDtypeStruct((b, s, MOBA_WIDTH), BF16),
        jax.ShapeDtypeStruct((b, ns, bpt, MOBA_WIDTH), F32),
        jax.ShapeDtypeStruct((b, s // RET_CHUNK, RET_QK_WIDTH, RET_CHUNK), BF16),
        jax.ShapeDtypeStruct((b, s // MOBA_BLOCK, MOBA_WIDTH, MOBA_BLOCK), BF16),
        jax.ShapeDtypeStruct((b, s // MOBA_BLOCK, MOBA_WIDTH, MOBA_BLOCK), BF16),
    )
    tok = lambda w: pl.BlockSpec((1, tm, w), lambda si, bi: (bi, si, 0))
    return pl.pallas_call(
        _in_proj_kernel,
        grid=(ns, b),
        in_specs=[
            tok(d),
            pl.BlockSpec((1, d), const),
            pl.BlockSpec((d, NAT_WIDTH), const),
            pl.BlockSpec((TR_WIDTH, d), const),
            pl.BlockSpec((tm, RET_QK_WIDTH), lambda si, bi: (si, 0)),
            pl.BlockSpec((tm, RET_QK_WIDTH), lambda si, bi: (si, 0)),
            pl.BlockSpec((RET_QK_DIM // 2, tm), lambda si, bi: (0, si)),
            pl.BlockSpec((RET_QK_DIM // 2, tm), lambda si, bi: (0, si)),
        ],
        out_specs=(
            tok(RET_QK_WIDTH), tok(RET_WIDTH), tok(RET_WIDTH), tok(MOBA_WIDTH),
            pl.BlockSpec((1, 1, bpt, MOBA_WIDTH), lambda si, bi: (bi, si, 0, 0)),
            pl.BlockSpec((1, cpt, RET_QK_WIDTH, RET_CHUNK), lambda si, bi: (bi, si, 0, 0)),
            pl.BlockSpec((1, bpt, MOBA_WIDTH, MOBA_BLOCK), lambda si, bi: (bi, si, 0, 0)),
            pl.BlockSpec((1, bpt, MOBA_WIDTH, MOBA_BLOCK), lambda si, bi: (bi, si, 0, 0)),
        ),
        out_shape=out_shape,
        compiler_params=pltpu.CompilerParams(
            dimension_semantics=("arbitrary", "arbitrary"), vmem_limit_bytes=VMEM_LIMIT),
        name="in_proj",
    )(x, gain, w_nat, w_tr, cq, sq, ckt, skt)


def _retention_kernel(rq_ref, rkt_ref, rv_ref, sg_ref, dmat_ref, qd_ref, kd_ref, cd_ref,
                      out_ref, state_ref):
    @pl.when(pl.program_id(1) == 0)
    def _():
        state_ref[...] = jnp.zeros_like(state_ref)

    n_chunks = rq_ref.shape[1] // RET_CHUNK

    def chunk(c, carry):
        r0 = pl.multiple_of(c * RET_CHUNK, RET_CHUNK)
        rows = pl.ds(r0, RET_CHUNK)
        for hd in range(RET_HEADS):
            q = rq_ref[0, rows, hd * RET_QK_DIM:(hd + 1) * RET_QK_DIM]
            kt = rkt_ref[0, c, hd * RET_QK_DIM:(hd + 1) * RET_QK_DIM, :]
            v = rv_ref[0, rows, hd * RET_V_DIM:(hd + 1) * RET_V_DIM]
            state = state_ref[hd]
            scores = _dot(q, kt) * dmat_ref[hd]
            o = _dot(scores.astype(BF16), v) + qd_ref[hd] * _dot(q, state.astype(BF16))
            kts = (kt.astype(F32) * kd_ref[hd]).astype(BF16)
            state_ref[hd] = state * cd_ref[hd] + _dot(kts, v)
            ms = jnp.mean(o * o, axis=-1, keepdims=True)
            gate = sg_ref[0, rows, hd * RET_V_DIM:(hd + 1) * RET_V_DIM].astype(F32)
            out_ref[0, rows, hd * RET_V_DIM:(hd + 1) * RET_V_DIM] = (
                o * lax.rsqrt(ms + EPS) * gate).astype(BF16)
        return carry

    lax.fori_loop(0, n_chunks, chunk, 0, unroll=8)


def _retention(rq, rkt, rv, sg, dmat, qd, kd, cd):
    b, s, _ = rq.shape
    tc = min(RET_TOKENS, s)
    tok = lambda w: pl.BlockSpec((1, tc, w), lambda bi, si: (bi, si, 0))
    tab = lambda a: pl.BlockSpec(a.shape, lambda bi, si: (0,) * a.ndim)
    return pl.pallas_call(
        _retention_kernel,
        grid=(b, s // tc),
        in_specs=[
            tok(RET_QK_WIDTH),
            pl.BlockSpec((1, tc // RET_CHUNK, RET_QK_WIDTH, RET_CHUNK), lambda bi, si: (bi, si, 0, 0)),
            tok(RET_WIDTH), tok(RET_WIDTH),
            tab(dmat), tab(qd), tab(kd), tab(cd),
        ],
        out_specs=tok(RET_WIDTH),
        out_shape=jax.ShapeDtypeStruct((b, s, RET_WIDTH), BF16),
        scratch_shapes=[pltpu.VMEM((RET_HEADS, RET_QK_DIM, RET_V_DIM), F32)],
        compiler_params=pltpu.CompilerParams(
            dimension_semantics=("arbitrary", "arbitrary"), vmem_limit_bytes=VMEM_LIMIT),
        name="retention",
    )(rq, rkt, rv, sg, dmat, qd, kd, cd)


ONES_ROWS = 16
PAIR_WIDTH = 2 * MOBA_HEAD_DIM
GATE_BLOCKS = 4


def _moba_gate_kernel(cfar_ref, qt_ref, km_ref, selfar_ref, selprev_ref):
    d = MOBA_HEAD_DIM
    nb = km_ref.shape[2]
    n_q = qt_ref.shape[1]
    shape = (nb, n_q * MOBA_BLOCK)
    blk = lax.broadcasted_iota(jnp.int32, shape, 0)
    i = pl.program_id(1) * n_q + lax.broadcasted_iota(jnp.int32, shape, 1) // MOBA_BLOCK
    blk_f = blk.astype(F32)
    for h in range(MOBA_HEADS):
        qt = jnp.concatenate([qt_ref[0, c, h * d:(h + 1) * d, :] for c in range(n_q)], axis=1)
        km = km_ref[0, h]
        km_hi = km.astype(BF16)
        km_lo = (km - km_hi.astype(F32)).astype(BF16)
        gate = _dot(km_hi, qt) + _dot(km_lo, qt)
        gate = jnp.where(blk < i, gate, NEG)
        chosen = jnp.zeros(gate.shape, F32)
        for _ in range(MOBA_TOPK):
            best = jnp.max(gate, axis=0, keepdims=True)
            idx = jnp.min(jnp.where(gate == best, blk_f, float(nb)), axis=0, keepdims=True)
            hit = blk_f == idx
            chosen = jnp.where(hit & (best > 0.5 * NEG), 1.0, chosen)
            gate = jnp.where(hit, NEG, gate)
        picked = chosen > 0.5
        selfar_ref[0, h] = jnp.where(picked & (blk < i - 1), cfar_ref[h], NEG).astype(BF16)
        prev_hit = jnp.max(jnp.where(picked & (blk == i - 1), 1.0, 0.0), axis=0, keepdims=True)
        selprev_ref[0, h] = jnp.where(prev_hit > 0.5, 0.0, NEG)


def _moba_gate(cfar, mqt, kmean):
    b, nb, _, _ = mqt.shape
    s = nb * MOBA_BLOCK
    n_q = math.gcd(GATE_BLOCKS, nb)
    width = n_q * MOBA_BLOCK
    return pl.pallas_call(
        _moba_gate_kernel,
        grid=(b, nb // n_q),
        in_specs=[
            pl.BlockSpec(memory_space=pltpu.SMEM),
            pl.BlockSpec((1, n_q, MOBA_WIDTH, MOBA_BLOCK), lambda bi, i: (bi, i, 0, 0)),
            pl.BlockSpec((1, MOBA_HEADS, nb, MOBA_HEAD_DIM), lambda bi, i: (bi, 0, 0, 0)),
        ],
        out_specs=(
            pl.BlockSpec((1, MOBA_HEADS, nb, width), lambda bi, i: (bi, 0, 0, i)),
            pl.BlockSpec((1, MOBA_HEADS, 1, width), lambda bi, i: (bi, 0, 0, i)),
        ),
        out_shape=(
            jax.ShapeDtypeStruct((b, MOBA_HEADS, nb, s), BF16),
            jax.ShapeDtypeStruct((b, MOBA_HEADS, 1, s), F32),
        ),
        compiler_params=pltpu.CompilerParams(
            dimension_semantics=("arbitrary", "arbitrary"), vmem_limit_bytes=VMEM_LIMIT),
        name="moba_gate",
    )(cfar, mqt, kmean)


def _moba_kernel(qt_ref, k_ref, vt_ref, selfar_ref, selprev_ref, tnear_ref,
                 out_ref, kaug_ref, vaug_ref, qaug_ref, qnear_ref, s_ref, m_ref, acc_ref):
    i = pl.program_id(1)
    d = MOBA_HEAD_DIM
    nb = vaug_ref.shape[1]
    s_len = kaug_ref.shape[1]
    blk_rows = MOBA_BLOCK
    n_pairs = MOBA_HEADS // 2

    @pl.when((pl.program_id(0) == 0) & (i == 0))
    def _():
        row_blk = lax.broadcasted_iota(jnp.int32, (s_len, 2 * PAIR_WIDTH), 0) // MOBA_BLOCK
        lane = lax.broadcasted_iota(jnp.int32, (s_len, 2 * PAIR_WIDTH), 1)
        hot_lane = jnp.where(lane < PAIR_WIDTH, lane - d, lane - PAIR_WIDTH)
        in_band = (lane >= d) & (lane < PAIR_WIDTH + d)
        pattern = jnp.where(in_band & (row_blk == hot_lane), 1.0, 0.0).astype(BF16)
        for pe in range(n_pairs):
            kaug_ref[pe] = pattern
        for e in range(MOBA_HEADS):
            vaug_ref[e, :, d:, :] = jnp.ones((nb, ONES_ROWS, MOBA_BLOCK), BF16)

    own_rows = pl.ds(pl.multiple_of(i * blk_rows, blk_rows), blk_rows)
    for pe in range(n_pairs):
        kaug_ref[pe, own_rows, 0:d] = k_ref[0, :, pe * PAIR_WIDTH:pe * PAIR_WIDTH + d]
        kaug_ref[pe, own_rows, PAIR_WIDTH + d:] = k_ref[0, :, pe * PAIR_WIDTH + d:(pe + 1) * PAIR_WIDTH]
    for e in range(MOBA_HEADS):
        vaug_ref[e, i, 0:d, :] = vt_ref[0, 0, e * d:(e + 1) * d, :]

    prow = lax.broadcasted_iota(jnp.int32, (PAIR_WIDTH, MOBA_BLOCK), 0)
    for e in range(MOBA_HEADS):
        pe, he = divmod(e, 2)
        qt2 = qt_ref[0, 0, pe * PAIR_WIDTH:(pe + 1) * PAIR_WIDTH, :]
        q_only = jnp.where((prow >= he * d) & (prow < (he + 1) * d), qt2, jnp.zeros_like(qt2))
        qnear_ref[e] = q_only
        qaug_ref[e] = q_only
        hot_row = d if he == 0 else 0
        qaug_ref[e, hot_row:hot_row + nb, :] = selfar_ref[0, e]

    def qk(e, j, slot, near=False):
        rows = pl.ds(pl.multiple_of(j * blk_rows, blk_rows), blk_rows)
        keys = kaug_ref[e // 2, rows, (e % 2) * PAIR_WIDTH:(e % 2 + 1) * PAIR_WIDTH]
        s_ref[slot, e] = _dot(keys, qnear_ref[e] if near else qaug_ref[e])

    def sm(e, j, slot, bias=None, first=False):
        st = s_ref[slot, e]
        if bias is not None:
            st = st + bias
        mj = jnp.max(st, axis=0, keepdims=True)
        if first:
            m_ref[e] = mj
            acc_ref[e] = _dot(vaug_ref[e, j], jnp.exp2(st - mj).astype(BF16))
        else:
            m_old = m_ref[e]
            m_new = jnp.maximum(m_old, mj)
            pv = _dot(vaug_ref[e, j], jnp.exp2(st - m_new).astype(BF16))
            acc_ref[e] = acc_ref[e] * jnp.exp2(m_old - m_new) + pv
            m_ref[e] = m_new

    clamp = lambda j: jnp.minimum(j, i)
    heads = range(MOBA_HEADS)
    jp = jnp.maximum(i - 1, 0)
    for blk, slot, near in ((i, 0, True), (jp, 1, True), (clamp(0), 2, False), (clamp(1), 3, False)):
        for e in heads:
            qk(e, blk, slot, near=near)
    for e in heads:
        sm(e, i, 0, bias=tnear_ref[e, blk_rows:, :], first=True)
    for e in heads:
        sm(e, jp, 1, bias=tnear_ref[e, 0:blk_rows, :] + selprev_ref[0, e])

    def far(t, carry):
        j = 4 * t
        for half in (0, 2):
            for e in heads:
                qk(e, clamp(j + half + 2), half)
            for e in heads:
                qk(e, clamp(j + half + 3), half + 1)
            for e in heads:
                sm(e, clamp(j + half), (half + 2) % 4)
            for e in heads:
                sm(e, clamp(j + half + 1), (half + 3) % 4)
        return carry

    lax.fori_loop(0, lax.shift_right_logical(i + 2, 2), far, 0)

    outs = []
    for e in range(MOBA_HEADS):
        acc = acc_ref[e]
        outs.append((acc[:d] / acc[d:d + 1]).T)
    out_ref[0] = jnp.concatenate(outs, axis=1).astype(BF16)


def _moba(mqt, mk, mvt, selfar, selprev, tnear):
    b, s, _ = mk.shape
    nb = s // MOBA_BLOCK
    assert nb <= MOBA_HEAD_DIM
    nh = MOBA_HEADS
    return pl.pallas_call(
        _moba_kernel,
        grid=(b, nb),
        in_specs=[
            pl.BlockSpec((1, 1, MOBA_WIDTH, MOBA_BLOCK), lambda bi, i: (bi, i, 0, 0)),
            pl.BlockSpec((1, MOBA_BLOCK, MOBA_WIDTH), lambda bi, i: (bi, i, 0)),
            pl.BlockSpec((1, 1, MOBA_WIDTH, MOBA_BLOCK), lambda bi, i: (bi, i, 0, 0)),
            pl.BlockSpec((1, nh, nb, MOBA_BLOCK), lambda bi, i: (bi, 0, 0, i)),
            pl.BlockSpec((1, nh, 1, MOBA_BLOCK), lambda bi, i: (bi, 0, 0, i)),
            pl.BlockSpec((nh, 2 * MOBA_BLOCK, MOBA_BLOCK), lambda bi, i: (0, 0, 0),
                         pipeline_mode=pl.Buffered(1)),
        ],
        out_specs=pl.BlockSpec((1, MOBA_BLOCK, MOBA_WIDTH), lambda bi, i: (bi, i, 0)),
        out_shape=jax.ShapeDtypeStruct((b, s, MOBA_WIDTH), BF16),
        scratch_shapes=[
            pltpu.VMEM((nh // 2, s, 2 * PAIR_WIDTH), BF16),
            pltpu.VMEM((nh, nb, MOBA_HEAD_DIM + ONES_ROWS, MOBA_BLOCK), BF16),
            pltpu.VMEM((nh, PAIR_WIDTH, MOBA_BLOCK), BF16),
            pltpu.VMEM((nh, PAIR_WIDTH, MOBA_BLOCK), BF16),
            pltpu.VMEM((4, nh, MOBA_BLOCK, MOBA_BLOCK), F32),
            pltpu.VMEM((nh, 1, MOBA_BLOCK), F32),
            pltpu.VMEM((nh, MOBA_HEAD_DIM + ONES_ROWS, MOBA_BLOCK), F32),
        ],
        compiler_params=pltpu.CompilerParams(
            dimension_semantics=("arbitrary", "arbitrary"), vmem_limit_bytes=VMEM_LIMIT),
        name="moba",
    )(mqt, mk, mvt, selfar, selprev, tnear)


def _rms(x, g):
    return x * lax.rsqrt(jnp.mean(x * x, axis=-1, keepdims=True) + EPS) * g


def _out_ffn_kernel(x_ref, ret_ref, moba_ref, wo_ref, wg_ref, wu_ref, wd_ref,
                    g_post_mix_ref, g_pre_ffn_ref, g_post_ffn_ref, out_ref):
    tm = x_ref.shape[0]
    groups = [slice(r0, r0 + tm // FFN_ROW_GROUPS) for r0 in range(0, tm, tm // FFN_ROW_GROUPS)]
    mixes = [_dot(jnp.concatenate([ret_ref[rows, :], moba_ref[rows, :]], axis=1), wo_ref[...])
             for rows in groups]
    for rows, mix in zip(groups, mixes):
        x1 = x_ref[rows, :] + _rms(mix, g_post_mix_ref[...])
        h = _rms(x1, g_pre_ffn_ref[...]).astype(BF16)
        f = None
        for lo, hi in FFN_CHUNKS:
            cols = slice(lo, hi)
            gate = _dot(h, wg_ref[:, cols])
            up = _dot(h, wu_ref[:, cols])
            act = (gate * jax.nn.sigmoid(gate) * up).astype(BF16)
            part = _dot(act, wd_ref[cols, :])
            f = part if f is None else f + part
        out_ref[rows, :] = x1 + _rms(f, g_post_ffn_ref[...])


def _out_ffn(x2, ret2, moba2, wo, wg, wu, wd, g_post_mix, g_pre_ffn, g_post_ffn):
    n, d = x2.shape
    tm = min(FFN_TOKENS, n)
    tok = lambda w: pl.BlockSpec((tm, w), lambda t: (t, 0))
    resident = lambda a: pl.BlockSpec(a.shape, lambda t: (0, 0), pipeline_mode=pl.Buffered(1))
    return pl.pallas_call(
        _out_ffn_kernel,
        grid=(n // tm,),
        in_specs=[tok(d), tok(RET_WIDTH), tok(MOBA_WIDTH),
                  resident(wo), resident(wg), resident(wu), resident(wd),
                  resident(g_post_mix), resident(g_pre_ffn), resident(g_post_ffn)],
        out_specs=tok(d),
        out_shape=jax.ShapeDtypeStruct((n, d), F32),
        compiler_params=pltpu.CompilerParams(
            dimension_semantics=("arbitrary",), vmem_limit_bytes=VMEM_LIMIT),
        name="out_ffn",
    )(x2, ret2, moba2, wo, wg, wu, wd, g_post_mix, g_pre_ffn, g_post_ffn)


def _rotary_tables(s):
    half = RET_QK_DIM // 2
    inv_freq = ROPE_BASE ** (-np.arange(half, dtype=np.float64) / half)
    ang = np.arange(s, dtype=np.float64)[:, None] * inv_freq[None, :]
    cos, sin = np.cos(ang), np.sin(ang)
    cq = np.tile(np.concatenate([cos, cos], axis=1), (1, RET_HEADS))
    sq = np.tile(np.concatenate([-sin, sin], axis=1), (1, RET_HEADS))
    return tuple(np.ascontiguousarray(t, dtype=np.float32) for t in (cq, sq, cos.T, sin.T))


def _retention_tables():
    c = RET_CHUNK
    log_gamma = np.log1p(-np.exp(np.linspace(math.log(1.0 / 32), math.log(1.0 / 512), RET_HEADS)))
    idx = np.arange(c, dtype=np.float64)
    diff = idx[:, None] - idx[None, :]
    dmat = np.where(diff >= 0, np.exp(np.maximum(diff, 0.0)[None] * log_gamma[:, None, None]), 0.0)
    q_decay = np.exp((idx + 1.0)[None, :] * log_gamma[:, None])
    k_decay = np.exp((c - 1.0 - idx)[None, :] * log_gamma[:, None])
    chunk_decay = np.exp(c * log_gamma)
    qd = np.broadcast_to(q_decay[:, :, None], (RET_HEADS, c, RET_V_DIM))
    kd = k_decay[:, None, :]
    cd = np.broadcast_to(chunk_decay[:, None, None], (RET_HEADS, 1, RET_V_DIM))
    return tuple(np.ascontiguousarray(t, dtype=np.float32) for t in (dmat, qd, kd, cd))


def _t5_bucket(rel):
    n = np.maximum(rel, 0)
    max_exact = REL_BUCKETS // 2
    large = max_exact + (np.log(np.maximum(n, 1) / max_exact) / math.log(REL_MAX_DIST / max_exact)
                         * (REL_BUCKETS - max_exact)).astype(np.int64)
    return np.where(n < max_exact, n, np.minimum(large, REL_BUCKETS - 1))


def _toeplitz(vec):
    h, two_n = vec.shape
    n = two_n // 2
    ext = jnp.concatenate([vec, jnp.zeros((h, 1), vec.dtype)], axis=1)
    skew = jnp.tile(ext, (1, n))[:, :n * two_n].reshape(h, n, two_n)
    return skew[:, :, n:]


def _bias_tables(rel_bias):
    bias_t = rel_bias.T.astype(F32) * LOG2E
    n = MOBA_BLOCK
    one_hot = (_t5_bucket(np.arange(2 * n))[:, None] == np.arange(REL_BUCKETS)).astype(np.float32)
    by_dist = jnp.einsum("rn,hn->hr", one_hot, bias_t, precision=lax.Precision.HIGHEST)
    tprev = _toeplitz(by_dist)
    own_vec = jnp.concatenate([jnp.full((bias_t.shape[0], n), NEG, F32), by_dist[:, :n]], axis=1)
    town = _toeplitz(own_vec)
    cfar = bias_t[:, REL_BUCKETS - 1]
    return jnp.concatenate([tprev, town], axis=1), cfar


def kernel(x, w_in, w_out, pre_mix_norm, post_mix_norm, pre_ffn_norm, post_ffn_norm,
           rel_bias, w_gate, w_up, w_down):
    b, s, d = x.shape
    assert d == D_MODEL and s % IN_TOKENS == 0 and MOBA_BLOCK + 1 >= REL_MAX_DIST
    depth = w_in.shape[0]

    cq, sq, ckt, skt = _rotary_tables(s)
    dmat, qd, kd, cd = _retention_tables()
    tnear, cfar = _bias_tables(rel_bias)

    sizes = [RET_QK_WIDTH, RET_QK_WIDTH, RET_WIDTH, RET_WIDTH, MOBA_WIDTH, MOBA_WIDTH, MOBA_WIDTH]
    o_rq, o_rk, o_rv, o_rg, o_mq, o_mk, o_mv, _ = np.cumsum([0] + sizes).tolist()

    for layer in range(depth):
        w = w_in[layer]
        w_nat = jnp.concatenate(
            [w[:, o_rq:o_rk], w[:, o_rv:o_mq], w[:, o_mk:o_mv]], axis=1).astype(BF16)
        w_tr = jnp.concatenate([w[:, o_rk:o_rv], w[:, o_mq:o_mk], w[:, o_mv:]], axis=1).T.astype(BF16)

        rq, rv, sg, mk, kmean, rkt, mqt, mvt = _in_proj(
            x, pre_mix_norm[layer][None, :], w_nat, w_tr, cq, sq, ckt, skt)

        ret = _retention(rq, rkt, rv, sg, dmat, qd, kd, cd)

        nb = s // MOBA_BLOCK
        kmean = kmean.reshape(b, nb, MOBA_HEADS, MOBA_HEAD_DIM).transpose(0, 2, 1, 3)
        selfar, selprev = _moba_gate(cfar, mqt, kmean)
        moba = _moba(mqt, mk, mvt, selfar, selprev, tnear)

        x = _out_ffn(
            x.reshape(b * s, d), ret.reshape(b * s, RET_WIDTH), moba.reshape(b * s, MOBA_WIDTH),
            w_out[layer].astype(BF16), w_gate[layer].astype(BF16), w_up[layer].astype(BF16),
            w_down[layer].astype(BF16), post_mix_norm[layer][None, :], pre_ffn_norm[layer][None, :],
            post_ffn_norm[layer][None, :]).reshape(b, s, d)
    return x
```

```python
import functools
import math

import jax
import jax.numpy as jnp
import numpy as np
from jax import lax
from jax.experimental import pallas as pl
from jax.experimental.pallas import tpu as pltpu

F32 = jnp.float32
BF16 = jnp.bfloat16

D_MODEL = 1024
RET_HEADS = 4
RET_QK_DIM = 64
RET_V_DIM = 128
RET_CHUNK = 128
RET_QK_WIDTH = RET_HEADS * RET_QK_DIM
RET_WIDTH = RET_HEADS * RET_V_DIM
MOBA_HEADS = 8
MOBA_HEAD_DIM = 64
MOBA_WIDTH = MOBA_HEADS * MOBA_HEAD_DIM
MOBA_BLOCK = 256
MOBA_TOPK = 3
REL_BUCKETS = 32
REL_MAX_DIST = 128
D_FF = 2816
EPS = 1e-6
ROPE_BASE = 10000.0

NEG = -1e30
LOG2E = math.log2(math.e)

IN_TOKENS = 1024
RET_TOKENS = 1024
FFN_TOKENS = 1024
FFN_ROW_GROUPS = 4
MXU_TILE = 256
FFN_CHUNKS = ((0, 6 * MXU_TILE), (6 * MXU_TILE, D_FF))
VMEM_LIMIT = 56 * 1024 * 1024

NAT_WIDTH = RET_QK_WIDTH + 2 * RET_WIDTH + MOBA_WIDTH
TR_WIDTH = RET_QK_WIDTH + 2 * MOBA_WIDTH


def _nt_dot(a, b):
    return lax.dot_general(a, b, (((1,), (1,)), ((), ())), preferred_element_type=F32)


def _dot(a, b):
    return jnp.dot(a, b, preferred_element_type=F32)


def _in_proj_kernel(x_ref, g_ref, wn_ref, wt_ref, cq_ref, sq_ref, ckt_ref, skt_ref,
                    rq_ref, rv_ref, sg_ref, mk_ref, kmean_ref, rkt_ref, mqt_ref, mvt_ref):
    tm = x_ref.shape[1]
    x = x_ref[0]
    ms = jnp.mean(x * x, axis=-1, keepdims=True)
    h = (x * lax.rsqrt(ms + EPS) * g_ref[...]).astype(BF16)

    pn = _dot(h, wn_ref[...])
    pt = _nt_dot(wt_ref[...], h)

    rq = pn[:, :RET_QK_WIDTH]
    lane = lax.broadcasted_iota(jnp.int32, rq.shape, 1)
    first_half = (lane % RET_QK_DIM) < (RET_QK_DIM // 2)
    partner = jnp.where(first_half,
                        pltpu.roll(rq, RET_QK_WIDTH - RET_QK_DIM // 2, 1),
                        pltpu.roll(rq, RET_QK_DIM // 2, 1))
    rq_ref[0] = (rq * cq_ref[...] + partner * sq_ref[...]).astype(BF16)

    rv_ref[0] = pn[:, RET_QK_WIDTH:RET_QK_WIDTH + RET_WIDTH].astype(BF16)
    rg = pn[:, RET_QK_WIDTH + RET_WIDTH:RET_QK_WIDTH + 2 * RET_WIDTH]
    sg_ref[0] = (rg * jax.nn.sigmoid(rg)).astype(BF16)

    mk = pn[:, RET_QK_WIDTH + 2 * RET_WIDTH:]
    mk_ref[0] = mk.astype(BF16)
    for blk in range(tm // MOBA_BLOCK):
        kmean_ref[0, 0, blk:blk + 1, :] = jnp.mean(
            mk[blk * MOBA_BLOCK:(blk + 1) * MOBA_BLOCK], axis=0, keepdims=True)

    half = RET_QK_DIM // 2
    cos_t = ckt_ref[...]
    sin_t = skt_ref[...]
    parts = []
    for hd in range(RET_HEADS):
        x1 = pt[hd * RET_QK_DIM:hd * RET_QK_DIM + half]
        x2 = pt[hd * RET_QK_DIM + half:(hd + 1) * RET_QK_DIM]
        parts.append(x1 * cos_t - x2 * sin_t)
        parts.append(x2 * cos_t + x1 * sin_t)
    rkt = (jnp.concatenate(parts, axis=0) * (RET_QK_DIM ** -0.5)).astype(BF16)
    for c in range(tm // RET_CHUNK):
        rkt_ref[0, c] = rkt[:, c * RET_CHUNK:(c + 1) * RET_CHUNK]

    mqt = (pt[RET_QK_WIDTH:RET_QK_WIDTH + MOBA_WIDTH] * (MOBA_HEAD_DIM ** -0.5 * LOG2E)).astype(BF16)
    mvt = pt[RET_QK_WIDTH + MOBA_WIDTH:].astype(BF16)
    for blk in range(tm // MOBA_BLOCK):
        mqt_ref[0, blk] = mqt[:, blk * MOBA_BLOCK:(blk + 1) * MOBA_BLOCK]
        mvt_ref[0, blk] = mvt[:, blk * MOBA_BLOCK:(blk + 1) * MOBA_BLOCK]


def _in_proj(x, gain, w_nat, w_tr, cq, sq, ckt, skt):
    b, s, d = x.shape
    tm = IN_TOKENS
    ns = s // tm
    bpt = tm // MOBA_BLOCK
    cpt = tm // RET_CHUNK
    const = lambda si, bi: (0, 0)
    out_shape = (
        jax.ShapeDtypeStruct((b, s, RET_QK_WIDTH), BF16),
        jax.ShapeDtypeStruct((b, s, RET_WIDTH), BF16),
        jax.ShapeDtypeStruct((b, s, RET_WIDTH), BF16),
        jax.ShapeDtypeStruct((b, s, MOBA_WIDTH), BF16),
        jax.ShapeDtypeStruct((b, ns, bpt, MOBA_WIDTH), F32),
        jax.ShapeDtypeStruct((b, s // RET_CHUNK, RET_QK_WIDTH, RET_CHUNK), BF16),
        jax.ShapeDtypeStruct((b, s // MOBA_BLOCK, MOBA_WIDTH, MOBA_BLOCK), BF16),
        jax.ShapeDtypeStruct((b, s // MOBA_BLOCK, MOBA_WIDTH, MOBA_BLOCK), BF16),
    )
    tok = lambda w: pl.BlockSpec((1, tm, w), lambda si, bi: (bi, si, 0))
    return pl.pallas_call(
        _in_proj_kernel,
        grid=(ns, b),
        in_specs=[
            tok(d),
            pl.BlockSpec((1, d), const),
            pl.BlockSpec((d, NAT_WIDTH), const),
            pl.BlockSpec((TR_WIDTH, d), const),
            pl.BlockSpec((tm, RET_QK_WIDTH), lambda si, bi: (si, 0)),
            pl.BlockSpec((tm, RET_QK_WIDTH), lambda si, bi: (si, 0)),
            pl.BlockSpec((RET_QK_DIM // 2, tm), lambda si, bi: (0, si)),
            pl.BlockSpec((RET_QK_DIM // 2, tm), lambda si, bi: (0, si)),
        ],
        out_specs=(
            tok(RET_QK_WIDTH), tok(RET_WIDTH), tok(RET_WIDTH), tok(MOBA_WIDTH),
            pl.BlockSpec((1, 1, bpt, MOBA_WIDTH), lambda si, bi: (bi, si, 0, 0)),
            pl.BlockSpec((1, cpt, RET_QK_WIDTH, RET_CHUNK), lambda si, bi: (bi, si, 0, 0)),
            pl.BlockSpec((1, bpt, MOBA_WIDTH, MOBA_BLOCK), lambda si, bi: (bi, si, 0, 0)),
            pl.BlockSpec((1, bpt, MOBA_WIDTH, MOBA_BLOCK), lambda si, bi: (bi, si, 0, 0)),
        ),
        out_shape=out_shape,
        compiler_params=pltpu.CompilerParams(
            dimension_semantics=("arbitrary", "arbitrary"), vmem_limit_bytes=VMEM_LIMIT),
        name="in_proj",
    )(x, gain, w_nat, w_tr, cq, sq, ckt, skt)


def _retention_kernel(rq_ref, rkt_ref, rv_ref, sg_ref, dmat_ref, qd_ref, kd_ref, cd_ref,
                      out_ref, state_ref):
    @pl.when(pl.program_id(1) == 0)
    def _():
        state_ref[...] = jnp.zeros_like(state_ref)

    n_chunks = rq_ref.shape[1] // RET_CHUNK

    def chunk(c, carry):
        r0 = pl.multiple_of(c * RET_CHUNK, RET_CHUNK)
        rows = pl.ds(r0, RET_CHUNK)
        for hd in range(RET_HEADS):
            q = rq_ref[0, rows, hd * RET_QK_DIM:(hd + 1) * RET_QK_DIM]
            kt = rkt_ref[0, c, hd * RET_QK_DIM:(hd + 1) * RET_QK_DIM, :]
            v = rv_ref[0, rows, hd * RET_V_DIM:(hd + 1) * RET_V_DIM]
            state = state_ref[hd]
            scores = _dot(q, kt) * dmat_ref[hd]
            o = _dot(scores.astype(BF16), v) + qd_ref[hd] * _dot(q, state.astype(BF16))
            kts = (kt.astype(F32) * kd_ref[hd]).astype(BF16)
            state_ref[hd] = state * cd_ref[hd] + _dot(kts, v)
            ms = jnp.mean(o * o, axis=-1, keepdims=True)
            gate = sg_ref[0, rows, hd * RET_V_DIM:(hd + 1) * RET_V_DIM].astype(F32)
            out_ref[0, rows, hd * RET_V_DIM:(hd + 1) * RET_V_DIM] = (
                o * lax.rsqrt(ms + EPS) * gate).astype(BF16)
        return carry

    lax.fori_loop(0, n_chunks, chunk, 0, unroll=8)


def _retention(rq, rkt, rv, sg, dmat, qd, kd, cd):
    b, s, _ = rq.shape
    tc = min(RET_TOKENS, s)
    tok = lambda w: pl.BlockSpec((1, tc, w), lambda bi, si: (bi, si, 0))
    tab = lambda a: pl.BlockSpec(a.shape, lambda bi, si: (0,) * a.ndim)
    return pl.pallas_call(
        _retention_kernel,
        grid=(b, s // tc),
        in_specs=[
            tok(RET_QK_WIDTH),
            pl.BlockSpec((1, tc // RET_CHUNK, RET_QK_WIDTH, RET_CHUNK), lambda bi, si: (bi, si, 0, 0)),
            tok(RET_WIDTH), tok(RET_WIDTH),
            tab(dmat), tab(qd), tab(kd), tab(cd),
        ],
        out_specs=tok(RET_WIDTH),
        out_shape=jax.ShapeDtypeStruct((b, s, RET_WIDTH), BF16),
        scratch_shapes=[pltpu.VMEM((RET_HEADS, RET_QK_DIM, RET_V_DIM), F32)],
        compiler_params=pltpu.CompilerParams(
            dimension_semantics=("arbitrary", "arbitrary"), vmem_limit_bytes=VMEM_LIMIT),
        name="retention",
    )(rq, rkt, rv, sg, dmat, qd, kd, cd)


ONES_ROWS = 16
PAIR_WIDTH = 2 * MOBA_HEAD_DIM
GATE_BLOCKS = 4


def _moba_gate_kernel(cfar_ref, qt_ref, km_ref, selfar_ref, selprev_ref):
    d = MOBA_HEAD_DIM
    nb = km_ref.shape[2]
    n_q = qt_ref.shape[1]
    shape = (nb, n_q * MOBA_BLOCK)
    blk = lax.broadcasted_iota(jnp.int32, shape, 0)
    i = pl.program_id(1) * n_q + lax.broadcasted_iota(jnp.int32, shape, 1) // MOBA_BLOCK
    blk_f = blk.astype(F32)
    for h in range(MOBA_HEADS):
        qt = jnp.concatenate([qt_ref[0, c, h * d:(h + 1) * d, :] for c in range(n_q)], axis=1)
        km = km_ref[0, h]
        km_hi = km.astype(BF16)
        km_lo = (km - km_hi.astype(F32)).astype(BF16)
        gate = _dot(km_hi, qt) + _dot(km_lo, qt)
        gate = jnp.where(blk < i, gate, NEG)
        chosen = jnp.zeros(gate.shape, F32)
        for _ in range(MOBA_TOPK):
            best = jnp.max(gate, axis=0, keepdims=True)
            idx = jnp.min(jnp.where(gate == best, blk_f, float(nb)), axis=0, keepdims=True)
            hit = blk_f == idx
            chosen = jnp.where(hit & (best > 0.5 * NEG), 1.0, chosen)
            gate = jnp.where(hit, NEG, gate)
        picked = chosen > 0.5
        selfar_ref[0, h] = jnp.where(picked & (blk < i - 1), cfar_ref[h], NEG).astype(BF16)
        prev_hit = jnp.max(jnp.where(picked & (blk == i - 1), 1.0, 0.0), axis=0, keepdims=True)
        selprev_ref[0, h] = jnp.where(prev_hit > 0.5, 0.0, NEG)


def _moba_gate(cfar, mqt, kmean):
    b, nb, _, _ = mqt.shape
    s = nb * MOBA_BLOCK
    n_q = math.gcd(GATE_BLOCKS, nb)
    width = n_q * MOBA_BLOCK
    return pl.pallas_call(
        _moba_gate_kernel,
        grid=(b, nb // n_q),
        in_specs=[
            pl.BlockSpec(memory_space=pltpu.SMEM),
            pl.BlockSpec((1, n_q, MOBA_WIDTH, MOBA_BLOCK), lambda bi, i: (bi, i, 0, 0)),
            pl.BlockSpec((1, MOBA_HEADS, nb, MOBA_HEAD_DIM), lambda bi, i: (bi, 0, 0, 0)),
        ],
        out_specs=(
            pl.BlockSpec((1, MOBA_HEADS, nb, width), lambda bi, i: (bi, 0, 0, i)),
            pl.BlockSpec((1, MOBA_HEADS, 1, width), lambda bi, i: (bi, 0, 0, i)),
        ),
        out_shape=(
            jax.ShapeDtypeStruct((b, MOBA_HEADS, nb, s), BF16),
            jax.ShapeDtypeStruct((b, MOBA_HEADS, 1, s), F32),
        ),
        compiler_params=pltpu.CompilerParams(
            dimension_semantics=("arbitrary", "arbitrary"), vmem_limit_bytes=VMEM_LIMIT),
        name="moba_gate",
    )(cfar, mqt, kmean)


def _moba_kernel(qt_ref, k_ref, vt_ref, selfar_ref, selprev_ref, tnear_ref,
                 out_ref, kaug_ref, vaug_ref, qaug_ref, qnear_ref, s_ref, smax_ref, m_ref, acc_ref):
    i = pl.program_id(1)
    d = MOBA_HEAD_DIM
    nb = vaug_ref.shape[1]
    s_len = kaug_ref.shape[1]
    blk_rows = MOBA_BLOCK
    n_pairs = MOBA_HEADS // 2

    @pl.when((pl.program_id(0) == 0) & (i == 0))
    def _():
        row_blk = lax.broadcasted_iota(jnp.int32, (s_len, 2 * PAIR_WIDTH), 0) // MOBA_BLOCK
        lane = lax.broadcasted_iota(jnp.int32, (s_len, 2 * PAIR_WIDTH), 1)
        hot_lane = jnp.where(lane < PAIR_WIDTH, lane - d, lane - PAIR_WIDTH)
        in_band = (lane >= d) & (lane < PAIR_WIDTH + d)
        pattern = jnp.where(in_band & (row_blk == hot_lane), 1.0, 0.0).astype(BF16)
        for pe in range(n_pairs):
            kaug_ref[pe] = pattern
        for e in range(MOBA_HEADS):
            vaug_ref[e, :, d:, :] = jnp.ones((nb, ONES_ROWS, MOBA_BLOCK), BF16)

    own_rows = pl.ds(pl.multiple_of(i * blk_rows, blk_rows), blk_rows)
    for pe in range(n_pairs):
        kaug_ref[pe, own_rows, 0:d] = k_ref[0, :, pe * PAIR_WIDTH:pe * PAIR_WIDTH + d]
        kaug_ref[pe, own_rows, PAIR_WIDTH + d:] = k_ref[0, :, pe * PAIR_WIDTH + d:(pe + 1) * PAIR_WIDTH]
    for e in range(MOBA_HEADS):
        vaug_ref[e, i, 0:d, :] = vt_ref[0, 0, e * d:(e + 1) * d, :]

    prow = lax.broadcasted_iota(jnp.int32, (PAIR_WIDTH, MOBA_BLOCK), 0)
    for e in range(MOBA_HEADS):
        pe, he = divmod(e, 2)
        qt2 = qt_ref[0, 0, pe * PAIR_WIDTH:(pe + 1) * PAIR_WIDTH, :]
        q_only = jnp.where((prow >= he * d) & (prow < (he + 1) * d), qt2, jnp.zeros_like(qt2))
        qnear_ref[e] = q_only
        qaug_ref[e] = q_only
        hot_row = d if he == 0 else 0
        qaug_ref[e, hot_row:hot_row + nb, :] = selfar_ref[0, e]

    def qk(e, j, slot, bias=None):
        rows = pl.ds(pl.multiple_of(j * blk_rows, blk_rows), blk_rows)
        keys = kaug_ref[e // 2, rows, (e % 2) * PAIR_WIDTH:(e % 2 + 1) * PAIR_WIDTH]
        if bias is None:
            st = _dot(keys, qaug_ref[e])
        else:
            st = _dot(keys, qnear_ref[e]) + bias(e)
        s_ref[slot, e] = st
        smax_ref[slot, e] = jnp.max(st, axis=0, keepdims=True)

    def sm(e, j, slot, first=False):
        st = s_ref[slot, e]
        mj = smax_ref[slot, e]
        if first:
            m_ref[e] = mj
            acc_ref[e] = _dot(vaug_ref[e, j], jnp.exp2(st - mj).astype(BF16))
        else:
            m_old = m_ref[e]
            m_new = jnp.maximum(m_old, mj)
            pv = _dot(vaug_ref[e, j], jnp.exp2(st - m_new).astype(BF16))
            acc_ref[e] = acc_ref[e] * jnp.exp2(m_old - m_new) + pv
            m_ref[e] = m_new

    clamp = lambda j: jnp.minimum(j, i)
    heads = range(MOBA_HEADS)
    jp = jnp.maximum(i - 1, 0)
    own_bias = lambda e: tnear_ref[e, blk_rows:, :]
    prev_bias = lambda e: tnear_ref[e, 0:blk_rows, :] + selprev_ref[0, e]
    for blk, slot, bias in ((i, 0, own_bias), (jp, 1, prev_bias), (clamp(0), 2, None), (clamp(1), 3, None)):
        for e in heads:
            qk(e, blk, slot, bias=bias)
    for e in heads:
        sm(e, i, 0, first=True)
    for e in heads:
        sm(e, jp, 1)

    def far(t, carry):
        j = 4 * t
        for half in (0, 2):
            for e in heads:
                qk(e, clamp(j + half + 2), half)
            for e in heads:
                qk(e, clamp(j + half + 3), half + 1)
            for e in heads:
                sm(e, clamp(j + half), (half + 2) % 4)
            for e in heads:
                sm(e, clamp(j + half + 1), (half + 3) % 4)
        return carry

    lax.fori_loop(0, lax.shift_right_logical(i + 2, 2), far, 0)

    outs = []
    for e in range(MOBA_HEADS):
        acc = acc_ref[e]
        outs.append((acc[:d] / acc[d:d + 1]).T)
    out_ref[0] = jnp.concatenate(outs, axis=1).astype(BF16)


def _moba(mqt, mk, mvt, selfar, selprev, tnear):
    b, s, _ = mk.shape
    nb = s // MOBA_BLOCK
    assert nb <= MOBA_HEAD_DIM
    nh = MOBA_HEADS
    return pl.pallas_call(
        _moba_kernel,
        grid=(b, nb),
        in_specs=[
            pl.BlockSpec((1, 1, MOBA_WIDTH, MOBA_BLOCK), lambda bi, i: (bi, i, 0, 0)),
            pl.BlockSpec((1, MOBA_BLOCK, MOBA_WIDTH), lambda bi, i: (bi, i, 0)),
            pl.BlockSpec((1, 1, MOBA_WIDTH, MOBA_BLOCK), lambda bi, i: (bi, i, 0, 0)),
            pl.BlockSpec((1, nh, nb, MOBA_BLOCK), lambda bi, i: (bi, 0, 0, i)),
            pl.BlockSpec((1, nh, 1, MOBA_BLOCK), lambda bi, i: (bi, 0, 0, i)),
            pl.BlockSpec((nh, 2 * MOBA_BLOCK, MOBA_BLOCK), lambda bi, i: (0, 0, 0),
                         pipeline_mode=pl.Buffered(1)),
        ],
        out_specs=pl.BlockSpec((1, MOBA_BLOCK, MOBA_WIDTH), lambda bi, i: (bi, i, 0)),
        out_shape=jax.ShapeDtypeStruct((b, s, MOBA_WIDTH), BF16),
        scratch_shapes=[
            pltpu.VMEM((nh // 2, s, 2 * PAIR_WIDTH), BF16),
            pltpu.VMEM((nh, nb, MOBA_HEAD_DIM + ONES_ROWS, MOBA_BLOCK), BF16),
            pltpu.VMEM((nh, PAIR_WIDTH, MOBA_BLOCK), BF16),
            pltpu.VMEM((nh, PAIR_WIDTH, MOBA_BLOCK), BF16),
            pltpu.VMEM((4, nh, MOBA_BLOCK, MOBA_BLOCK), F32),
            pltpu.VMEM((4, nh, 1, MOBA_BLOCK), F32),
            pltpu.VMEM((nh, 1, MOBA_BLOCK), F32),
            pltpu.VMEM((nh, MOBA_HEAD_DIM + ONES_ROWS, MOBA_BLOCK), F32),
        ],
        compiler_params=pltpu.CompilerParams(
            dimension_semantics=("arbitrary", "arbitrary"), vmem_limit_bytes=VMEM_LIMIT),
        name="moba",
    )(mqt, mk, mvt, selfar, selprev, tnear)


def _rms(x, g):
    return x * lax.rsqrt(jnp.mean(x * x, axis=-1, keepdims=True) + EPS) * g


def _out_ffn_kernel(x_ref, ret_ref, moba_ref, wo_ref, wg_ref, wu_ref, wd_ref,
                    g_post_mix_ref, g_pre_ffn_ref, g_post_ffn_ref, out_ref):
    tm = x_ref.shape[0]
    groups = [slice(r0, r0 + tm // FFN_ROW_GROUPS) for r0 in range(0, tm, tm // FFN_ROW_GROUPS)]
    mixes = [_dot(jnp.concatenate([ret_ref[rows, :], moba_ref[rows, :]], axis=1), wo_ref[...])
             for rows in groups]
    for rows, mix in zip(groups, mixes):
        x1 = x_ref[rows, :] + _rms(mix, g_post_mix_ref[...])
        h = _rms(x1, g_pre_ffn_ref[...]).astype(BF16)
        f = None
        for lo, hi in FFN_CHUNKS:
            cols = slice(lo, hi)
            gate = _dot(h, wg_ref[:, cols])
            up = _dot(h, wu_ref[:, cols])
            act = (gate * jax.nn.sigmoid(gate) * up).astype(BF16)
            part = _dot(act, wd_ref[cols, :])
            f = part if f is None else f + part
        out_ref[rows, :] = x1 + _rms(f, g_post_ffn_ref[...])


def _out_ffn(x2, ret2, moba2, wo, wg, wu, wd, g_post_mix, g_pre_ffn, g_post_ffn):
    n, d = x2.shape
    tm = min(FFN_TOKENS, n)
    tok = lambda w: pl.BlockSpec((tm, w), lambda t: (t, 0))
    resident = lambda a: pl.BlockSpec(a.shape, lambda t: (0, 0), pipeline_mode=pl.Buffered(1))
    return pl.pallas_call(
        _out_ffn_kernel,
        grid=(n // tm,),
        in_specs=[tok(d), tok(RET_WIDTH), tok(MOBA_WIDTH),
                  resident(wo), resident(wg), resident(wu), resident(wd),
                  resident(g_post_mix), resident(g_pre_ffn), resident(g_post_ffn)],
        out_specs=tok(d),
        out_shape=jax.ShapeDtypeStruct((n, d), F32),
        compiler_params=pltpu.CompilerParams(
            dimension_semantics=("arbitrary",), vmem_limit_bytes=VMEM_LIMIT),
        name="out_ffn",
    )(x2, ret2, moba2, wo, wg, wu, wd, g_post_mix, g_pre_ffn, g_post_ffn)


def _rotary_tables(s):
    half = RET_QK_DIM // 2
    inv_freq = ROPE_BASE ** (-np.arange(half, dtype=np.float64) / half)
    ang = np.arange(s, dtype=np.float64)[:, None] * inv_freq[None, :]
    cos, sin = np.cos(ang), np.sin(ang)
    cq = np.tile(np.concatenate([cos, cos], axis=1), (1, RET_HEADS))
    sq = np.tile(np.concatenate([-sin, sin], axis=1), (1, RET_HEADS))
    return tuple(np.ascontiguousarray(t, dtype=np.float32) for t in (cq, sq, cos.T, sin.T))


def _retention_tables():
    c = RET_CHUNK
    log_gamma = np.log1p(-np.exp(np.linspace(math.log(1.0 / 32), math.log(1.0 / 512), RET_HEADS)))
    idx = np.arange(c, dtype=np.float64)
    diff = idx[:, None] - idx[None, :]
    dmat = np.where(diff >= 0, np.exp(np.maximum(diff, 0.0)[None] * log_gamma[:, None, None]), 0.0)
    q_decay = np.exp((idx + 1.0)[None, :] * log_gamma[:, None])
    k_decay = np.exp((c - 1.0 - idx)[None, :] * log_gamma[:, None])
    chunk_decay = np.exp(c * log_gamma)
    qd = np.broadcast_to(q_decay[:, :, None], (RET_HEADS, c, RET_V_DIM))
    kd = k_decay[:, None, :]
    cd = np.broadcast_to(chunk_decay[:, None, None], (RET_HEADS, 1, RET_V_DIM))
    return tuple(np.ascontiguousarray(t, dtype=np.float32) for t in (dmat, qd, kd, cd))


def _t5_bucket(rel):
    n = np.maximum(rel, 0)
    max_exact = REL_BUCKETS // 2
    large = max_exact + (np.log(np.maximum(n, 1) / max_exact) / math.log(REL_MAX_DIST / max_exact)
                         * (REL_BUCKETS - max_exact)).astype(np.int64)
    return np.where(n < max_exact, n, np.minimum(large, REL_BUCKETS - 1))


def _toeplitz(vec):
    h, two_n = vec.shape
    n = two_n // 2
    ext = jnp.concatenate([vec, jnp.zeros((h, 1), vec.dtype)], axis=1)
    skew = jnp.tile(ext, (1, n))[:, :n * two_n].reshape(h, n, two_n)
    return skew[:, :, n:]


def _bias_tables(rel_bias):
    bias_t = rel_bias.T.astype(F32) * LOG2E
    n = MOBA_BLOCK
    one_hot = (_t5_bucket(np.arange(2 * n))[:, None] == np.arange(REL_BUCKETS)).astype(np.float32)
    by_dist = jnp.einsum("rn,hn->hr", one_hot, bias_t, precision=lax.Precision.HIGHEST)
    tprev = _toeplitz(by_dist)
    own_vec = jnp.concatenate([jnp.full((bias_t.shape[0], n), NEG, F32), by_dist[:, :n]], axis=1)
    town = _toeplitz(own_vec)
    cfar = bias_t[:, REL_BUCKETS - 1]
    return jnp.concatenate([tprev, town], axis=1), cfar


def kernel(x, w_in, w_out, pre_mix_norm, post_mix_norm, pre_ffn_norm, post_ffn_norm,
           rel_bias, w_gate, w_up, w_down):
    b, s, d = x.shape
    assert d == D_MODEL and s % IN_TOKENS == 0 and MOBA_BLOCK + 1 >= REL_MAX_DIST
    depth = w_in.shape[0]

    cq, sq, ckt, skt = _rotary_tables(s)
    dmat, qd, kd, cd = _retention_tables()
    tnear, cfar = _bias_tables(rel_bias)

    sizes = [RET_QK_WIDTH, RET_QK_WIDTH, RET_WIDTH, RET_WIDTH, MOBA_WIDTH, MOBA_WIDTH, MOBA_WIDTH]
    o_rq, o_rk, o_rv, o_rg, o_mq, o_mk, o_mv, _ = np.cumsum([0] + sizes).tolist()

    for layer in range(depth):
        w = w_in[layer]
        w_nat = jnp.concatenate(
            [w[:, o_rq:o_rk], w[:, o_rv:o_mq], w[:, o_mk:o_mv]], axis=1).astype(BF16)
        w_tr = jnp.concatenate([w[:, o_rk:o_rv], w[:, o_mq:o_mk], w[:, o_mv:]], axis=1).T.astype(BF16)

        rq, rv, sg, mk, kmean, rkt, mqt, mvt = _in_proj(
            x, pre_mix_norm[layer][None, :], w_nat, w_tr, cq, sq, ckt, skt)

        ret = _retention(rq, rkt, rv, sg, dmat, qd, kd, cd)

        nb = s // MOBA_BLOCK
        kmean = kmean.reshape(b, nb, MOBA_HEADS, MOBA_HEAD_DIM).transpose(0, 2, 1, 3)
        selfar, selprev = _moba_gate(cfar, mqt, kmean)
        moba = _moba(mqt, mk, mvt, selfar, selprev, tnear)

        x = _out_ffn(
            x.reshape(b * s, d), ret.reshape(b * s, RET_WIDTH), moba.reshape(b * s, MOBA_WIDTH),
            w_out[layer].astype(BF16), w_gate[layer].astype(BF16), w_up[layer].astype(BF16),
            w_down[layer].astype(BF16), post_mix_norm[layer][None, :], pre_ffn_norm[layer][None, :],
            post_ffn_norm[layer][None, :]).reshape(b, s, d)
    return x
```

```python
import functools
import math

import jax
import jax.numpy as jnp
import numpy as np
from jax import lax
from jax.experimental import pallas as pl
from jax.experimental.pallas import tpu as pltpu

F32 = jnp.float32
BF16 = jnp.bfloat16

D_MODEL = 1024
RET_HEADS = 4
RET_QK_DIM = 64
RET_V_DIM = 128
RET_CHUNK = 128
RET_QK_WIDTH = RET_HEADS * RET_QK_DIM
RET_WIDTH = RET_HEADS * RET_V_DIM
MOBA_HEADS = 8
MOBA_HEAD_DIM = 64
MOBA_WIDTH = MOBA_HEADS * MOBA_HEAD_DIM
MOBA_BLOCK = 256
MOBA_TOPK = 3
REL_BUCKETS = 32
REL_MAX_DIST = 128
D_FF = 2816
EPS = 1e-6
ROPE_BASE = 10000.0

NEG = -1e30
LOG2E = math.log2(math.e)

IN_TOKENS = 1024
RET_TOKENS = 1024
FFN_TOKENS = 1024
FFN_ROW_GROUPS = 4
MXU_TILE = 256
FFN_CHUNKS = ((0, 6 * MXU_TILE), (6 * MXU_TILE, D_FF))
VMEM_LIMIT = 56 * 1024 * 1024

NAT_WIDTH = RET_QK_WIDTH + 2 * RET_WIDTH + MOBA_WIDTH
TR_WIDTH = RET_QK_WIDTH + 2 * MOBA_WIDTH


def _nt_dot(a, b):
    return lax.dot_general(a, b, (((1,), (1,)), ((), ())), preferred_element_type=F32)


def _dot(a, b):
    return jnp.dot(a, b, preferred_element_type=F32)


def _in_proj_kernel(x_ref, g_ref, wn_ref, wt_ref, cq_ref, sq_ref, ckt_ref, skt_ref,
                    rq_ref, rv_ref, sg_ref, mk_ref, kmean_ref, rkt_ref, mqt_ref, mvt_ref):
    tm = x_ref.shape[1]
    x = x_ref[0]
    ms = jnp.mean(x * x, axis=-1, keepdims=True)
    h = (x * lax.rsqrt(ms + EPS) * g_ref[...]).astype(BF16)

    pn = _dot(h, wn_ref[...])
    pt = _nt_dot(wt_ref[...], h)

    rq = pn[:, :RET_QK_WIDTH]
    lane = lax.broadcasted_iota(jnp.int32, rq.shape, 1)
    first_half = (lane % RET_QK_DIM) < (RET_QK_DIM // 2)
    partner = jnp.where(first_half,
                        pltpu.roll(rq, RET_QK_WIDTH - RET_QK_DIM // 2, 1),
                        pltpu.roll(rq, RET_QK_DIM // 2, 1))
    rq_ref[0] = (rq * cq_ref[...] + partner * sq_ref[...]).astype(BF16)

    rv_ref[0] = pn[:, RET_QK_WIDTH:RET_QK_WIDTH + RET_WIDTH].astype(BF16)
    rg = pn[:, RET_QK_WIDTH + RET_WIDTH:RET_QK_WIDTH + 2 * RET_WIDTH]
    sg_ref[0] = (rg * jax.nn.sigmoid(rg)).astype(BF16)

    mk = pn[:, RET_QK_WIDTH + 2 * RET_WIDTH:]
    mk_ref[0] = mk.astype(BF16)
    for blk in range(tm // MOBA_BLOCK):
        kmean_ref[0, 0, blk:blk + 1, :] = jnp.mean(
            mk[blk * MOBA_BLOCK:(blk + 1) * MOBA_BLOCK], axis=0, keepdims=True)

    half = RET_QK_DIM // 2
    cos_t = ckt_ref[...]
    sin_t = skt_ref[...]
    parts = []
    for hd in range(RET_HEADS):
        x1 = pt[hd * RET_QK_DIM:hd * RET_QK_DIM + half]
        x2 = pt[hd * RET_QK_DIM + half:(hd + 1) * RET_QK_DIM]
        parts.append(x1 * cos_t - x2 * sin_t)
        parts.append(x2 * cos_t + x1 * sin_t)
    rkt = (jnp.concatenate(parts, axis=0) * (RET_QK_DIM ** -0.5)).astype(BF16)
    for c in range(tm // RET_CHUNK):
        rkt_ref[0, c] = rkt[:, c * RET_CHUNK:(c + 1) * RET_CHUNK]

    mqt = (pt[RET_QK_WIDTH:RET_QK_WIDTH + MOBA_WIDTH] * (MOBA_HEAD_DIM ** -0.5 * LOG2E)).astype(BF16)
    mvt = pt[RET_QK_WIDTH + MOBA_WIDTH:].astype(BF16)
    for blk in range(tm // MOBA_BLOCK):
        mqt_ref[0, blk] = mqt[:, blk * MOBA_BLOCK:(blk + 1) * MOBA_BLOCK]
        mvt_ref[0, blk] = mvt[:, blk * MOBA_BLOCK:(blk + 1) * MOBA_BLOCK]


def _in_proj(x, gain, w_nat, w_tr, cq, sq, ckt, skt):
    b, s, d = x.shape
    tm = IN_TOKENS
    ns = s // tm
    bpt = tm // MOBA_BLOCK
    cpt = tm // RET_CHUNK
    const = lambda si, bi: (0, 0)
    out_shape = (
        jax.ShapeDtypeStruct((b, s, RET_QK_WIDTH), BF16),
        jax.ShapeDtypeStruct((b, s, RET_WIDTH), BF16),
        jax.ShapeDtypeStruct((b, s, RET_WIDTH), BF16),
        jax.ShapeDtypeStruct((b, s, MOBA_WIDTH), BF16),
        jax.ShapeDtypeStruct((b, ns, bpt, MOBA_WIDTH), F32),
        jax.ShapeDtypeStruct((b, s // RET_CHUNK, RET_QK_WIDTH, RET_CHUNK), BF16),
        jax.ShapeDtypeStruct((b, s // MOBA_BLOCK, MOBA_WIDTH, MOBA_BLOCK), BF16),
        jax.ShapeDtypeStruct((b, s // MOBA_BLOCK, MOBA_WIDTH, MOBA_BLOCK), BF16),
    )
    tok = lambda w: pl.BlockSpec((1, tm, w), lambda si, bi: (bi, si, 0))
    return pl.pallas_call(
        _in_proj_kernel,
        grid=(ns, b),
        in_specs=[
            tok(d),
            pl.BlockSpec((1, d), const),
            pl.BlockSpec((d, NAT_WIDTH), const),
            pl.BlockSpec((TR_WIDTH, d), const),
            pl.BlockSpec((tm, RET_QK_WIDTH), lambda si, bi: (si, 0)),
            pl.BlockSpec((tm, RET_QK_WIDTH), lambda si, bi: (si, 0)),
            pl.BlockSpec((RET_QK_DIM // 2, tm), lambda si, bi: (0, si)),
            pl.BlockSpec((RET_QK_DIM // 2, tm), lambda si, bi: (0, si)),
        ],
        out_specs=(
            tok(RET_QK_WIDTH), tok(RET_WIDTH), tok(RET_WIDTH), tok(MOBA_WIDTH),
            pl.BlockSpec((1, 1, bpt, MOBA_WIDTH), lambda si, bi: (bi, si, 0, 0)),
            pl.BlockSpec((1, cpt, RET_QK_WIDTH, RET_CHUNK), lambda si, bi: (bi, si, 0, 0)),
            pl.BlockSpec((1, bpt, MOBA_WIDTH, MOBA_BLOCK), lambda si, bi: (bi, si, 0, 0)),
            pl.BlockSpec((1, bpt, MOBA_WIDTH, MOBA_BLOCK), lambda si, bi: (bi, si, 0, 0)),
        ),
        out_shape=out_shape,
        compiler_params=pltpu.CompilerParams(
            dimension_semantics=("arbitrary", "arbitrary"), vmem_limit_bytes=VMEM_LIMIT),
        name="in_proj",
    )(x, gain, w_nat, w_tr, cq, sq, ckt, skt)


def _retention_kernel(rq_ref, rkt_ref, rv_ref, sg_ref, dmat_ref, qd_ref, kd_ref, cd_ref,
                      out_ref, state_ref):
    @pl.when(pl.program_id(1) == 0)
    def _():
        state_ref[...] = jnp.zeros_like(state_ref)

    n_chunks = rq_ref.shape[1] // RET_CHUNK

    def chunk(c, carry):
        r0 = pl.multiple_of(c * RET_CHUNK, RET_CHUNK)
        rows = pl.ds(r0, RET_CHUNK)
        for hd in range(RET_HEADS):
            q = rq_ref[0, rows, hd * RET_QK_DIM:(hd + 1) * RET_QK_DIM]
            kt = rkt_ref[0, c, hd * RET_QK_DIM:(hd + 1) * RET_QK_DIM, :]
            v = rv_ref[0, rows, hd * RET_V_DIM:(hd + 1) * RET_V_DIM]
            state = state_ref[hd]
            scores = _dot(q, kt) * dmat_ref[hd]
            o = _dot(scores.astype(BF16), v) + qd_ref[hd] * _dot(q, state.astype(BF16))
            kts = (kt.astype(F32) * kd_ref[hd]).astype(BF16)
            state_ref[hd] = state * cd_ref[hd] + _dot(kts, v)
            ms = jnp.mean(o * o, axis=-1, keepdims=True)
            gate = sg_ref[0, rows, hd * RET_V_DIM:(hd + 1) * RET_V_DIM].astype(F32)
            out_ref[0, rows, hd * RET_V_DIM:(hd + 1) * RET_V_DIM] = (
                o * lax.rsqrt(ms + EPS) * gate).astype(BF16)
        return carry

    lax.fori_loop(0, n_chunks, chunk, 0, unroll=8)


def _retention(rq, rkt, rv, sg, dmat, qd, kd, cd):
    b, s, _ = rq.shape
    tc = min(RET_TOKENS, s)
    tok = lambda w: pl.BlockSpec((1, tc, w), lambda bi, si: (bi, si, 0))
    tab = lambda a: pl.BlockSpec(a.shape, lambda bi, si: (0,) * a.ndim)
    return pl.pallas_call(
        _retention_kernel,
        grid=(b, s // tc),
        in_specs=[
            tok(RET_QK_WIDTH),
            pl.BlockSpec((1, tc // RET_CHUNK, RET_QK_WIDTH, RET_CHUNK), lambda bi, si: (bi, si, 0, 0)),
            tok(RET_WIDTH), tok(RET_WIDTH),
            tab(dmat), tab(qd), tab(kd), tab(cd),
        ],
        out_specs=tok(RET_WIDTH),
        out_shape=jax.ShapeDtypeStruct((b, s, RET_WIDTH), BF16),
        scratch_shapes=[pltpu.VMEM((RET_HEADS, RET_QK_DIM, RET_V_DIM), F32)],
        compiler_params=pltpu.CompilerParams(
            dimension_semantics=("arbitrary", "arbitrary"), vmem_limit_bytes=VMEM_LIMIT),
        name="retention",
    )(rq, rkt, rv, sg, dmat, qd, kd, cd)


ONES_ROWS = 16
PAIR_WIDTH = 2 * MOBA_HEAD_DIM
GATE_BLOCKS = 4


def _moba_gate_kernel(cfar_ref, qt_ref, km_ref, selfar_ref, selprev_ref):
    d = MOBA_HEAD_DIM
    nb = km_ref.shape[2]
    n_q = qt_ref.shape[1]
    shape = (nb, n_q * MOBA_BLOCK)
    blk = lax.broadcasted_iota(jnp.int32, shape, 0)
    i = pl.program_id(1) * n_q + lax.broadcasted_iota(jnp.int32, shape, 1) // MOBA_BLOCK
    blk_f = blk.astype(F32)
    for h in range(MOBA_HEADS):
        qt = jnp.concatenate([qt_ref[0, c, h * d:(h + 1) * d, :] for c in range(n_q)], axis=1)
        km = km_ref[0, h]
        km_hi = km.astype(BF16)
        km_lo = (km - km_hi.astype(F32)).astype(BF16)
        gate = _dot(km_hi, qt) + _dot(km_lo, qt)
        gate = jnp.where(blk < i, gate, NEG)
        chosen = jnp.zeros(gate.shape, F32)
        for _ in range(MOBA_TOPK):
            best = jnp.max(gate, axis=0, keepdims=True)
            idx = jnp.min(jnp.where(gate == best, blk_f, float(nb)), axis=0, keepdims=True)
            hit = blk_f == idx
            chosen = jnp.where(hit & (best > 0.5 * NEG), 1.0, chosen)
            gate = jnp.where(hit, NEG, gate)
        picked = chosen > 0.5
        selfar_ref[0, h] = jnp.where(picked & (blk < i - 1), cfar_ref[h], NEG).astype(BF16)
        prev_hit = jnp.max(jnp.where(picked & (blk == i - 1), 1.0, 0.0), axis=0, keepdims=True)
        selprev_ref[0, h] = jnp.where(prev_hit > 0.5, 0.0, NEG)


def _moba_gate(cfar, mqt, kmean):
    b, nb, _, _ = mqt.shape
    s = nb * MOBA_BLOCK
    n_q = math.gcd(GATE_BLOCKS, nb)
    width = n_q * MOBA_BLOCK
    return pl.pallas_call(
        _moba_gate_kernel,
        grid=(b, nb // n_q),
        in_specs=[
            pl.BlockSpec(memory_space=pltpu.SMEM),
            pl.BlockSpec((1, n_q, MOBA_WIDTH, MOBA_BLOCK), lambda bi, i: (bi, i, 0, 0)),
            pl.BlockSpec((1, MOBA_HEADS, nb, MOBA_HEAD_DIM), lambda bi, i: (bi, 0, 0, 0)),
        ],
        out_specs=(
            pl.BlockSpec((1, MOBA_HEADS, nb, width), lambda bi, i: (bi, 0, 0, i)),
            pl.BlockSpec((1, MOBA_HEADS, 1, width), lambda bi, i: (bi, 0, 0, i)),
        ),
        out_shape=(
            jax.ShapeDtypeStruct((b, MOBA_HEADS, nb, s), BF16),
            jax.ShapeDtypeStruct((b, MOBA_HEADS, 1, s), F32),
        ),
        compiler_params=pltpu.CompilerParams(
            dimension_semantics=("arbitrary", "arbitrary"), vmem_limit_bytes=VMEM_LIMIT),
        name="moba_gate",
    )(cfar, mqt, kmean)


def _moba_kernel(qt_ref, k_ref, vt_ref, selfar_ref, selprev_ref, tnear_ref,
                 out_ref, kaug_ref, vaug_ref, qaug_ref, qnear_ref, s_ref, m_ref, acc_ref):
    i = pl.program_id(1)
    d = MOBA_HEAD_DIM
    nb = vaug_ref.shape[1]
    s_len = kaug_ref.shape[1]
    blk_rows = MOBA_BLOCK
    n_pairs = MOBA_HEADS // 2

    @pl.when((pl.program_id(0) == 0) & (i == 0))
    def _():
        row_blk = lax.broadcasted_iota(jnp.int32, (s_len, 2 * PAIR_WIDTH), 0) // MOBA_BLOCK
        lane = lax.broadcasted_iota(jnp.int32, (s_len, 2 * PAIR_WIDTH), 1)
        hot_lane = jnp.where(lane < PAIR_WIDTH, lane - d, lane - PAIR_WIDTH)
        in_band = (lane >= d) & (lane < PAIR_WIDTH + d)
        pattern = jnp.where(in_band & (row_blk == hot_lane), 1.0, 0.0).astype(BF16)
        for pe in range(n_pairs):
            kaug_ref[pe] = pattern
        for e in range(MOBA_HEADS):
            vaug_ref[e, :, d:, :] = jnp.ones((nb, ONES_ROWS, MOBA_BLOCK), BF16)

    own_rows = pl.ds(pl.multiple_of(i * blk_rows, blk_rows), blk_rows)
    for pe in range(n_pairs):
        kaug_ref[pe, own_rows, 0:d] = k_ref[0, :, pe * PAIR_WIDTH:pe * PAIR_WIDTH + d]
        kaug_ref[pe, own_rows, PAIR_WIDTH + d:] = k_ref[0, :, pe * PAIR_WIDTH + d:(pe + 1) * PAIR_WIDTH]
    for e in range(MOBA_HEADS):
        vaug_ref[e, i, 0:d, :] = vt_ref[0, 0, e * d:(e + 1) * d, :]

    prow = lax.broadcasted_iota(jnp.int32, (PAIR_WIDTH, MOBA_BLOCK), 0)
    for e in range(MOBA_HEADS):
        pe, he = divmod(e, 2)
        qt2 = qt_ref[0, 0, pe * PAIR_WIDTH:(pe + 1) * PAIR_WIDTH, :]
        q_only = jnp.where((prow >= he * d) & (prow < (he + 1) * d), qt2, jnp.zeros_like(qt2))
        qnear_ref[e] = q_only
        qaug_ref[e] = q_only
        hot_row = d if he == 0 else 0
        qaug_ref[e, hot_row:hot_row + nb, :] = selfar_ref[0, e]

    def qk(e, j, slot, near=False):
        rows = pl.ds(pl.multiple_of(j * blk_rows, blk_rows), blk_rows)
        keys = kaug_ref[e // 2, rows, (e % 2) * PAIR_WIDTH:(e % 2 + 1) * PAIR_WIDTH]
        s_ref[slot, e] = _dot(keys, qnear_ref[e] if near else qaug_ref[e])

    def sm(e, j, slot, one_sweep, bias=None, first=False):
        st = s_ref[slot, e]
        if bias is not None:
            st = st + bias
        if first:
            mj = jnp.max(st, axis=0, keepdims=True)
            m_ref[e] = mj
            acc_ref[e] = _dot(vaug_ref[e, j], jnp.exp2(st - mj).astype(BF16))
        elif one_sweep:
            m_old = m_ref[e]
            p = jnp.exp2(st - m_old).astype(BF16)
            grow = jnp.maximum(jnp.max(p, axis=0, keepdims=True).astype(F32), 1.0)
            acc_ref[e] = (acc_ref[e] + _dot(vaug_ref[e, j], p)) / grow
            m_ref[e] = m_old + jnp.log2(grow)
        else:
            m_old = m_ref[e]
            m_new = jnp.maximum(m_old, jnp.max(st, axis=0, keepdims=True))
            pv = _dot(vaug_ref[e, j], jnp.exp2(st - m_new).astype(BF16))
            acc_ref[e] = acc_ref[e] * jnp.exp2(m_old - m_new) + pv
            m_ref[e] = m_new

    clamp = lambda j: jnp.minimum(j, i)
    heads = range(MOBA_HEADS)
    jp = jnp.maximum(i - 1, 0)

    def attend(one_sweep):
        for blk, slot, near in ((i, 0, True), (jp, 1, True), (clamp(0), 2, False), (clamp(1), 3, False)):
            for e in heads:
                qk(e, blk, slot, near=near)
        for e in heads:
            sm(e, i, 0, one_sweep, bias=tnear_ref[e, blk_rows:, :], first=True)
        for e in heads:
            sm(e, jp, 1, one_sweep, bias=tnear_ref[e, 0:blk_rows, :] + selprev_ref[0, e])

        def far(t, carry):
            j = 4 * t
            for half in (0, 2):
                for e in heads:
                    qk(e, clamp(j + half + 2), half)
                for e in heads:
                    qk(e, clamp(j + half + 3), half + 1)
                for e in heads:
                    sm(e, clamp(j + half), (half + 2) % 4, one_sweep)
                for e in heads:
                    sm(e, clamp(j + half + 1), (half + 3) % 4, one_sweep)
            return carry

        lax.fori_loop(0, lax.shift_right_logical(i + 2, 2), far, 0)

    attend(one_sweep=True)
    check = jnp.sum(acc_ref[...]) + jnp.sum(m_ref[...])
    pl.when(jnp.logical_not(jnp.abs(check) < jnp.inf))(lambda: attend(one_sweep=False))

    outs = []
    for e in range(MOBA_HEADS):
        acc = acc_ref[e]
        outs.append((acc[:d] / acc[d:d + 1]).T)
    out_ref[0] = jnp.concatenate(outs, axis=1).astype(BF16)


def _moba(mqt, mk, mvt, selfar, selprev, tnear):
    b, s, _ = mk.shape
    nb = s // MOBA_BLOCK
    assert nb <= MOBA_HEAD_DIM
    nh = MOBA_HEADS
    return pl.pallas_call(
        _moba_kernel,
        grid=(b, nb),
        in_specs=[
            pl.BlockSpec((1, 1, MOBA_WIDTH, MOBA_BLOCK), lambda bi, i: (bi, i, 0, 0)),
            pl.BlockSpec((1, MOBA_BLOCK, MOBA_WIDTH), lambda bi, i: (bi, i, 0)),
            pl.BlockSpec((1, 1, MOBA_WIDTH, MOBA_BLOCK), lambda bi, i: (bi, i, 0, 0)),
            pl.BlockSpec((1, nh, nb, MOBA_BLOCK), lambda bi, i: (bi, 0, 0, i)),
            pl.BlockSpec((1, nh, 1, MOBA_BLOCK), lambda bi, i: (bi, 0, 0, i)),
            pl.BlockSpec((nh, 2 * MOBA_BLOCK, MOBA_BLOCK), lambda bi, i: (0, 0, 0),
                         pipeline_mode=pl.Buffered(1)),
        ],
        out_specs=pl.BlockSpec((1, MOBA_BLOCK, MOBA_WIDTH), lambda bi, i: (bi, i, 0)),
        out_shape=jax.ShapeDtypeStruct((b, s, MOBA_WIDTH), BF16),
        scratch_shapes=[
            pltpu.VMEM((nh // 2, s, 2 * PAIR_WIDTH), BF16),
            pltpu.VMEM((nh, nb, MOBA_HEAD_DIM + ONES_ROWS, MOBA_BLOCK), BF16),
            pltpu.VMEM((nh, PAIR_WIDTH, MOBA_BLOCK), BF16),
            pltpu.VMEM((nh, PAIR_WIDTH, MOBA_BLOCK), BF16),
            pltpu.VMEM((4, nh, MOBA_BLOCK, MOBA_BLOCK), F32),
            pltpu.VMEM((nh, 1, MOBA_BLOCK), F32),
            pltpu.VMEM((nh, MOBA_HEAD_DIM + ONES_ROWS, MOBA_BLOCK), F32),
        ],
        compiler_params=pltpu.CompilerParams(
            dimension_semantics=("arbitrary", "arbitrary"), vmem_limit_bytes=VMEM_LIMIT),
        name="moba",
    )(mqt, mk, mvt, selfar, selprev, tnear)


def _rms(x, g):
    return x * lax.rsqrt(jnp.mean(x * x, axis=-1, keepdims=True) + EPS) * g


def _out_ffn_kernel(x_ref, ret_ref, moba_ref, wo_ref, wg_ref, wu_ref, wd_ref,
                    g_post_mix_ref, g_pre_ffn_ref, g_post_ffn_ref, out_ref):
    tm = x_ref.shape[0]
    groups = [slice(r0, r0 + tm // FFN_ROW_GROUPS) for r0 in range(0, tm, tm // FFN_ROW_GROUPS)]
    mixes = [_dot(jnp.concatenate([ret_ref[rows, :], moba_ref[rows, :]], axis=1), wo_ref[...])
             for rows in groups]
    for rows, mix in zip(groups, mixes):
        x1 = x_ref[rows, :] + _rms(mix, g_post_mix_ref[...])
        h = _rms(x1, g_pre_ffn_ref[...]).astype(BF16)
        f = None
        for lo, hi in FFN_CHUNKS:
            cols = slice(lo, hi)
            gate = _dot(h, wg_ref[:, cols])
            up = _dot(h, wu_ref[:, cols])
            act = (gate * jax.nn.sigmoid(gate) * up).astype(BF16)
            part = _dot(act, wd_ref[cols, :])
            f = part if f is None else f + part
        out_ref[rows, :] = x1 + _rms(f, g_post_ffn_ref[...])


def _out_ffn(x2, ret2, moba2, wo, wg, wu, wd, g_post_mix, g_pre_ffn, g_post_ffn):
    n, d = x2.shape
    tm = min(FFN_TOKENS, n)
    tok = lambda w: pl.BlockSpec((tm, w), lambda t: (t, 0))
    resident = lambda a: pl.BlockSpec(a.shape, lambda t: (0, 0), pipeline_mode=pl.Buffered(1))
    return pl.pallas_call(
        _out_ffn_kernel,
        grid=(n // tm,),
        in_specs=[tok(d), tok(RET_WIDTH), tok(MOBA_WIDTH),
                  resident(wo), resident(wg), resident(wu), resident(wd),
                  resident(g_post_mix), resident(g_pre_ffn), resident(g_post_ffn)],
        out_specs=tok(d),
        out_shape=jax.ShapeDtypeStruct((n, d), F32),
        compiler_params=pltpu.CompilerParams(
            dimension_semantics=("arbitrary",), vmem_limit_bytes=VMEM_LIMIT),
        name="out_ffn",
    )(x2, ret2, moba2, wo, wg, wu, wd, g_post_mix, g_pre_ffn, g_post_ffn)


def _rotary_tables(s):
    half = RET_QK_DIM // 2
    inv_freq = ROPE_BASE ** (-np.arange(half, dtype=np.float64) / half)
    ang = np.arange(s, dtype=np.float64)[:, None] * inv_freq[None, :]
    cos, sin = np.cos(ang), np.sin(ang)
    cq = np.tile(np.concatenate([cos, cos], axis=1), (1, RET_HEADS))
    sq = np.tile(np.concatenate([-sin, sin], axis=1), (1, RET_HEADS))
    return tuple(np.ascontiguousarray(t, dtype=np.float32) for t in (cq, sq, cos.T, sin.T))


def _retention_tables():
    c = RET_CHUNK
    log_gamma = np.log1p(-np.exp(np.linspace(math.log(1.0 / 32), math.log(1.0 / 512), RET_HEADS)))
    idx = np.arange(c, dtype=np.float64)
    diff = idx[:, None] - idx[None, :]
    dmat = np.where(diff >= 0, np.exp(np.maximum(diff, 0.0)[None] * log_gamma[:, None, None]), 0.0)
    q_decay = np.exp((idx + 1.0)[None, :] * log_gamma[:, None])
    k_decay = np.exp((c - 1.0 - idx)[None, :] * log_gamma[:, None])
    chunk_decay = np.exp(c * log_gamma)
    qd = np.broadcast_to(q_decay[:, :, None], (RET_HEADS, c, RET_V_DIM))
    kd = k_decay[:, None, :]
    cd = np.broadcast_to(chunk_decay[:, None, None], (RET_HEADS, 1, RET_V_DIM))
    return tuple(np.ascontiguousarray(t, dtype=np.float32) for t in (dmat, qd, kd, cd))


def _t5_bucket(rel):
    n = np.maximum(rel, 0)
    max_exact = REL_BUCKETS // 2
    large = max_exact + (np.log(np.maximum(n, 1) / max_exact) / math.log(REL_MAX_DIST / max_exact)
                         * (REL_BUCKETS - max_exact)).astype(np.int64)
    return np.where(n < max_exact, n, np.minimum(large, REL_BUCKETS - 1))


def _toeplitz(vec):
    h, two_n = vec.shape
    n = two_n // 2
    ext = jnp.concatenate([vec, jnp.zeros((h, 1), vec.dtype)], axis=1)
    skew = jnp.tile(ext, (1, n))[:, :n * two_n].reshape(h, n, two_n)
    return skew[:, :, n:]


def _bias_tables(rel_bias):
    bias_t = rel_bias.T.astype(F32) * LOG2E
    n = MOBA_BLOCK
    one_hot = (_t5_bucket(np.arange(2 * n))[:, None] == np.arange(REL_BUCKETS)).astype(np.float32)
    by_dist = jnp.einsum("rn,hn->hr", one_hot, bias_t, precision=lax.Precision.HIGHEST)
    tprev = _toeplitz(by_dist)
    own_vec = jnp.concatenate([jnp.full((bias_t.shape[0], n), NEG, F32), by_dist[:, :n]], axis=1)
    town = _toeplitz(own_vec)
    cfar = bias_t[:, REL_BUCKETS - 1]
    return jnp.concatenate([tprev, town], axis=1), cfar


def kernel(x, w_in, w_out, pre_mix_norm, post_mix_norm, pre_ffn_norm, post_ffn_norm,
           rel_bias, w_gate, w_up, w_down):
    b, s, d = x.shape
    assert d == D_MODEL and s % IN_TOKENS == 0 and MOBA_BLOCK + 1 >= REL_MAX_DIST
    depth = w_in.shape[0]

    cq, sq, ckt, skt = _rotary_tables(s)
    dmat, qd, kd, cd = _retention_tables()
    tnear, cfar = _bias_tables(rel_bias)

    sizes = [RET_QK_WIDTH, RET_QK_WIDTH, RET_WIDTH, RET_WIDTH, MOBA_WIDTH, MOBA_WIDTH, MOBA_WIDTH]
    o_rq, o_rk, o_rv, o_rg, o_mq, o_mk, o_mv, _ = np.cumsum([0] + sizes).tolist()

    for layer in range(depth):
        w = w_in[layer]
        w_nat = jnp.concatenate(
            [w[:, o_rq:o_rk], w[:, o_rv:o_mq], w[:, o_mk:o_mv]], axis=1).astype(BF16)
        w_tr = jnp.concatenate([w[:, o_rk:o_rv], w[:, o_mq:o_mk], w[:, o_mv:]], axis=1).T.astype(BF16)

        rq, rv, sg, mk, kmean, rkt, mqt, mvt = _in_proj(
            x, pre_mix_norm[layer][None, :], w_nat, w_tr, cq, sq, ckt, skt)

        ret = _retention(rq, rkt, rv, sg, dmat, qd, kd, cd)

        nb = s // MOBA_BLOCK
        kmean = kmean.reshape(b, nb, MOBA_HEADS, MOBA_HEAD_DIM).transpose(0, 2, 1, 3)
        selfar, selprev = _moba_gate(cfar, mqt, kmean)
        moba = _moba(mqt, mk, mvt, selfar, selprev, tnear)

        x = _out_ffn(
            x.reshape(b * s, d), ret.reshape(b * s, RET_WIDTH), moba.reshape(b * s, MOBA_WIDTH),
            w_out[layer].astype(BF16), w_gate[layer].astype(BF16), w_up[layer].astype(BF16),
            w_down[layer].astype(BF16), post_mix_norm[layer][None, :], pre_ffn_norm[layer][None, :],
            post_ffn_norm[layer][None, :]).reshape(b, s, d)
    return x
```

```python
import functools
import math

import jax
import jax.numpy as jnp
import numpy as np
from jax import lax
from jax.experimental import pallas as pl
from jax.experimental.pallas import tpu as pltpu

F32 = jnp.float32
BF16 = jnp.bfloat16

D_MODEL = 1024
RET_HEADS = 4
RET_QK_DIM = 64
RET_V_DIM = 128
RET_CHUNK = 128
RET_QK_WIDTH = RET_HEADS * RET_QK_DIM
RET_WIDTH = RET_HEADS * RET_V_DIM
MOBA_HEADS = 8
MOBA_HEAD_DIM = 64
MOBA_WIDTH = MOBA_HEADS * MOBA_HEAD_DIM
MOBA_BLOCK = 256
MOBA_TOPK = 3
REL_BUCKETS = 32
REL_MAX_DIST = 128
D_FF = 2816
EPS = 1e-6
ROPE_BASE = 10000.0

NEG = -1e30
LOG2E = math.log2(math.e)

IN_TOKENS = 1024
RET_TOKENS = 1024
FFN_TOKENS = 1024
FFN_ROW_GROUPS = 4
MXU_TILE = 256
FFN_CHUNKS = ((0, 6 * MXU_TILE), (6 * MXU_TILE, D_FF))
VMEM_LIMIT = 56 * 1024 * 1024

NAT_WIDTH = RET_QK_WIDTH + 2 * RET_WIDTH + MOBA_WIDTH
TR_WIDTH = RET_QK_WIDTH + 2 * MOBA_WIDTH


def _nt_dot(a, b):
    return lax.dot_general(a, b, (((1,), (1,)), ((), ())), preferred_element_type=F32)


def _dot(a, b):
    return jnp.dot(a, b, preferred_element_type=F32)


def _in_proj_kernel(x_ref, g_ref, wn_ref, wt_ref, cq_ref, sq_ref, ckt_ref, skt_ref,
                    rq_ref, rv_ref, sg_ref, mk_ref, kmean_ref, rkt_ref, mqt_ref, mvt_ref):
    tm = x_ref.shape[1]
    x = x_ref[0]
    ms = jnp.mean(x * x, axis=-1, keepdims=True)
    h = (x * lax.rsqrt(ms + EPS) * g_ref[...]).astype(BF16)

    pn = _dot(h, wn_ref[...])
    pt = _nt_dot(wt_ref[...], h)

    rq = pn[:, :RET_QK_WIDTH]
    lane = lax.broadcasted_iota(jnp.int32, rq.shape, 1)
    first_half = (lane % RET_QK_DIM) < (RET_QK_DIM // 2)
    partner = jnp.where(first_half,
                        pltpu.roll(rq, RET_QK_WIDTH - RET_QK_DIM // 2, 1),
                        pltpu.roll(rq, RET_QK_DIM // 2, 1))
    rq_ref[0] = (rq * cq_ref[...] + partner * sq_ref[...]).astype(BF16)

    rv_ref[0] = pn[:, RET_QK_WIDTH:RET_QK_WIDTH + RET_WIDTH].astype(BF16)
    rg = pn[:, RET_QK_WIDTH + RET_WIDTH:RET_QK_WIDTH + 2 * RET_WIDTH]
    sg_ref[0] = (rg * jax.nn.sigmoid(rg)).astype(BF16)

    mk = pn[:, RET_QK_WIDTH + 2 * RET_WIDTH:]
    mk_ref[0] = mk.astype(BF16)
    for blk in range(tm // MOBA_BLOCK):
        kmean_ref[0, 0, blk:blk + 1, :] = jnp.mean(
            mk[blk * MOBA_BLOCK:(blk + 1) * MOBA_BLOCK], axis=0, keepdims=True)

    half = RET_QK_DIM // 2
    cos_t = ckt_ref[...]
    sin_t = skt_ref[...]
    parts = []
    for hd in range(RET_HEADS):
        x1 = pt[hd * RET_QK_DIM:hd * RET_QK_DIM + half]
        x2 = pt[hd * RET_QK_DIM + half:(hd + 1) * RET_QK_DIM]
        parts.append(x1 * cos_t - x2 * sin_t)
        parts.append(x2 * cos_t + x1 * sin_t)
    rkt = (jnp.concatenate(parts, axis=0) * (RET_QK_DIM ** -0.5)).astype(BF16)
    for c in range(tm // RET_CHUNK):
        rkt_ref[0, c] = rkt[:, c * RET_CHUNK:(c + 1) * RET_CHUNK]

    mqt = (pt[RET_QK_WIDTH:RET_QK_WIDTH + MOBA_WIDTH] * (MOBA_HEAD_DIM ** -0.5 * LOG2E)).astype(BF16)
    mvt = pt[RET_QK_WIDTH + MOBA_WIDTH:].astype(BF16)
    for blk in range(tm // MOBA_BLOCK):
        mqt_ref[0, blk] = mqt[:, blk * MOBA_BLOCK:(blk + 1) * MOBA_BLOCK]
        mvt_ref[0, blk] = mvt[:, blk * MOBA_BLOCK:(blk + 1) * MOBA_BLOCK]


def _in_proj(x, gain, w_nat, w_tr, cq, sq, ckt, skt):
    b, s, d = x.shape
    tm = IN_TOKENS
    ns = s // tm
    bpt = tm // MOBA_BLOCK
    cpt = tm // RET_CHUNK
    const = lambda si, bi: (0, 0)
    out_shape = (
        jax.ShapeDtypeStruct((b, s, RET_QK_WIDTH), BF16),
        jax.ShapeDtypeStruct((b, s, RET_WIDTH), BF16),
        jax.ShapeDtypeStruct((b, s, RET_WIDTH), BF16),
        jax.ShapeDtypeStruct((b, s, MOBA_WIDTH), BF16),
        jax.ShapeDtypeStruct((b, ns, bpt, MOBA_WIDTH), F32),
        jax.ShapeDtypeStruct((b, s // RET_CHUNK, RET_QK_WIDTH, RET_CHUNK), BF16),
        jax.ShapeDtypeStruct((b, s // MOBA_BLOCK, MOBA_WIDTH, MOBA_BLOCK), BF16),
        jax.ShapeDtypeStruct((b, s // MOBA_BLOCK, MOBA_WIDTH, MOBA_BLOCK), BF16),
    )
    tok = lambda w: pl.BlockSpec((1, tm, w), lambda si, bi: (bi, si, 0))
    return pl.pallas_call(
        _in_proj_kernel,
        grid=(ns, b),
        in_specs=[
            tok(d),
            pl.BlockSpec((1, d), const),
            pl.BlockSpec((d, NAT_WIDTH), const),
            pl.BlockSpec((TR_WIDTH, d), const),
            pl.BlockSpec((tm, RET_QK_WIDTH), lambda si, bi: (si, 0)),
            pl.BlockSpec((tm, RET_QK_WIDTH), lambda si, bi: (si, 0)),
            pl.BlockSpec((RET_QK_DIM // 2, tm), lambda si, bi: (0, si)),
            pl.BlockSpec((RET_QK_DIM // 2, tm), lambda si, bi: (0, si)),
        ],
        out_specs=(
            tok(RET_QK_WIDTH), tok(RET_WIDTH), tok(RET_WIDTH), tok(MOBA_WIDTH),
            pl.BlockSpec((1, 1, bpt, MOBA_WIDTH), lambda si, bi: (bi, si, 0, 0)),
            pl.BlockSpec((1, cpt, RET_QK_WIDTH, RET_CHUNK), lambda si, bi: (bi, si, 0, 0)),
            pl.BlockSpec((1, bpt, MOBA_WIDTH, MOBA_BLOCK), lambda si, bi: (bi, si, 0, 0)),
            pl.BlockSpec((1, bpt, MOBA_WIDTH, MOBA_BLOCK), lambda si, bi: (bi, si, 0, 0)),
        ),
        out_shape=out_shape,
        compiler_params=pltpu.CompilerParams(
            dimension_semantics=("arbitrary", "arbitrary"), vmem_limit_bytes=VMEM_LIMIT),
        name="in_proj",
    )(x, gain, w_nat, w_tr, cq, sq, ckt, skt)


def _retention_kernel(rq_ref, rkt_ref, rv_ref, sg_ref, dmat_ref, qd_ref, kd_ref, cd_ref,
                      out_ref, state_ref):
    @pl.when(pl.program_id(1) == 0)
    def _():
        state_ref[...] = jnp.zeros_like(state_ref)

    n_chunks = rq_ref.shape[1] // RET_CHUNK

    def chunk(c, carry):
        r0 = pl.multiple_of(c * RET_CHUNK, RET_CHUNK)
        rows = pl.ds(r0, RET_CHUNK)
        for hd in range(RET_HEADS):
            q = rq_ref[0, rows, hd * RET_QK_DIM:(hd + 1) * RET_QK_DIM]
            kt = rkt_ref[0, c, hd * RET_QK_DIM:(hd + 1) * RET_QK_DIM, :]
            v = rv_ref[0, rows, hd * RET_V_DIM:(hd + 1) * RET_V_DIM]
            state = state_ref[hd]
            scores = _dot(q, kt) * dmat_ref[hd]
            o = _dot(scores.astype(BF16), v) + qd_ref[hd] * _dot(q, state.astype(BF16))
            kts = (kt.astype(F32) * kd_ref[hd]).astype(BF16)
            state_ref[hd] = state * cd_ref[hd] + _dot(kts, v)
            ms = jnp.mean(o * o, axis=-1, keepdims=True)
            gate = sg_ref[0, rows, hd * RET_V_DIM:(hd + 1) * RET_V_DIM].astype(F32)
            out_ref[0, rows, hd * RET_V_DIM:(hd + 1) * RET_V_DIM] = (
                o * lax.rsqrt(ms + EPS) * gate).astype(BF16)
        return carry

    lax.fori_loop(0, n_chunks, chunk, 0, unroll=8)


def _retention(rq, rkt, rv, sg, dmat, qd, kd, cd):
    b, s, _ = rq.shape
    tc = min(RET_TOKENS, s)
    tok = lambda w: pl.BlockSpec((1, tc, w), lambda bi, si: (bi, si, 0))
    tab = lambda a: pl.BlockSpec(a.shape, lambda bi, si: (0,) * a.ndim)
    return pl.pallas_call(
        _retention_kernel,
        grid=(b, s // tc),
        in_specs=[
            tok(RET_QK_WIDTH),
            pl.BlockSpec((1, tc // RET_CHUNK, RET_QK_WIDTH, RET_CHUNK), lambda bi, si: (bi, si, 0, 0)),
            tok(RET_WIDTH), tok(RET_WIDTH),
            tab(dmat), tab(qd), tab(kd), tab(cd),
        ],
        out_specs=tok(RET_WIDTH),
        out_shape=jax.ShapeDtypeStruct((b, s, RET_WIDTH), BF16),
        scratch_shapes=[pltpu.VMEM((RET_HEADS, RET_QK_DIM, RET_V_DIM), F32)],
        compiler_params=pltpu.CompilerParams(
            dimension_semantics=("arbitrary", "arbitrary"), vmem_limit_bytes=VMEM_LIMIT),
        name="retention",
    )(rq, rkt, rv, sg, dmat, qd, kd, cd)


ONES_ROWS = 16
PAIR_WIDTH = 2 * MOBA_HEAD_DIM
GATE_BLOCKS = 4


def _moba_gate_kernel(cfar_ref, qt_ref, km_ref, selfar_ref, selprev_ref):
    d = MOBA_HEAD_DIM
    nb = km_ref.shape[2]
    n_q = qt_ref.shape[1]
    shape = (nb, n_q * MOBA_BLOCK)
    blk = lax.broadcasted_iota(jnp.int32, shape, 0)
    i = pl.program_id(1) * n_q + lax.broadcasted_iota(jnp.int32, shape, 1) // MOBA_BLOCK
    blk_f = blk.astype(F32)
    for h in range(MOBA_HEADS):
        qt = jnp.concatenate([qt_ref[0, c, h * d:(h + 1) * d, :] for c in range(n_q)], axis=1)
        km = km_ref[0, h]
        km_hi = km.astype(BF16)
        km_lo = (km - km_hi.astype(F32)).astype(BF16)
        gate = _dot(km_hi, qt) + _dot(km_lo, qt)
        gate = jnp.where(blk < i, gate, NEG)
        chosen = jnp.zeros(gate.shape, F32)
        for _ in range(MOBA_TOPK):
            best = jnp.max(gate, axis=0, keepdims=True)
            idx = jnp.min(jnp.where(gate == best, blk_f, float(nb)), axis=0, keepdims=True)
            hit = blk_f == idx
            chosen = jnp.where(hit & (best > 0.5 * NEG), 1.0, chosen)
            gate = jnp.where(hit, NEG, gate)
        picked = chosen > 0.5
        selfar_ref[0, h] = jnp.where(picked & (blk < i - 1), cfar_ref[h], NEG).astype(BF16)
        prev_hit = jnp.max(jnp.where(picked & (blk == i - 1), 1.0, 0.0), axis=0, keepdims=True)
        selprev_ref[0, h] = jnp.where(prev_hit > 0.5, 0.0, NEG)


def _moba_gate(cfar, mqt, kmean):
    b, nb, _, _ = mqt.shape
    s = nb * MOBA_BLOCK
    n_q = math.gcd(GATE_BLOCKS, nb)
    width = n_q * MOBA_BLOCK
    return pl.pallas_call(
        _moba_gate_kernel,
        grid=(b, nb // n_q),
        in_specs=[
            pl.BlockSpec(memory_space=pltpu.SMEM),
            pl.BlockSpec((1, n_q, MOBA_WIDTH, MOBA_BLOCK), lambda bi, i: (bi, i, 0, 0)),
            pl.BlockSpec((1, MOBA_HEADS, nb, MOBA_HEAD_DIM), lambda bi, i: (bi, 0, 0, 0)),
        ],
        out_specs=(
            pl.BlockSpec((1, MOBA_HEADS, nb, width), lambda bi, i: (bi, 0, 0, i)),
            pl.BlockSpec((1, MOBA_HEADS, 1, width), lambda bi, i: (bi, 0, 0, i)),
        ),
        out_shape=(
            jax.ShapeDtypeStruct((b, MOBA_HEADS, nb, s), BF16),
            jax.ShapeDtypeStruct((b, MOBA_HEADS, 1, s), F32),
        ),
        compiler_params=pltpu.CompilerParams(
            dimension_semantics=("arbitrary", "arbitrary"), vmem_limit_bytes=VMEM_LIMIT),
        name="moba_gate",
    )(cfar, mqt, kmean)


def _moba_kernel(qt_ref, k_ref, vt_ref, selfar_ref, selprev_ref, tnear_ref,
                 out_ref, kaug_ref, vaug_ref, qaug_ref, qnear_ref, s_ref, m_ref, acc_ref):
    i = pl.program_id(1)
    d = MOBA_HEAD_DIM
    nb = vaug_ref.shape[1]
    s_len = kaug_ref.shape[1]
    blk_rows = MOBA_BLOCK
    n_pairs = MOBA_HEADS // 2

    @pl.when((pl.program_id(0) == 0) & (i == 0))
    def _():
        row_blk = lax.broadcasted_iota(jnp.int32, (s_len, 2 * PAIR_WIDTH), 0) // MOBA_BLOCK
        lane = lax.broadcasted_iota(jnp.int32, (s_len, 2 * PAIR_WIDTH), 1)
        hot_lane = jnp.where(lane < PAIR_WIDTH, lane - d, lane - PAIR_WIDTH)
        in_band = (lane >= d) & (lane < PAIR_WIDTH + d)
        pattern = jnp.where(in_band & (row_blk == hot_lane), 1.0, 0.0).astype(BF16)
        for pe in range(n_pairs):
            kaug_ref[pe] = pattern
        for e in range(MOBA_HEADS):
            vaug_ref[e, :, d:, :] = jnp.ones((nb, ONES_ROWS, MOBA_BLOCK), BF16)

    own_rows = pl.ds(pl.multiple_of(i * blk_rows, blk_rows), blk_rows)
    for pe in range(n_pairs):
        kaug_ref[pe, own_rows, 0:d] = k_ref[0, :, pe * PAIR_WIDTH:pe * PAIR_WIDTH + d]
        kaug_ref[pe, own_rows, PAIR_WIDTH + d:] = k_ref[0, :, pe * PAIR_WIDTH + d:(pe + 1) * PAIR_WIDTH]
    for e in range(MOBA_HEADS):
        vaug_ref[e, i, 0:d, :] = vt_ref[0, 0, e * d:(e + 1) * d, :]

    prow = lax.broadcasted_iota(jnp.int32, (PAIR_WIDTH, MOBA_BLOCK), 0)
    for e in range(MOBA_HEADS):
        pe, he = divmod(e, 2)
        qt2 = qt_ref[0, 0, pe * PAIR_WIDTH:(pe + 1) * PAIR_WIDTH, :]
        q_only = jnp.where((prow >= he * d) & (prow < (he + 1) * d), qt2, jnp.zeros_like(qt2))
        qnear_ref[e] = q_only
        qaug_ref[e] = q_only
        hot_row = d if he == 0 else 0
        qaug_ref[e, hot_row:hot_row + nb, :] = selfar_ref[0, e]

    def qk(e, j, slot, near=False):
        rows = pl.ds(pl.multiple_of(j * blk_rows, blk_rows), blk_rows)
        keys = kaug_ref[e // 2, rows, (e % 2) * PAIR_WIDTH:(e % 2 + 1) * PAIR_WIDTH]
        s_ref[slot, e] = _dot(keys, qnear_ref[e] if near else qaug_ref[e])

    def sm(e, j, slot, one_sweep, bias=None, first=False):
        st = s_ref[slot, e]
        if bias is not None:
            st = st + bias
        if first:
            mj = jnp.max(st, axis=0, keepdims=True)
            m_ref[e] = mj
            acc_ref[e] = _dot(vaug_ref[e, j], jnp.exp2(st - mj).astype(BF16))
        elif one_sweep:
            m_old = m_ref[e]
            p = jnp.exp2(st - m_old).astype(BF16)
            grow = jnp.maximum(jnp.max(p, axis=0, keepdims=True).astype(F32), 1.0)
            acc_ref[e] = (acc_ref[e] + _dot(vaug_ref[e, j], p)) / grow
            m_ref[e] = m_old + jnp.log2(grow)
        else:
            m_old = m_ref[e]
            m_new = jnp.maximum(m_old, jnp.max(st, axis=0, keepdims=True))
            pv = _dot(vaug_ref[e, j], jnp.exp2(st - m_new).astype(BF16))
            acc_ref[e] = acc_ref[e] * jnp.exp2(m_old - m_new) + pv
            m_ref[e] = m_new

    clamp = lambda j: jnp.minimum(j, i)
    heads = range(MOBA_HEADS)
    jp = jnp.maximum(i - 1, 0)
    n_full = lax.shift_right_logical(i, 2)
    has_tail = (i & 3) >= 2

    def attend(one_sweep):
        for blk, slot, near in ((i, 0, True), (jp, 1, True), (clamp(0), 2, False), (clamp(1), 3, False)):
            for e in heads:
                qk(e, blk, slot, near=near)
        for e in heads:
            sm(e, i, 0, one_sweep, bias=tnear_ref[e, blk_rows:, :], first=True)
        for e in heads:
            sm(e, jp, 1, one_sweep, bias=tnear_ref[e, 0:blk_rows, :] + selprev_ref[0, e])

        def far(t, carry):
            j = 4 * t
            for half in (0, 2):
                for e in heads:
                    qk(e, clamp(j + half + 2), half)
                for e in heads:
                    qk(e, clamp(j + half + 3), half + 1)
                for e in heads:
                    sm(e, clamp(j + half), (half + 2) % 4, one_sweep)
                for e in heads:
                    sm(e, clamp(j + half + 1), (half + 3) % 4, one_sweep)
            return carry

        lax.fori_loop(0, n_full, far, 0)

        @pl.when(has_tail)
        def _():
            for e in heads:
                sm(e, clamp(4 * n_full), 2, one_sweep)
            for e in heads:
                sm(e, clamp(4 * n_full + 1), 3, one_sweep)

    attend(one_sweep=True)
    check = jnp.sum(acc_ref[...]) + jnp.sum(m_ref[...])
    pl.when(jnp.logical_not(jnp.abs(check) < jnp.inf))(lambda: attend(one_sweep=False))

    outs = []
    for e in range(MOBA_HEADS):
        acc = acc_ref[e]
        outs.append((acc[:d] / acc[d:d + 1]).T)
    out_ref[0] = jnp.concatenate(outs, axis=1).astype(BF16)


def _moba(mqt, mk, mvt, selfar, selprev, tnear):
    b, s, _ = mk.shape
    nb = s // MOBA_BLOCK
    assert nb <= MOBA_HEAD_DIM
    nh = MOBA_HEADS
    return pl.pallas_call(
        _moba_kernel,
        grid=(b, nb),
        in_specs=[
            pl.BlockSpec((1, 1, MOBA_WIDTH, MOBA_BLOCK), lambda bi, i: (bi, i, 0, 0)),
            pl.BlockSpec((1, MOBA_BLOCK, MOBA_WIDTH), lambda bi, i: (bi, i, 0)),
            pl.BlockSpec((1, 1, MOBA_WIDTH, MOBA_BLOCK), lambda bi, i: (bi, i, 0, 0)),
            pl.BlockSpec((1, nh, nb, MOBA_BLOCK), lambda bi, i: (bi, 0, 0, i)),
            pl.BlockSpec((1, nh, 1, MOBA_BLOCK), lambda bi, i: (bi, 0, 0, i)),
            pl.BlockSpec((nh, 2 * MOBA_BLOCK, MOBA_BLOCK), lambda bi, i: (0, 0, 0),
                         pipeline_mode=pl.Buffered(1)),
        ],
        out_specs=pl.BlockSpec((1, MOBA_BLOCK, MOBA_WIDTH), lambda bi, i: (bi, i, 0)),
        out_shape=jax.ShapeDtypeStruct((b, s, MOBA_WIDTH), BF16),
        scratch_shapes=[
            pltpu.VMEM((nh // 2, s, 2 * PAIR_WIDTH), BF16),
            pltpu.VMEM((nh, nb, MOBA_HEAD_DIM + ONES_ROWS, MOBA_BLOCK), BF16),
            pltpu.VMEM((nh, PAIR_WIDTH, MOBA_BLOCK), BF16),
            pltpu.VMEM((nh, PAIR_WIDTH, MOBA_BLOCK), BF16),
            pltpu.VMEM((4, nh, MOBA_BLOCK, MOBA_BLOCK), F32),
            pltpu.VMEM((nh, 1, MOBA_BLOCK), F32),
            pltpu.VMEM((nh, MOBA_HEAD_DIM + ONES_ROWS, MOBA_BLOCK), F32),
        ],
        compiler_params=pltpu.CompilerParams(
            dimension_semantics=("arbitrary", "arbitrary"), vmem_limit_bytes=VMEM_LIMIT),
        name="moba",
    )(mqt, mk, mvt, selfar, selprev, tnear)


def _rms(x, g):
    return x * lax.rsqrt(jnp.mean(x * x, axis=-1, keepdims=True) + EPS) * g


def _out_ffn_kernel(x_ref, ret_ref, moba_ref, wo_ref, wg_ref, wu_ref, wd_ref,
                    g_post_mix_ref, g_pre_ffn_ref, g_post_ffn_ref, out_ref):
    tm = x_ref.shape[0]
    groups = [slice(r0, r0 + tm // FFN_ROW_GROUPS) for r0 in range(0, tm, tm // FFN_ROW_GROUPS)]
    mixes = [_dot(jnp.concatenate([ret_ref[rows, :], moba_ref[rows, :]], axis=1), wo_ref[...])
             for rows in groups]
    for rows, mix in zip(groups, mixes):
        x1 = x_ref[rows, :] + _rms(mix, g_post_mix_ref[...])
        h = _rms(x1, g_pre_ffn_ref[...]).astype(BF16)
        f = None
        for lo, hi in FFN_CHUNKS:
            cols = slice(lo, hi)
            gate = _dot(h, wg_ref[:, cols])
            up = _dot(h, wu_ref[:, cols])
            act = (gate * jax.nn.sigmoid(gate) * up).astype(BF16)
            part = _dot(act, wd_ref[cols, :])
            f = part if f is None else f + part
        out_ref[rows, :] = x1 + _rms(f, g_post_ffn_ref[...])


def _out_ffn(x2, ret2, moba2, wo, wg, wu, wd, g_post_mix, g_pre_ffn, g_post_ffn):
    n, d = x2.shape
    tm = min(FFN_TOKENS, n)
    tok = lambda w: pl.BlockSpec((tm, w), lambda t: (t, 0))
    resident = lambda a: pl.BlockSpec(a.shape, lambda t: (0, 0), pipeline_mode=pl.Buffered(1))
    return pl.pallas_call(
        _out_ffn_kernel,
        grid=(n // tm,),
        in_specs=[tok(d), tok(RET_WIDTH), tok(MOBA_WIDTH),
                  resident(wo), resident(wg), resident(wu), resident(wd),
                  resident(g_post_mix), resident(g_pre_ffn), resident(g_post_ffn)],
        out_specs=tok(d),
        out_shape=jax.ShapeDtypeStruct((n, d), F32),
        compiler_params=pltpu.CompilerParams(
            dimension_semantics=("arbitrary",), vmem_limit_bytes=VMEM_LIMIT),
        name="out_ffn",
    )(x2, ret2, moba2, wo, wg, wu, wd, g_post_mix, g_pre_ffn, g_post_ffn)


def _rotary_tables(s):
    half = RET_QK_DIM // 2
    inv_freq = ROPE_BASE ** (-np.arange(half, dtype=np.float64) / half)
    ang = np.arange(s, dtype=np.float64)[:, None] * inv_freq[None, :]
    cos, sin = np.cos(ang), np.sin(ang)
    cq = np.tile(np.concatenate([cos, cos], axis=1), (1, RET_HEADS))
    sq = np.tile(np.concatenate([-sin, sin], axis=1), (1, RET_HEADS))
    return tuple(np.ascontiguousarray(t, dtype=np.float32) for t in (cq, sq, cos.T, sin.T))


def _retention_tables():
    c = RET_CHUNK
    log_gamma = np.log1p(-np.exp(np.linspace(math.log(1.0 / 32), math.log(1.0 / 512), RET_HEADS)))
    idx = np.arange(c, dtype=np.float64)
    diff = idx[:, None] - idx[None, :]
    dmat = np.where(diff >= 0, np.exp(np.maximum(diff, 0.0)[None] * log_gamma[:, None, None]), 0.0)
    q_decay = np.exp((idx + 1.0)[None, :] * log_gamma[:, None])
    k_decay = np.exp((c - 1.0 - idx)[None, :] * log_gamma[:, None])
    chunk_decay = np.exp(c * log_gamma)
    qd = np.broadcast_to(q_decay[:, :, None], (RET_HEADS, c, RET_V_DIM))
    kd = k_decay[:, None, :]
    cd = np.broadcast_to(chunk_decay[:, None, None], (RET_HEADS, 1, RET_V_DIM))
    return tuple(np.ascontiguousarray(t, dtype=np.float32) for t in (dmat, qd, kd, cd))


def _t5_bucket(rel):
    n = np.maximum(rel, 0)
    max_exact = REL_BUCKETS // 2
    large = max_exact + (np.log(np.maximum(n, 1) / max_exact) / math.log(REL_MAX_DIST / max_exact)
                         * (REL_BUCKETS - max_exact)).astype(np.int64)
    return np.where(n < max_exact, n, np.minimum(large, REL_BUCKETS - 1))


def _toeplitz(vec):
    h, two_n = vec.shape
    n = two_n // 2
    ext = jnp.concatenate([vec, jnp.zeros((h, 1), vec.dtype)], axis=1)
    skew = jnp.tile(ext, (1, n))[:, :n * two_n].reshape(h, n, two_n)
    return skew[:, :, n:]


def _bias_tables(rel_bias):
    bias_t = rel_bias.T.astype(F32) * LOG2E
    n = MOBA_BLOCK
    one_hot = (_t5_bucket(np.arange(2 * n))[:, None] == np.arange(REL_BUCKETS)).astype(np.float32)
    by_dist = jnp.einsum("rn,hn->hr", one_hot, bias_t, precision=lax.Precision.HIGHEST)
    tprev = _toeplitz(by_dist)
    own_vec = jnp.concatenate([jnp.full((bias_t.shape[0], n), NEG, F32), by_dist[:, :n]], axis=1)
    town = _toeplitz(own_vec)
    cfar = bias_t[:, REL_BUCKETS - 1]
    return jnp.concatenate([tprev, town], axis=1), cfar


def kernel(x, w_in, w_out, pre_mix_norm, post_mix_norm, pre_ffn_norm, post_ffn_norm,
           rel_bias, w_gate, w_up, w_down):
    b, s, d = x.shape
    assert d == D_MODEL and s % IN_TOKENS == 0 and MOBA_BLOCK + 1 >= REL_MAX_DIST
    depth = w_in.shape[0]

    cq, sq, ckt, skt = _rotary_tables(s)
    dmat, qd, kd, cd = _retention_tables()
    tnear, cfar = _bias_tables(rel_bias)

    sizes = [RET_QK_WIDTH, RET_QK_WIDTH, RET_WIDTH, RET_WIDTH, MOBA_WIDTH, MOBA_WIDTH, MOBA_WIDTH]
    o_rq, o_rk, o_rv, o_rg, o_mq, o_mk, o_mv, _ = np.cumsum([0] + sizes).tolist()

    for layer in range(depth):
        w = w_in[layer]
        w_nat = jnp.concatenate(
            [w[:, o_rq:o_rk], w[:, o_rv:o_mq], w[:, o_mk:o_mv]], axis=1).astype(BF16)
        w_tr = jnp.concatenate([w[:, o_rk:o_rv], w[:, o_mq:o_mk], w[:, o_mv:]], axis=1).T.astype(BF16)

        rq, rv, sg, mk, kmean, rkt, mqt, mvt = _in_proj(
            x, pre_mix_norm[layer][None, :], w_nat, w_tr, cq, sq, ckt, skt)

        ret = _retention(rq, rkt, rv, sg, dmat, qd, kd, cd)

        nb = s // MOBA_BLOCK
        kmean = kmean.reshape(b, nb, MOBA_HEADS, MOBA_HEAD_DIM).transpose(0, 2, 1, 3)
        selfar, selprev = _moba_gate(cfar, mqt, kmean)
        moba = _moba(mqt, mk, mvt, selfar, selprev, tnear)

        x = _out_ffn(
            x.reshape(b * s, d), ret.reshape(b * s, RET_WIDTH), moba.reshape(b * s, MOBA_WIDTH),
            w_out[layer].astype(BF16), w_gate[layer].astype(BF16), w_up[layer].astype(BF16),
            w_down[layer].astype(BF16), post_mix_norm[layer][None, :], pre_ffn_norm[layer][None, :],
            post_ffn_norm[layer][None, :]).reshape(b, s, d)
    return x
```

```python
import functools
import math

import jax
import jax.numpy as jnp
import numpy as np
from jax import lax
from jax.experimental import pallas as pl
from jax.experimental.pallas import tpu as pltpu

F32 = jnp.float32
BF16 = jnp.bfloat16

D_MODEL = 1024
RET_HEADS = 4
RET_QK_DIM = 64
RET_V_DIM = 128
RET_CHUNK = 128
RET_QK_WIDTH = RET_HEADS * RET_QK_DIM
RET_WIDTH = RET_HEADS * RET_V_DIM
MOBA_HEADS = 8
MOBA_HEAD_DIM = 64
MOBA_WIDTH = MOBA_HEADS * MOBA_HEAD_DIM
MOBA_BLOCK = 256
MOBA_TOPK = 3
REL_BUCKETS = 32
REL_MAX_DIST = 128
D_FF = 2816
EPS = 1e-6
ROPE_BASE = 10000.0

NEG = -1e30
LOG2E = math.log2(math.e)

IN_TOKENS = 1024
RET_TOKENS = 1024
FFN_TOKENS = 1024
FFN_ROW_GROUPS = 4
MXU_TILE = 256
FFN_CHUNKS = ((0, 6 * MXU_TILE), (6 * MXU_TILE, D_FF))
VMEM_LIMIT = 56 * 1024 * 1024

NAT_WIDTH = RET_QK_WIDTH + 2 * RET_WIDTH + MOBA_WIDTH
TR_WIDTH = RET_QK_WIDTH + 2 * MOBA_WIDTH


def _nt_dot(a, b):
    return lax.dot_general(a, b, (((1,), (1,)), ((), ())), preferred_element_type=F32)


def _dot(a, b):
    return jnp.dot(a, b, preferred_element_type=F32)


def _in_proj_kernel(x_ref, g_ref, wn_ref, wt_ref, cq_ref, sq_ref, ckt_ref, skt_ref,
                    rq_ref, rv_ref, sg_ref, mk_ref, kmean_ref, rkt_ref, mqt_ref, mvt_ref):
    tm = x_ref.shape[1]
    x = x_ref[0]
    ms = jnp.mean(x * x, axis=-1, keepdims=True)
    h = (x * lax.rsqrt(ms + EPS) * g_ref[...]).astype(BF16)

    pn = _dot(h, wn_ref[...])
    pt = _nt_dot(wt_ref[...], h)

    rq = pn[:, :RET_QK_WIDTH]
    lane = lax.broadcasted_iota(jnp.int32, rq.shape, 1)
    first_half = (lane % RET_QK_DIM) < (RET_QK_DIM // 2)
    partner = jnp.where(first_half,
                        pltpu.roll(rq, RET_QK_WIDTH - RET_QK_DIM // 2, 1),
                        pltpu.roll(rq, RET_QK_DIM // 2, 1))
    rq_ref[0] = (rq * cq_ref[...] + partner * sq_ref[...]).astype(BF16)

    rv_ref[0] = pn[:, RET_QK_WIDTH:RET_QK_WIDTH + RET_WIDTH].astype(BF16)
    rg = pn[:, RET_QK_WIDTH + RET_WIDTH:RET_QK_WIDTH + 2 * RET_WIDTH]
    sg_ref[0] = (rg * jax.nn.sigmoid(rg)).astype(BF16)

    mk = pn[:, RET_QK_WIDTH + 2 * RET_WIDTH:]
    mk_ref[0] = mk.astype(BF16)
    for blk in range(tm // MOBA_BLOCK):
        kmean_ref[0, 0, blk:blk + 1, :] = jnp.mean(
            mk[blk * MOBA_BLOCK:(blk + 1) * MOBA_BLOCK], axis=0, keepdims=True)

    half = RET_QK_DIM // 2
    cos_t = ckt_ref[...]
    sin_t = skt_ref[...]
    parts = []
    for hd in range(RET_HEADS):
        x1 = pt[hd * RET_QK_DIM:hd * RET_QK_DIM + half]
        x2 = pt[hd * RET_QK_DIM + half:(hd + 1) * RET_QK_DIM]
        parts.append(x1 * cos_t - x2 * sin_t)
        parts.append(x2 * cos_t + x1 * sin_t)
    rkt = (jnp.concatenate(parts, axis=0) * (RET_QK_DIM ** -0.5)).astype(BF16)
    for c in range(tm // RET_CHUNK):
        rkt_ref[0, c] = rkt[:, c * RET_CHUNK:(c + 1) * RET_CHUNK]

    mqt = (pt[RET_QK_WIDTH:RET_QK_WIDTH + MOBA_WIDTH] * (MOBA_HEAD_DIM ** -0.5 * LOG2E)).astype(BF16)
    mvt = pt[RET_QK_WIDTH + MOBA_WIDTH:].astype(BF16)
    for blk in range(tm // MOBA_BLOCK):
        mqt_ref[0, blk] = mqt[:, blk * MOBA_BLOCK:(blk + 1) * MOBA_BLOCK]
        mvt_ref[0, blk] = mvt[:, blk * MOBA_BLOCK:(blk + 1) * MOBA_BLOCK]


def _in_proj(x, gain, w_nat, w_tr, cq, sq, ckt, skt):
    b, s, d = x.shape
    tm = IN_TOKENS
    ns = s // tm
    bpt = tm // MOBA_BLOCK
    cpt = tm // RET_CHUNK
    const = lambda si, bi: (0, 0)
    out_shape = (
        jax.ShapeDtypeStruct((b, s, RET_QK_WIDTH), BF16),
        jax.ShapeDtypeStruct((b, s, RET_WIDTH), BF16),
        jax.ShapeDtypeStruct((b, s, RET_WIDTH), BF16),
        jax.ShapeDtypeStruct((b, s, MOBA_WIDTH), BF16),
        jax.ShapeDtypeStruct((b, ns, bpt, MOBA_WIDTH), F32),
        jax.ShapeDtypeStruct((b, s // RET_CHUNK, RET_QK_WIDTH, RET_CHUNK), BF16),
        jax.ShapeDtypeStruct((b, s // MOBA_BLOCK, MOBA_WIDTH, MOBA_BLOCK), BF16),
        jax.ShapeDtypeStruct((b, s // MOBA_BLOCK, MOBA_WIDTH, MOBA_BLOCK), BF16),
    )
    tok = lambda w: pl.BlockSpec((1, tm, w), lambda si, bi: (bi, si, 0))
    return pl.pallas_call(
        _in_proj_kernel,
        grid=(ns, b),
        in_specs=[
            tok(d),
            pl.BlockSpec((1, d), const),
            pl.BlockSpec((d, NAT_WIDTH), const),
            pl.BlockSpec((TR_WIDTH, d), const),
            pl.BlockSpec((tm, RET_QK_WIDTH), lambda si, bi: (si, 0)),
            pl.BlockSpec((tm, RET_QK_WIDTH), lambda si, bi: (si, 0)),
            pl.BlockSpec((RET_QK_DIM // 2, tm), lambda si, bi: (0, si)),
            pl.BlockSpec((RET_QK_DIM // 2, tm), lambda si, bi: (0, si)),
        ],
        out_specs=(
            tok(RET_QK_WIDTH), tok(RET_WIDTH), tok(RET_WIDTH), tok(MOBA_WIDTH),
            pl.BlockSpec((1, 1, bpt, MOBA_WIDTH), lambda si, bi: (bi, si, 0, 0)),
            pl.BlockSpec((1, cpt, RET_QK_WIDTH, RET_CHUNK), lambda si, bi: (bi, si, 0, 0)),
            pl.BlockSpec((1, bpt, MOBA_WIDTH, MOBA_BLOCK), lambda si, bi: (bi, si, 0, 0)),
            pl.BlockSpec((1, bpt, MOBA_WIDTH, MOBA_BLOCK), lambda si, bi: (bi, si, 0, 0)),
        ),
        out_shape=out_shape,
        compiler_params=pltpu.CompilerParams(
            dimension_semantics=("arbitrary", "arbitrary"), vmem_limit_bytes=VMEM_LIMIT),
        name="in_proj",
    )(x, gain, w_nat, w_tr, cq, sq, ckt, skt)


def _retention_kernel(rq_ref, rkt_ref, rv_ref, sg_ref, dmat_ref, qd_ref, kd_ref, cd_ref,
                      out_ref, state_ref):
    @pl.when(pl.program_id(1) == 0)
    def _():
        state_ref[...] = jnp.zeros_like(state_ref)

    n_chunks = rq_ref.shape[1] // RET_CHUNK

    def chunk(c, carry):
        r0 = pl.multiple_of(c * RET_CHUNK, RET_CHUNK)
        rows = pl.ds(r0, RET_CHUNK)
        for hd in range(RET_HEADS):
            q = rq_ref[0, rows, hd * RET_QK_DIM:(hd + 1) * RET_QK_DIM]
            kt = rkt_ref[0, c, hd * RET_QK_DIM:(hd + 1) * RET_QK_DIM, :]
            v = rv_ref[0, rows, hd * RET_V_DIM:(hd + 1) * RET_V_DIM]
            state = state_ref[hd]
            scores = _dot(q, kt) * dmat_ref[hd]
            o = _dot(scores.astype(BF16), v) + qd_ref[hd] * _dot(q, state.astype(BF16))
            kts = (kt.astype(F32) * kd_ref[hd]).astype(BF16)
            state_ref[hd] = state * cd_ref[hd] + _dot(kts, v)
            ms = jnp.mean(o * o, axis=-1, keepdims=True)
            gate = sg_ref[0, rows, hd * RET_V_DIM:(hd + 1) * RET_V_DIM].astype(F32)
            out_ref[0, rows, hd * RET_V_DIM:(hd + 1) * RET_V_DIM] = (
                o * lax.rsqrt(ms + EPS) * gate).astype(BF16)
        return carry

    lax.fori_loop(0, n_chunks, chunk, 0, unroll=8)


def _retention(rq, rkt, rv, sg, dmat, qd, kd, cd):
    b, s, _ = rq.shape
    tc = min(RET_TOKENS, s)
    tok = lambda w: pl.BlockSpec((1, tc, w), lambda bi, si: (bi, si, 0))
    tab = lambda a: pl.BlockSpec(a.shape, lambda bi, si: (0,) * a.ndim)
    return pl.pallas_call(
        _retention_kernel,
        grid=(b, s // tc),
        in_specs=[
            tok(RET_QK_WIDTH),
            pl.BlockSpec((1, tc // RET_CHUNK, RET_QK_WIDTH, RET_CHUNK), lambda bi, si: (bi, si, 0, 0)),
            tok(RET_WIDTH), tok(RET_WIDTH),
            tab(dmat), tab(qd), tab(kd), tab(cd),
        ],
        out_specs=tok(RET_WIDTH),
        out_shape=jax.ShapeDtypeStruct((b, s, RET_WIDTH), BF16),
        scratch_shapes=[pltpu.VMEM((RET_HEADS, RET_QK_DIM, RET_V_DIM), F32)],
        compiler_params=pltpu.CompilerParams(
            dimension_semantics=("arbitrary", "arbitrary"), vmem_limit_bytes=VMEM_LIMIT),
        name="retention",
    )(rq, rkt, rv, sg, dmat, qd, kd, cd)


ONES_ROWS = 16
PAIR_WIDTH = 2 * MOBA_HEAD_DIM


def _block_choice(qt, km, own, cfar):
    nb = km.shape[0]
    km_hi = km.astype(BF16)
    km_lo = (km - km_hi.astype(F32)).astype(BF16)
    gate = _dot(km_hi, qt) + _dot(km_lo, qt)
    blk = lax.broadcasted_iota(jnp.int32, gate.shape, 0)
    blk_f = blk.astype(F32)
    gate = jnp.where(blk < own, gate, NEG)
    chosen = jnp.zeros(gate.shape, F32)
    for _ in range(MOBA_TOPK):
        best = jnp.max(gate, axis=0, keepdims=True)
        idx = jnp.min(jnp.where(gate == best, blk_f, float(nb)), axis=0, keepdims=True)
        hit = blk_f == idx
        chosen = jnp.where(hit & (best > 0.5 * NEG), 1.0, chosen)
        gate = jnp.where(hit, NEG, gate)
    picked = chosen > 0.5
    selfar = jnp.where(picked & (blk < own - 1), cfar, NEG).astype(BF16)
    prev_hit = jnp.max(jnp.where(picked & (blk == own - 1), 1.0, 0.0), axis=0, keepdims=True)
    return selfar, jnp.where(prev_hit > 0.5, 0.0, NEG)


def _moba_kernel(cfar_ref, qt_ref, qt_next_ref, k_ref, vt_ref, km_ref, tnear_ref,
                 out_ref, kaug_ref, vaug_ref, qaug_ref, qnear_ref, selfar_ref, selprev_ref,
                 selprev_next_ref, s_ref, m_ref, acc_ref):
    i = pl.program_id(1)
    d = MOBA_HEAD_DIM
    nb = vaug_ref.shape[1]
    s_len = kaug_ref.shape[1]
    blk_rows = MOBA_BLOCK
    n_pairs = MOBA_HEADS // 2

    @pl.when((pl.program_id(0) == 0) & (i == 0))
    def _():
        row_blk = lax.broadcasted_iota(jnp.int32, (s_len, 2 * PAIR_WIDTH), 0) // MOBA_BLOCK
        lane = lax.broadcasted_iota(jnp.int32, (s_len, 2 * PAIR_WIDTH), 1)
        hot_lane = jnp.where(lane < PAIR_WIDTH, lane - d, lane - PAIR_WIDTH)
        in_band = (lane >= d) & (lane < PAIR_WIDTH + d)
        pattern = jnp.where(in_band & (row_blk == hot_lane), 1.0, 0.0).astype(BF16)
        for pe in range(n_pairs):
            kaug_ref[pe] = pattern
        for e in range(MOBA_HEADS):
            vaug_ref[e, :, d:, :] = jnp.ones((nb, ONES_ROWS, MOBA_BLOCK), BF16)

    own_rows = pl.ds(pl.multiple_of(i * blk_rows, blk_rows), blk_rows)
    for pe in range(n_pairs):
        kaug_ref[pe, own_rows, 0:d] = k_ref[0, :, pe * PAIR_WIDTH:pe * PAIR_WIDTH + d]
        kaug_ref[pe, own_rows, PAIR_WIDTH + d:] = k_ref[0, :, pe * PAIR_WIDTH + d:(pe + 1) * PAIR_WIDTH]
    for e in range(MOBA_HEADS):
        vaug_ref[e, i, 0:d, :] = vt_ref[0, 0, e * d:(e + 1) * d, :]

    @pl.when(i == 0)
    def _():
        selfar_ref[...] = jnp.full(selfar_ref.shape, NEG, BF16)
        selprev_next_ref[...] = jnp.full(selprev_next_ref.shape, NEG, F32)

    prow = lax.broadcasted_iota(jnp.int32, (PAIR_WIDTH, MOBA_BLOCK), 0)
    for e in range(MOBA_HEADS):
        pe, he = divmod(e, 2)
        qt2 = qt_ref[0, 0, pe * PAIR_WIDTH:(pe + 1) * PAIR_WIDTH, :]
        q_only = jnp.where((prow >= he * d) & (prow < (he + 1) * d), qt2, jnp.zeros_like(qt2))
        qnear_ref[e] = q_only
        qaug_ref[e] = q_only
        hot_row = d if he == 0 else 0
        qaug_ref[e, hot_row:hot_row + nb, :] = selfar_ref[e]
        selprev_ref[e] = selprev_next_ref[e]

    for e in range(MOBA_HEADS):
        selfar_ref[e], selprev_next_ref[e] = _block_choice(
            qt_next_ref[0, 0, e * d:(e + 1) * d, :], km_ref[0, e], i + 1, cfar_ref[e])

    def qk(e, j, slot, near=False):
        rows = pl.ds(pl.multiple_of(j * blk_rows, blk_rows), blk_rows)
        keys = kaug_ref[e // 2, rows, (e % 2) * PAIR_WIDTH:(e % 2 + 1) * PAIR_WIDTH]
        s_ref[slot, e] = _dot(keys, qnear_ref[e] if near else qaug_ref[e])

    def sm(e, j, slot, one_sweep, bias=None, first=False):
        st = s_ref[slot, e]
        if bias is not None:
            st = st + bias
        if first:
            mj = jnp.max(st, axis=0, keepdims=True)
            m_ref[e] = mj
            acc_ref[e] = _dot(vaug_ref[e, j], jnp.exp2(st - mj).astype(BF16))
        elif one_sweep:
            m_old = m_ref[e]
            p = jnp.exp2(st - m_old).astype(BF16)
            grow = jnp.maximum(jnp.max(p, axis=0, keepdims=True).astype(F32), 1.0)
            acc_ref[e] = (acc_ref[e] + _dot(vaug_ref[e, j], p)) / grow
            m_ref[e] = m_old + jnp.log2(grow)
        else:
            m_old = m_ref[e]
            m_new = jnp.maximum(m_old, jnp.max(st, axis=0, keepdims=True))
            pv = _dot(vaug_ref[e, j], jnp.exp2(st - m_new).astype(BF16))
            acc_ref[e] = acc_ref[e] * jnp.exp2(m_old - m_new) + pv
            m_ref[e] = m_new

    clamp = lambda j: jnp.minimum(j, i)
    heads = range(MOBA_HEADS)
    jp = jnp.maximum(i - 1, 0)
    n_full = lax.shift_right_logical(i, 2)
    has_tail = (i & 3) >= 2

    def attend(one_sweep):
        for blk, slot, near in ((i, 0, True), (jp, 1, True), (clamp(0), 2, False), (clamp(1), 3, False)):
            for e in heads:
                qk(e, blk, slot, near=near)
        for e in heads:
            sm(e, i, 0, one_sweep, bias=tnear_ref[e, blk_rows:, :], first=True)
        for e in heads:
            sm(e, jp, 1, one_sweep, bias=tnear_ref[e, 0:blk_rows, :] + selprev_ref[e])

        def far(t, carry):
            j = 4 * t
            for half in (0, 2):
                for e in heads:
                    qk(e, clamp(j + half + 2), half)
                for e in heads:
                    qk(e, clamp(j + half + 3), half + 1)
                for e in heads:
                    sm(e, clamp(j + half), (half + 2) % 4, one_sweep)
                for e in heads:
                    sm(e, clamp(j + half + 1), (half + 3) % 4, one_sweep)
            return carry

        lax.fori_loop(0, n_full, far, 0)

        @pl.when(has_tail)
        def _():
            for e in heads:
                sm(e, clamp(4 * n_full), 2, one_sweep)
            for e in heads:
                sm(e, clamp(4 * n_full + 1), 3, one_sweep)

    attend(one_sweep=True)
    check = jnp.sum(acc_ref[...]) + jnp.sum(m_ref[...])
    pl.when(jnp.logical_not(jnp.abs(check) < jnp.inf))(lambda: attend(one_sweep=False))

    outs = []
    for e in range(MOBA_HEADS):
        acc = acc_ref[e]
        outs.append((acc[:d] / acc[d:d + 1]).T)
    out_ref[0] = jnp.concatenate(outs, axis=1).astype(BF16)


def _moba(cfar, mqt, mk, mvt, kmean, tnear):
    b, s, _ = mk.shape
    nb = s // MOBA_BLOCK
    assert nb <= MOBA_HEAD_DIM
    nh = MOBA_HEADS
    q_block = (1, 1, MOBA_WIDTH, MOBA_BLOCK)
    return pl.pallas_call(
        _moba_kernel,
        grid=(b, nb),
        in_specs=[
            pl.BlockSpec(memory_space=pltpu.SMEM),
            pl.BlockSpec(q_block, lambda bi, i: (bi, i, 0, 0)),
            pl.BlockSpec(q_block, lambda bi, i: (bi, jnp.minimum(i + 1, nb - 1), 0, 0)),
            pl.BlockSpec((1, MOBA_BLOCK, MOBA_WIDTH), lambda bi, i: (bi, i, 0)),
            pl.BlockSpec(q_block, lambda bi, i: (bi, i, 0, 0)),
            pl.BlockSpec((1, nh, nb, MOBA_HEAD_DIM), lambda bi, i: (bi, 0, 0, 0)),
            pl.BlockSpec((nh, 2 * MOBA_BLOCK, MOBA_BLOCK), lambda bi, i: (0, 0, 0),
                         pipeline_mode=pl.Buffered(1)),
        ],
        out_specs=pl.BlockSpec((1, MOBA_BLOCK, MOBA_WIDTH), lambda bi, i: (bi, i, 0)),
        out_shape=jax.ShapeDtypeStruct((b, s, MOBA_WIDTH), BF16),
        scratch_shapes=[
            pltpu.VMEM((nh // 2, s, 2 * PAIR_WIDTH), BF16),
            pltpu.VMEM((nh, nb, MOBA_HEAD_DIM + ONES_ROWS, MOBA_BLOCK), BF16),
            pltpu.VMEM((nh, PAIR_WIDTH, MOBA_BLOCK), BF16),
            pltpu.VMEM((nh, PAIR_WIDTH, MOBA_BLOCK), BF16),
            pltpu.VMEM((nh, nb, MOBA_BLOCK), BF16),
            pltpu.VMEM((nh, 1, MOBA_BLOCK), F32),
            pltpu.VMEM((nh, 1, MOBA_BLOCK), F32),
            pltpu.VMEM((4, nh, MOBA_BLOCK, MOBA_BLOCK), F32),
            pltpu.VMEM((nh, 1, MOBA_BLOCK), F32),
            pltpu.VMEM((nh, MOBA_HEAD_DIM + ONES_ROWS, MOBA_BLOCK), F32),
        ],
        compiler_params=pltpu.CompilerParams(
            dimension_semantics=("arbitrary", "arbitrary"), vmem_limit_bytes=VMEM_LIMIT),
        name="moba",
    )(cfar, mqt, mqt, mk, mvt, kmean, tnear)


def _rms(x, g):
    return x * lax.rsqrt(jnp.mean(x * x, axis=-1, keepdims=True) + EPS) * g


def _out_ffn_kernel(x_ref, ret_ref, moba_ref, wo_ref, wg_ref, wu_ref, wd_ref,
                    g_post_mix_ref, g_pre_ffn_ref, g_post_ffn_ref, out_ref):
    tm = x_ref.shape[0]
    groups = [slice(r0, r0 + tm // FFN_ROW_GROUPS) for r0 in range(0, tm, tm // FFN_ROW_GROUPS)]
    mixes = [_dot(jnp.concatenate([ret_ref[rows, :], moba_ref[rows, :]], axis=1), wo_ref[...])
             for rows in groups]
    for rows, mix in zip(groups, mixes):
        x1 = x_ref[rows, :] + _rms(mix, g_post_mix_ref[...])
        h = _rms(x1, g_pre_ffn_ref[...]).astype(BF16)
        f = None
        for lo, hi in FFN_CHUNKS:
            cols = slice(lo, hi)
            gate = _dot(h, wg_ref[:, cols])
            up = _dot(h, wu_ref[:, cols])
            act = (gate * jax.nn.sigmoid(gate) * up).astype(BF16)
            part = _dot(act, wd_ref[cols, :])
            f = part if f is None else f + part
        out_ref[rows, :] = x1 + _rms(f, g_post_ffn_ref[...])


def _out_ffn(x2, ret2, moba2, wo, wg, wu, wd, g_post_mix, g_pre_ffn, g_post_ffn):
    n, d = x2.shape
    tm = min(FFN_TOKENS, n)
    tok = lambda w: pl.BlockSpec((tm, w), lambda t: (t, 0))
    resident = lambda a: pl.BlockSpec(a.shape, lambda t: (0, 0), pipeline_mode=pl.Buffered(1))
    return pl.pallas_call(
        _out_ffn_kernel,
        grid=(n // tm,),
        in_specs=[tok(d), tok(RET_WIDTH), tok(MOBA_WIDTH),
                  resident(wo), resident(wg), resident(wu), resident(wd),
                  resident(g_post_mix), resident(g_pre_ffn), resident(g_post_ffn)],
        out_specs=tok(d),
        out_shape=jax.ShapeDtypeStruct((n, d), F32),
        compiler_params=pltpu.CompilerParams(
            dimension_semantics=("arbitrary",), vmem_limit_bytes=VMEM_LIMIT),
        name="out_ffn",
    )(x2, ret2, moba2, wo, wg, wu, wd, g_post_mix, g_pre_ffn, g_post_ffn)


def _rotary_tables(s):
    half = RET_QK_DIM // 2
    inv_freq = ROPE_BASE ** (-np.arange(half, dtype=np.float64) / half)
    ang = np.arange(s, dtype=np.float64)[:, None] * inv_freq[None, :]
    cos, sin = np.cos(ang), np.sin(ang)
    cq = np.tile(np.concatenate([cos, cos], axis=1), (1, RET_HEADS))
    sq = np.tile(np.concatenate([-sin, sin], axis=1), (1, RET_HEADS))
    return tuple(np.ascontiguousarray(t, dtype=np.float32) for t in (cq, sq, cos.T, sin.T))


def _retention_tables():
    c = RET_CHUNK
    log_gamma = np.log1p(-np.exp(np.linspace(math.log(1.0 / 32), math.log(1.0 / 512), RET_HEADS)))
    idx = np.arange(c, dtype=np.float64)
    diff = idx[:, None] - idx[None, :]
    dmat = np.where(diff >= 0, np.exp(np.maximum(diff, 0.0)[None] * log_gamma[:, None, None]), 0.0)
    q_decay = np.exp((idx + 1.0)[None, :] * log_gamma[:, None])
    k_decay = np.exp((c - 1.0 - idx)[None, :] * log_gamma[:, None])
    chunk_decay = np.exp(c * log_gamma)
    qd = np.broadcast_to(q_decay[:, :, None], (RET_HEADS, c, RET_V_DIM))
    kd = k_decay[:, None, :]
    cd = np.broadcast_to(chunk_decay[:, None, None], (RET_HEADS, 1, RET_V_DIM))
    return tuple(np.ascontiguousarray(t, dtype=np.float32) for t in (dmat, qd, kd, cd))


def _t5_bucket(rel):
    n = np.maximum(rel, 0)
    max_exact = REL_BUCKETS // 2
    large = max_exact + (np.log(np.maximum(n, 1) / max_exact) / math.log(REL_MAX_DIST / max_exact)
                         * (REL_BUCKETS - max_exact)).astype(np.int64)
    return np.where(n < max_exact, n, np.minimum(large, REL_BUCKETS - 1))


def _toeplitz(vec):
    h, two_n = vec.shape
    n = two_n // 2
    ext = jnp.concatenate([vec, jnp.zeros((h, 1), vec.dtype)], axis=1)
    skew = jnp.tile(ext, (1, n))[:, :n * two_n].reshape(h, n, two_n)
    return skew[:, :, n:]


def _bias_tables(rel_bias):
    bias_t = rel_bias.T.astype(F32) * LOG2E
    n = MOBA_BLOCK
    one_hot = (_t5_bucket(np.arange(2 * n))[:, None] == np.arange(REL_BUCKETS)).astype(np.float32)
    by_dist = jnp.einsum("rn,hn->hr", one_hot, bias_t, precision=lax.Precision.HIGHEST)
    tprev = _toeplitz(by_dist)
    own_vec = jnp.concatenate([jnp.full((bias_t.shape[0], n), NEG, F32), by_dist[:, :n]], axis=1)
    town = _toeplitz(own_vec)
    cfar = bias_t[:, REL_BUCKETS - 1]
    return jnp.concatenate([tprev, town], axis=1), cfar


def kernel(x, w_in, w_out, pre_mix_norm, post_mix_norm, pre_ffn_norm, post_ffn_norm,
           rel_bias, w_gate, w_up, w_down):
    b, s, d = x.shape
    assert d == D_MODEL and s % IN_TOKENS == 0 and MOBA_BLOCK + 1 >= REL_MAX_DIST
    depth = w_in.shape[0]

    cq, sq, ckt, skt = _rotary_tables(s)
    dmat, qd, kd, cd = _retention_tables()
    tnear, cfar = _bias_tables(rel_bias)

    sizes = [RET_QK_WIDTH, RET_QK_WIDTH, RET_WIDTH, RET_WIDTH, MOBA_WIDTH, MOBA_WIDTH, MOBA_WIDTH]
    o_rq, o_rk, o_rv, o_rg, o_mq, o_mk, o_mv, _ = np.cumsum([0] + sizes).tolist()

    for layer in range(depth):
        w = w_in[layer]
        w_nat = jnp.concatenate(
            [w[:, o_rq:o_rk], w[:, o_rv:o_mq], w[:, o_mk:o_mv]], axis=1).astype(BF16)
        w_tr = jnp.concatenate([w[:, o_rk:o_rv], w[:, o_mq:o_mk], w[:, o_mv:]], axis=1).T.astype(BF16)

        rq, rv, sg, mk, kmean, rkt, mqt, mvt = _in_proj(
            x, pre_mix_norm[layer][None, :], w_nat, w_tr, cq, sq, ckt, skt)

        ret = _retention(rq, rkt, rv, sg, dmat, qd, kd, cd)

        nb = s // MOBA_BLOCK
        kmean = kmean.reshape(b, nb, MOBA_HEADS, MOBA_HEAD_DIM).transpose(0, 2, 1, 3)
        moba = _moba(cfar, mqt, mk, mvt, kmean, tnear)

        x = _out_ffn(
            x.reshape(b * s, d), ret.reshape(b * s, RET_WIDTH), moba.reshape(b * s, MOBA_WIDTH),
            w_out[layer].astype(BF16), w_gate[layer].astype(BF16), w_up[layer].astype(BF16),
            w_down[layer].astype(BF16), post_mix_norm[layer][None, :], pre_ffn_norm[layer][None, :],
            post_ffn_norm[layer][None, :]).reshape(b, s, d)
    return x
```

```python
import functools
import math

import jax
import jax.numpy as jnp
import numpy as np
from jax import lax
from jax.experimental import pallas as pl
from jax.experimental.pallas import tpu as pltpu

F32 = jnp.float32
BF16 = jnp.bfloat16

D_MODEL = 1024
RET_HEADS = 4
RET_QK_DIM = 64
RET_V_DIM = 128
RET_CHUNK = 128
RET_QK_WIDTH = RET_HEADS * RET_QK_DIM
RET_WIDTH = RET_HEADS * RET_V_DIM
MOBA_HEADS = 8
MOBA_HEAD_DIM = 64
MOBA_WIDTH = MOBA_HEADS * MOBA_HEAD_DIM
MOBA_BLOCK = 256
MOBA_TOPK = 3
REL_BUCKETS = 32
REL_MAX_DIST = 128
D_FF = 2816
EPS = 1e-6
ROPE_BASE = 10000.0

NEG = -1e30
LOG2E = math.log2(math.e)

IN_TOKENS = 1024
IN_ROW_GROUPS = 2
RET_TOKENS = 1024
FFN_TOKENS = 1024
FFN_ROW_GROUPS = 4
MXU_TILE = 256
FFN_CHUNKS = ((0, 6 * MXU_TILE), (6 * MXU_TILE, D_FF))
VMEM_LIMIT = 56 * 1024 * 1024

NAT_WIDTH = RET_QK_WIDTH + 2 * RET_WIDTH + MOBA_WIDTH
TR_WIDTH = RET_QK_WIDTH + 2 * MOBA_WIDTH


def _nt_dot(a, b):
    return lax.dot_general(a, b, (((1,), (1,)), ((), ())), preferred_element_type=F32)


def _dot(a, b):
    return jnp.dot(a, b, preferred_element_type=F32)


def _in_proj_kernel(x_ref, g_ref, wn_ref, wt_ref, cq_ref, sq_ref, ckt_ref, skt_ref,
                    rq_ref, rv_ref, sg_ref, mk_ref, kmean_ref, rkt_ref, mqt_ref, mvt_ref):
    tm = x_ref.shape[1] // IN_ROW_GROUPS
    hs = []
    for grp in range(IN_ROW_GROUPS):
        x = x_ref[0, grp * tm:(grp + 1) * tm, :]
        ms = jnp.mean(x * x, axis=-1, keepdims=True)
        hs.append((x * lax.rsqrt(ms + EPS) * g_ref[...]).astype(BF16))

    for grp, h in enumerate(hs):
        rows = slice(grp * tm, (grp + 1) * tm)
        pn = _dot(h, wn_ref[...])
        pt = _nt_dot(wt_ref[...], h)

        rq = pn[:, :RET_QK_WIDTH]
        lane = lax.broadcasted_iota(jnp.int32, rq.shape, 1)
        first_half = (lane % RET_QK_DIM) < (RET_QK_DIM // 2)
        partner = jnp.where(first_half,
                            pltpu.roll(rq, RET_QK_WIDTH - RET_QK_DIM // 2, 1),
                            pltpu.roll(rq, RET_QK_DIM // 2, 1))
        rq_ref[0, rows, :] = (rq * cq_ref[rows, :] + partner * sq_ref[rows, :]).astype(BF16)

        rv_ref[0, rows, :] = pn[:, RET_QK_WIDTH:RET_QK_WIDTH + RET_WIDTH].astype(BF16)
        rg = pn[:, RET_QK_WIDTH + RET_WIDTH:RET_QK_WIDTH + 2 * RET_WIDTH]
        sg_ref[0, rows, :] = (rg * jax.nn.sigmoid(rg)).astype(BF16)

        mk = pn[:, RET_QK_WIDTH + 2 * RET_WIDTH:]
        mk_ref[0, rows, :] = mk.astype(BF16)
        blk0 = grp * (tm // MOBA_BLOCK)
        for blk in range(tm // MOBA_BLOCK):
            kmean_ref[0, 0, blk0 + blk:blk0 + blk + 1, :] = jnp.mean(
                mk[blk * MOBA_BLOCK:(blk + 1) * MOBA_BLOCK], axis=0, keepdims=True)

        half = RET_QK_DIM // 2
        cos_t = ckt_ref[:, rows]
        sin_t = skt_ref[:, rows]
        parts = []
        for hd in range(RET_HEADS):
            x1 = pt[hd * RET_QK_DIM:hd * RET_QK_DIM + half]
            x2 = pt[hd * RET_QK_DIM + half:(hd + 1) * RET_QK_DIM]
            parts.append(x1 * cos_t - x2 * sin_t)
            parts.append(x2 * cos_t + x1 * sin_t)
        rkt = (jnp.concatenate(parts, axis=0) * (RET_QK_DIM ** -0.5)).astype(BF16)
        c0 = grp * (tm // RET_CHUNK)
        for c in range(tm // RET_CHUNK):
            rkt_ref[0, c0 + c] = rkt[:, c * RET_CHUNK:(c + 1) * RET_CHUNK]

        mqt = (pt[RET_QK_WIDTH:RET_QK_WIDTH + MOBA_WIDTH] * (MOBA_HEAD_DIM ** -0.5 * LOG2E)).astype(BF16)
        mvt = pt[RET_QK_WIDTH + MOBA_WIDTH:].astype(BF16)
        for blk in range(tm // MOBA_BLOCK):
            mqt_ref[0, blk0 + blk] = mqt[:, blk * MOBA_BLOCK:(blk + 1) * MOBA_BLOCK]
            mvt_ref[0, blk0 + blk] = mvt[:, blk * MOBA_BLOCK:(blk + 1) * MOBA_BLOCK]


def _in_proj(x, gain, w_nat, w_tr, cq, sq, ckt, skt):
    b, s, d = x.shape
    tm = IN_TOKENS
    ns = s // tm
    bpt = tm // MOBA_BLOCK
    cpt = tm // RET_CHUNK
    const = lambda si, bi: (0, 0)
    out_shape = (
        jax.ShapeDtypeStruct((b, s, RET_QK_WIDTH), BF16),
        jax.ShapeDtypeStruct((b, s, RET_WIDTH), BF16),
        jax.ShapeDtypeStruct((b, s, RET_WIDTH), BF16),
        jax.ShapeDtypeStruct((b, s, MOBA_WIDTH), BF16),
        jax.ShapeDtypeStruct((b, ns, bpt, MOBA_WIDTH), F32),
        jax.ShapeDtypeStruct((b, s // RET_CHUNK, RET_QK_WIDTH, RET_CHUNK), BF16),
        jax.ShapeDtypeStruct((b, s // MOBA_BLOCK, MOBA_WIDTH, MOBA_BLOCK), BF16),
        jax.ShapeDtypeStruct((b, s // MOBA_BLOCK, MOBA_WIDTH, MOBA_BLOCK), BF16),
    )
    tok = lambda w: pl.BlockSpec((1, tm, w), lambda si, bi: (bi, si, 0))
    return pl.pallas_call(
        _in_proj_kernel,
        grid=(ns, b),
        in_specs=[
            tok(d),
            pl.BlockSpec((1, d), const),
            pl.BlockSpec((d, NAT_WIDTH), const),
            pl.BlockSpec((TR_WIDTH, d), const),
            pl.BlockSpec((tm, RET_QK_WIDTH), lambda si, bi: (si, 0)),
            pl.BlockSpec((tm, RET_QK_WIDTH), lambda si, bi: (si, 0)),
            pl.BlockSpec((RET_QK_DIM // 2, tm), lambda si, bi: (0, si)),
            pl.BlockSpec((RET_QK_DIM // 2, tm), lambda si, bi: (0, si)),
        ],
        out_specs=(
            tok(RET_QK_WIDTH), tok(RET_WIDTH), tok(RET_WIDTH), tok(MOBA_WIDTH),
            pl.BlockSpec((1, 1, bpt, MOBA_WIDTH), lambda si, bi: (bi, si, 0, 0)),
            pl.BlockSpec((1, cpt, RET_QK_WIDTH, RET_CHUNK), lambda si, bi: (bi, si, 0, 0)),
            pl.BlockSpec((1, bpt, MOBA_WIDTH, MOBA_BLOCK), lambda si, bi: (bi, si, 0, 0)),
            pl.BlockSpec((1, bpt, MOBA_WIDTH, MOBA_BLOCK), lambda si, bi: (bi, si, 0, 0)),
        ),
        out_shape=out_shape,
        compiler_params=pltpu.CompilerParams(
            dimension_semantics=("arbitrary", "arbitrary"), vmem_limit_bytes=VMEM_LIMIT),
        name="in_proj",
    )(x, gain, w_nat, w_tr, cq, sq, ckt, skt)


def _retention_kernel(rq_ref, rkt_ref, rv_ref, sg_ref, dmat_ref, qd_ref, kd_ref, cd_ref,
                      out_ref, state_ref):
    @pl.when(pl.program_id(1) == 0)
    def _():
        state_ref[...] = jnp.zeros_like(state_ref)

    n_chunks = rq_ref.shape[1] // RET_CHUNK

    def chunk(c, carry):
        r0 = pl.multiple_of(c * RET_CHUNK, RET_CHUNK)
        rows = pl.ds(r0, RET_CHUNK)
        for hd in range(RET_HEADS):
            q = rq_ref[0, rows, hd * RET_QK_DIM:(hd + 1) * RET_QK_DIM]
            kt = rkt_ref[0, c, hd * RET_QK_DIM:(hd + 1) * RET_QK_DIM, :]
            v = rv_ref[0, rows, hd * RET_V_DIM:(hd + 1) * RET_V_DIM]
            state = state_ref[hd]
            qk_qs = _dot(q, jnp.concatenate([kt, state.astype(BF16)], axis=1))
            scores = qk_qs[:, :RET_CHUNK] * dmat_ref[hd]
            o = _dot(scores.astype(BF16), v) + qd_ref[hd] * qk_qs[:, RET_CHUNK:]
            kts = (kt.astype(F32) * kd_ref[hd]).astype(BF16)
            state_ref[hd] = state * cd_ref[hd] + _dot(kts, v)
            ms = jnp.mean(o * o, axis=-1, keepdims=True)
            gate = sg_ref[0, rows, hd * RET_V_DIM:(hd + 1) * RET_V_DIM].astype(F32)
            out_ref[0, rows, hd * RET_V_DIM:(hd + 1) * RET_V_DIM] = (
                o * lax.rsqrt(ms + EPS) * gate).astype(BF16)
        return carry

    lax.fori_loop(0, n_chunks, chunk, 0, unroll=8)


def _retention(rq, rkt, rv, sg, dmat, qd, kd, cd):
    b, s, _ = rq.shape
    tc = min(RET_TOKENS, s)
    tok = lambda w: pl.BlockSpec((1, tc, w), lambda bi, si: (bi, si, 0))
    tab = lambda a: pl.BlockSpec(a.shape, lambda bi, si: (0,) * a.ndim)
    return pl.pallas_call(
        _retention_kernel,
        grid=(b, s // tc),
        in_specs=[
            tok(RET_QK_WIDTH),
            pl.BlockSpec((1, tc // RET_CHUNK, RET_QK_WIDTH, RET_CHUNK), lambda bi, si: (bi, si, 0, 0)),
            tok(RET_WIDTH), tok(RET_WIDTH),
            tab(dmat), tab(qd), tab(kd), tab(cd),
        ],
        out_specs=tok(RET_WIDTH),
        out_shape=jax.ShapeDtypeStruct((b, s, RET_WIDTH), BF16),
        scratch_shapes=[pltpu.VMEM((RET_HEADS, RET_QK_DIM, RET_V_DIM), F32)],
        compiler_params=pltpu.CompilerParams(
            dimension_semantics=("arbitrary", "arbitrary"), vmem_limit_bytes=VMEM_LIMIT),
        name="retention",
    )(rq, rkt, rv, sg, dmat, qd, kd, cd)


ONES_ROWS = 16
PAIR_WIDTH = 2 * MOBA_HEAD_DIM
GATE_BLOCKS = 4


def _moba_gate_kernel(cfar_ref, qt_ref, km_ref, selfar_ref, selprev_ref):
    d = MOBA_HEAD_DIM
    nb = km_ref.shape[2]
    n_q = qt_ref.shape[1]
    shape = (nb, n_q * MOBA_BLOCK)
    blk = lax.broadcasted_iota(jnp.int32, shape, 0)
    i = pl.program_id(1) * n_q + lax.broadcasted_iota(jnp.int32, shape, 1) // MOBA_BLOCK
    blk_f = blk.astype(F32)
    for h in range(MOBA_HEADS):
        qt = jnp.concatenate([qt_ref[0, c, h * d:(h + 1) * d, :] for c in range(n_q)], axis=1)
        km = km_ref[0, h]
        km_hi = km.astype(BF16)
        km_lo = (km - km_hi.astype(F32)).astype(BF16)
        gate = _dot(km_hi, qt) + _dot(km_lo, qt)
        gate = jnp.where(blk < i, gate, NEG)
        chosen = jnp.zeros(gate.shape, F32)
        for _ in range(MOBA_TOPK):
            best = jnp.max(gate, axis=0, keepdims=True)
            idx = jnp.min(jnp.where(gate == best, blk_f, float(nb)), axis=0, keepdims=True)
            hit = blk_f == idx
            chosen = jnp.where(hit & (best > 0.5 * NEG), 1.0, chosen)
            gate = jnp.where(hit, NEG, gate)
        picked = chosen > 0.5
        selfar_ref[0, h] = jnp.where(picked & (blk < i - 1), cfar_ref[h], NEG).astype(BF16)
        prev_hit = jnp.max(jnp.where(picked & (blk == i - 1), 1.0, 0.0), axis=0, keepdims=True)
        selprev_ref[0, h] = jnp.where(prev_hit > 0.5, 0.0, NEG)


def _moba_gate(cfar, mqt, kmean):
    b, nb, _, _ = mqt.shape
    s = nb * MOBA_BLOCK
    n_q = math.gcd(GATE_BLOCKS, nb)
    width = n_q * MOBA_BLOCK
    return pl.pallas_call(
        _moba_gate_kernel,
        grid=(b, nb // n_q),
        in_specs=[
            pl.BlockSpec(memory_space=pltpu.SMEM),
            pl.BlockSpec((1, n_q, MOBA_WIDTH, MOBA_BLOCK), lambda bi, i: (bi, i, 0, 0)),
            pl.BlockSpec((1, MOBA_HEADS, nb, MOBA_HEAD_DIM), lambda bi, i: (bi, 0, 0, 0)),
        ],
        out_specs=(
            pl.BlockSpec((1, MOBA_HEADS, nb, width), lambda bi, i: (bi, 0, 0, i)),
            pl.BlockSpec((1, MOBA_HEADS, 1, width), lambda bi, i: (bi, 0, 0, i)),
        ),
        out_shape=(
            jax.ShapeDtypeStruct((b, MOBA_HEADS, nb, s), BF16),
            jax.ShapeDtypeStruct((b, MOBA_HEADS, 1, s), F32),
        ),
        compiler_params=pltpu.CompilerParams(
            dimension_semantics=("arbitrary", "arbitrary"), vmem_limit_bytes=VMEM_LIMIT),
        name="moba_gate",
    )(cfar, mqt, kmean)


def _moba_kernel(qt_ref, k_ref, vt_ref, selfar_ref, selprev_ref, tnear_ref,
                 out_ref, kaug_ref, vaug_ref, qaug_ref, qnear_ref, s_ref, m_ref, acc_ref):
    i = pl.program_id(1)
    d = MOBA_HEAD_DIM
    nb = vaug_ref.shape[1]
    s_len = kaug_ref.shape[1]
    blk_rows = MOBA_BLOCK
    n_pairs = MOBA_HEADS // 2

    @pl.when((pl.program_id(0) == 0) & (i == 0))
    def _():
        row_blk = lax.broadcasted_iota(jnp.int32, (s_len, 2 * PAIR_WIDTH), 0) // MOBA_BLOCK
        lane = lax.broadcasted_iota(jnp.int32, (s_len, 2 * PAIR_WIDTH), 1)
        hot_lane = jnp.where(lane < PAIR_WIDTH, lane - d, lane - PAIR_WIDTH)
        in_band = (lane >= d) & (lane < PAIR_WIDTH + d)
        pattern = jnp.where(in_band & (row_blk == hot_lane), 1.0, 0.0).astype(BF16)
        for pe in range(n_pairs):
            kaug_ref[pe] = pattern
        for e in range(MOBA_HEADS):
            vaug_ref[e, :, d:, :] = jnp.ones((nb, ONES_ROWS, MOBA_BLOCK), BF16)

    own_rows = pl.ds(pl.multiple_of(i * blk_rows, blk_rows), blk_rows)
    for pe in range(n_pairs):
        kaug_ref[pe, own_rows, 0:d] = k_ref[0, :, pe * PAIR_WIDTH:pe * PAIR_WIDTH + d]
        kaug_ref[pe, own_rows, PAIR_WIDTH + d:] = k_ref[0, :, pe * PAIR_WIDTH + d:(pe + 1) * PAIR_WIDTH]
    for e in range(MOBA_HEADS):
        vaug_ref[e, i, 0:d, :] = vt_ref[0, 0, e * d:(e + 1) * d, :]

    prow = lax.broadcasted_iota(jnp.int32, (PAIR_WIDTH, MOBA_BLOCK), 0)
    for e in range(MOBA_HEADS):
        pe, he = divmod(e, 2)
        qt2 = qt_ref[0, 0, pe * PAIR_WIDTH:(pe + 1) * PAIR_WIDTH, :]
        q_only = jnp.where((prow >= he * d) & (prow < (he + 1) * d), qt2, jnp.zeros_like(qt2))
        qnear_ref[e] = q_only
        qaug_ref[e] = q_only
        hot_row = d if he == 0 else 0
        qaug_ref[e, hot_row:hot_row + nb, :] = selfar_ref[0, e]

    def qk(e, j, slot, near=False):
        rows = pl.ds(pl.multiple_of(j * blk_rows, blk_rows), blk_rows)
        keys = kaug_ref[e // 2, rows, (e % 2) * PAIR_WIDTH:(e % 2 + 1) * PAIR_WIDTH]
        s_ref[slot, e] = _dot(keys, qnear_ref[e] if near else qaug_ref[e])

    def sm(e, j, slot, one_sweep, bias=None, first=False):
        st = s_ref[slot, e]
        if bias is not None:
            st = st + bias
        if first:
            mj = jnp.max(st, axis=0, keepdims=True)
            m_ref[e] = mj
            acc_ref[e] = _dot(vaug_ref[e, j], jnp.exp2(st - mj).astype(BF16))
        elif one_sweep:
            m_old = m_ref[e]
            p = jnp.exp2(st - m_old).astype(BF16)
            grow = jnp.maximum(jnp.max(p, axis=0, keepdims=True).astype(F32), 1.0)
            acc_ref[e] = (acc_ref[e] + _dot(vaug_ref[e, j], p)) / grow
            m_ref[e] = m_old + jnp.log2(grow)
        else:
            m_old = m_ref[e]
            m_new = jnp.maximum(m_old, jnp.max(st, axis=0, keepdims=True))
            pv = _dot(vaug_ref[e, j], jnp.exp2(st - m_new).astype(BF16))
            acc_ref[e] = acc_ref[e] * jnp.exp2(m_old - m_new) + pv
            m_ref[e] = m_new

    clamp = lambda j: jnp.minimum(j, i)
    heads = range(MOBA_HEADS)
    jp = jnp.maximum(i - 1, 0)
    n_full = lax.shift_right_logical(i, 2)
    has_tail = (i & 3) >= 2

    def attend(one_sweep):
        for blk, slot, near in ((i, 0, True), (jp, 1, True), (clamp(0), 2, False), (clamp(1), 3, False)):
            for e in heads:
                qk(e, blk, slot, near=near)
        for e in heads:
            sm(e, i, 0, one_sweep, bias=tnear_ref[e, blk_rows:, :], first=True)
        for e in heads:
            sm(e, jp, 1, one_sweep, bias=tnear_ref[e, 0:blk_rows, :] + selprev_ref[0, e])

        def far(t, carry):
            j = 4 * t
            for half in (0, 2):
                for e in heads:
                    qk(e, clamp(j + half + 2), half)
                for e in heads:
                    qk(e, clamp(j + half + 3), half + 1)
                for e in heads:
                    sm(e, clamp(j + half), (half + 2) % 4, one_sweep)
                for e in heads:
                    sm(e, clamp(j + half + 1), (half + 3) % 4, one_sweep)
            return carry

        lax.fori_loop(0, n_full, far, 0)

        @pl.when(has_tail)
        def _():
            for e in heads:
                sm(e, clamp(4 * n_full), 2, one_sweep)
            for e in heads:
                sm(e, clamp(4 * n_full + 1), 3, one_sweep)

    attend(one_sweep=True)
    check = jnp.sum(acc_ref[...]) + jnp.sum(m_ref[...])
    pl.when(jnp.logical_not(jnp.abs(check) < jnp.inf))(lambda: attend(one_sweep=False))

    outs = []
    for e in range(MOBA_HEADS):
        acc = acc_ref[e]
        outs.append((acc[:d] / acc[d:d + 1]).T)
    out_ref[0] = jnp.concatenate(outs, axis=1).astype(BF16)


def _moba(mqt, mk, mvt, selfar, selprev, tnear):
    b, s, _ = mk.shape
    nb = s // MOBA_BLOCK
    assert nb <= MOBA_HEAD_DIM
    nh = MOBA_HEADS
    return pl.pallas_call(
        _moba_kernel,
        grid=(b, nb),
        in_specs=[
            pl.BlockSpec((1, 1, MOBA_WIDTH, MOBA_BLOCK), lambda bi, i: (bi, i, 0, 0)),
            pl.BlockSpec((1, MOBA_BLOCK, MOBA_WIDTH), lambda bi, i: (bi, i, 0)),
            pl.BlockSpec((1, 1, MOBA_WIDTH, MOBA_BLOCK), lambda bi, i: (bi, i, 0, 0)),
            pl.BlockSpec((1, nh, nb, MOBA_BLOCK), lambda bi, i: (bi, 0, 0, i)),
            pl.BlockSpec((1, nh, 1, MOBA_BLOCK), lambda bi, i: (bi, 0, 0, i)),
            pl.BlockSpec((nh, 2 * MOBA_BLOCK, MOBA_BLOCK), lambda bi, i: (0, 0, 0),
                         pipeline_mode=pl.Buffered(1)),
        ],
        out_specs=pl.BlockSpec((1, MOBA_BLOCK, MOBA_WIDTH), lambda bi, i: (bi, i, 0)),
        out_shape=jax.ShapeDtypeStruct((b, s, MOBA_WIDTH), BF16),
        scratch_shapes=[
            pltpu.VMEM((nh // 2, s, 2 * PAIR_WIDTH), BF16),
            pltpu.VMEM((nh, nb, MOBA_HEAD_DIM + ONES_ROWS, MOBA_BLOCK), BF16),
            pltpu.VMEM((nh, PAIR_WIDTH, MOBA_BLOCK), BF16),
            pltpu.VMEM((nh, PAIR_WIDTH, MOBA_BLOCK), BF16),
            pltpu.VMEM((4, nh, MOBA_BLOCK, MOBA_BLOCK), F32),
            pltpu.VMEM((nh, 1, MOBA_BLOCK), F32),
            pltpu.VMEM((nh, MOBA_HEAD_DIM + ONES_ROWS, MOBA_BLOCK), F32),
        ],
        compiler_params=pltpu.CompilerParams(
            dimension_semantics=("arbitrary", "arbitrary"), vmem_limit_bytes=VMEM_LIMIT),
        name="moba",
    )(mqt, mk, mvt, selfar, selprev, tnear)


def _rms(x, g):
    return x * lax.rsqrt(jnp.mean(x * x, axis=-1, keepdims=True) + EPS) * g


def _out_ffn_kernel(x_ref, ret_ref, moba_ref, wo_ref, wg_ref, wu_ref, wd_ref,
                    g_post_mix_ref, g_pre_ffn_ref, g_post_ffn_ref, out_ref):
    tm = x_ref.shape[0]
    groups = [slice(r0, r0 + tm // FFN_ROW_GROUPS) for r0 in range(0, tm, tm // FFN_ROW_GROUPS)]
    mixes = [_dot(jnp.concatenate([ret_ref[rows, :], moba_ref[rows, :]], axis=1), wo_ref[...])
             for rows in groups]
    for rows, mix in zip(groups, mixes):
        x1 = x_ref[rows, :] + _rms(mix, g_post_mix_ref[...])
        h = _rms(x1, g_pre_ffn_ref[...]).astype(BF16)
        f = None
        for lo, hi in FFN_CHUNKS:
            cols = slice(lo, hi)
            gate = _dot(h, wg_ref[:, cols])
            up = _dot(h, wu_ref[:, cols])
            act = (gate * jax.nn.sigmoid(gate) * up).astype(BF16)
            part = _dot(act, wd_ref[cols, :])
            f = part if f is None else f + part
        out_ref[rows, :] = x1 + _rms(f, g_post_ffn_ref[...])


def _out_ffn(x2, ret2, moba2, wo, wg, wu, wd, g_post_mix, g_pre_ffn, g_post_ffn):
    n, d = x2.shape
    tm = min(FFN_TOKENS, n)
    tok = lambda w: pl.BlockSpec((tm, w), lambda t: (t, 0))
    resident = lambda a: pl.BlockSpec(a.shape, lambda t: (0, 0), pipeline_mode=pl.Buffered(1))
    return pl.pallas_call(
        _out_ffn_kernel,
        grid=(n // tm,),
        in_specs=[tok(d), tok(RET_WIDTH), tok(MOBA_WIDTH),
                  resident(wo), resident(wg), resident(wu), resident(wd),
                  resident(g_post_mix), resident(g_pre_ffn), resident(g_post_ffn)],
        out_specs=tok(d),
        out_shape=jax.ShapeDtypeStruct((n, d), F32),
        compiler_params=pltpu.CompilerParams(
            dimension_semantics=("arbitrary",), vmem_limit_bytes=VMEM_LIMIT),
        name="out_ffn",
    )(x2, ret2, moba2, wo, wg, wu, wd, g_post_mix, g_pre_ffn, g_post_ffn)


def _rotary_tables(s):
    half = RET_QK_DIM // 2
    inv_freq = ROPE_BASE ** (-np.arange(half, dtype=np.float64) / half)
    ang = np.arange(s, dtype=np.float64)[:, None] * inv_freq[None, :]
    cos, sin = np.cos(ang), np.sin(ang)
    cq = np.tile(np.concatenate([cos, cos], axis=1), (1, RET_HEADS))
    sq = np.tile(np.concatenate([-sin, sin], axis=1), (1, RET_HEADS))
    return tuple(np.ascontiguousarray(t, dtype=np.float32) for t in (cq, sq, cos.T, sin.T))


def _retention_tables():
    c = RET_CHUNK
    log_gamma = np.log1p(-np.exp(np.linspace(math.log(1.0 / 32), math.log(1.0 / 512), RET_HEADS)))
    idx = np.arange(c, dtype=np.float64)
    diff = idx[:, None] - idx[None, :]
    dmat = np.where(diff >= 0, np.exp(np.maximum(diff, 0.0)[None] * log_gamma[:, None, None]), 0.0)
    q_decay = np.exp((idx + 1.0)[None, :] * log_gamma[:, None])
    k_decay = np.exp((c - 1.0 - idx)[None, :] * log_gamma[:, None])
    chunk_decay = np.exp(c * log_gamma)
    qd = np.broadcast_to(q_decay[:, :, None], (RET_HEADS, c, RET_V_DIM))
    kd = k_decay[:, None, :]
    cd = np.broadcast_to(chunk_decay[:, None, None], (RET_HEADS, 1, RET_V_DIM))
    return tuple(np.ascontiguousarray(t, dtype=np.float32) for t in (dmat, qd, kd, cd))


def _t5_bucket(rel):
    n = np.maximum(rel, 0)
    max_exact = REL_BUCKETS // 2
    large = max_exact + (np.log(np.maximum(n, 1) / max_exact) / math.log(REL_MAX_DIST / max_exact)
                         * (REL_BUCKETS - max_exact)).astype(np.int64)
    return np.where(n < max_exact, n, np.minimum(large, REL_BUCKETS - 1))


def _toeplitz(vec):
    h, two_n = vec.shape
    n = two_n // 2
    ext = jnp.concatenate([vec, jnp.zeros((h, 1), vec.dtype)], axis=1)
    skew = jnp.tile(ext, (1, n))[:, :n * two_n].reshape(h, n, two_n)
    return skew[:, :, n:]


def _bias_tables(rel_bias):
    bias_t = rel_bias.T.astype(F32) * LOG2E
    n = MOBA_BLOCK
    one_hot = (_t5_bucket(np.arange(2 * n))[:, None] == np.arange(REL_BUCKETS)).astype(np.float32)
    by_dist = jnp.einsum("rn,hn->hr", one_hot, bias_t, precision=lax.Precision.HIGHEST)
    tprev = _toeplitz(by_dist)
    own_vec = jnp.concatenate([jnp.full((bias_t.shape[0], n), NEG, F32), by_dist[:, :n]], axis=1)
    town = _toeplitz(own_vec)
    cfar = bias_t[:, REL_BUCKETS - 1]
    return jnp.concatenate([tprev, town], axis=1), cfar


def kernel(x, w_in, w_out, pre_mix_norm, post_mix_norm, pre_ffn_norm, post_ffn_norm,
           rel_bias, w_gate, w_up, w_down):
    b, s, d = x.shape
    assert d == D_MODEL and s % IN_TOKENS == 0 and MOBA_BLOCK + 1 >= REL_MAX_DIST
    depth = w_in.shape[0]

    cq, sq, ckt, skt = _rotary_tables(s)
    dmat, qd, kd, cd = _retention_tables()
    tnear, cfar = _bias_tables(rel_bias)

    sizes = [RET_QK_WIDTH, RET_QK_WIDTH, RET_WIDTH, RET_WIDTH, MOBA_WIDTH, MOBA_WIDTH, MOBA_WIDTH]
    o_rq, o_rk, o_rv, o_rg, o_mq, o_mk, o_mv, _ = np.cumsum([0] + sizes).tolist()

    for layer in range(depth):
        w = w_in[layer]
        w_nat = jnp.concatenate(
            [w[:, o_rq:o_rk], w[:, o_rv:o_mq], w[:, o_mk:o_mv]], axis=1).astype(BF16)
        w_tr = jnp.concatenate([w[:, o_rk:o_rv], w[:, o_mq:o_mk], w[:, o_mv:]], axis=1).T.astype(BF16)

        rq, rv, sg, mk, kmean, rkt, mqt, mvt = _in_proj(
            x, pre_mix_norm[layer][None, :], w_nat, w_tr, cq, sq, ckt, skt)

        ret = _retention(rq, rkt, rv, sg, dmat, qd, kd, cd)

        nb = s // MOBA_BLOCK
        kmean = kmean.reshape(b, nb, MOBA_HEADS, MOBA_HEAD_DIM).transpose(0, 2, 1, 3)
        selfar, selprev = _moba_gate(cfar, mqt, kmean)
        moba = _moba(mqt, mk, mvt, selfar, selprev, tnear)

        x = _out_ffn(
            x.reshape(b * s, d), ret.reshape(b * s, RET_WIDTH), moba.reshape(b * s, MOBA_WIDTH),
            w_out[layer].astype(BF16), w_gate[layer].astype(BF16), w_up[layer].astype(BF16),
            w_down[layer].astype(BF16), post_mix_norm[layer][None, :], pre_ffn_norm[layer][None, :],
            post_ffn_norm[layer][None, :]).reshape(b, s, d)
    return x
```

```python
import functools
import math

import jax
import jax.numpy as jnp
import numpy as np
from jax import lax
from jax.experimental import pallas as pl
from jax.experimental.pallas import tpu as pltpu

F32 = jnp.float32
BF16 = jnp.bfloat16

D_MODEL = 1024
RET_HEADS = 4
RET_QK_DIM = 64
RET_V_DIM = 128
RET_CHUNK = 128
RET_QK_WIDTH = RET_HEADS * RET_QK_DIM
RET_WIDTH = RET_HEADS * RET_V_DIM
MOBA_HEADS = 8
MOBA_HEAD_DIM = 64
MOBA_WIDTH = MOBA_HEADS * MOBA_HEAD_DIM
MOBA_BLOCK = 256
MOBA_TOPK = 3
REL_BUCKETS = 32
REL_MAX_DIST = 128
D_FF = 2816
EPS = 1e-6
ROPE_BASE = 10000.0

NEG = -1e30
LOG2E = math.log2(math.e)

IN_TOKENS = 1024
IN_ROW_GROUPS = 2
RET_TOKENS = 1024
FFN_TOKENS = 1024
FFN_ROW_GROUPS = 4
MXU_TILE = 256
FFN_CHUNKS = ((0, 6 * MXU_TILE), (6 * MXU_TILE, D_FF))
VMEM_LIMIT = 56 * 1024 * 1024

NAT_WIDTH = RET_QK_WIDTH + 2 * RET_WIDTH + MOBA_WIDTH
TR_WIDTH = RET_QK_WIDTH + 2 * MOBA_WIDTH


def _nt_dot(a, b):
    return lax.dot_general(a, b, (((1,), (1,)), ((), ())), preferred_element_type=F32)


def _dot(a, b):
    return jnp.dot(a, b, preferred_element_type=F32)


def _in_proj_kernel(x_ref, g_ref, wn_ref, wt_ref, cq_ref, sq_ref, ckt_ref, skt_ref,
                    rq_ref, rv_ref, sg_ref, mk_ref, kmean_ref, rkt_ref, mqt_ref, mvt_ref):
    tm = x_ref.shape[1] // IN_ROW_GROUPS
    hs = []
    for grp in range(IN_ROW_GROUPS):
        x = x_ref[0, grp * tm:(grp + 1) * tm, :]
        ms = jnp.mean(x * x, axis=-1, keepdims=True)
        hs.append((x * lax.rsqrt(ms + EPS) * g_ref[...]).astype(BF16))

    for grp, h in enumerate(hs):
        rows = slice(grp * tm, (grp + 1) * tm)
        pn = _dot(h, wn_ref[...])
        pt = _nt_dot(wt_ref[...], h)

        rq = pn[:, :RET_QK_WIDTH]
        lane = lax.broadcasted_iota(jnp.int32, rq.shape, 1)
        first_half = (lane % RET_QK_DIM) < (RET_QK_DIM // 2)
        partner = jnp.where(first_half,
                            pltpu.roll(rq, RET_QK_WIDTH - RET_QK_DIM // 2, 1),
                            pltpu.roll(rq, RET_QK_DIM // 2, 1))
        rq_ref[0, rows, :] = (rq * cq_ref[rows, :] + partner * sq_ref[rows, :]).astype(BF16)

        rv_ref[0, rows, :] = pn[:, RET_QK_WIDTH:RET_QK_WIDTH + RET_WIDTH].astype(BF16)
        rg = pn[:, RET_QK_WIDTH + RET_WIDTH:RET_QK_WIDTH + 2 * RET_WIDTH]
        sg_ref[0, rows, :] = (rg * jax.nn.sigmoid(rg)).astype(BF16)

        mk = pn[:, RET_QK_WIDTH + 2 * RET_WIDTH:]
        mk_ref[0, rows, :] = mk.astype(BF16)
        blk0 = grp * (tm // MOBA_BLOCK)
        for blk in range(tm // MOBA_BLOCK):
            kmean_ref[0, 0, blk0 + blk:blk0 + blk + 1, :] = jnp.mean(
                mk[blk * MOBA_BLOCK:(blk + 1) * MOBA_BLOCK], axis=0, keepdims=True)

        half = RET_QK_DIM // 2
        cos_t = ckt_ref[:, rows]
        sin_t = skt_ref[:, rows]
        parts = []
        for hd in range(RET_HEADS):
            x1 = pt[hd * RET_QK_DIM:hd * RET_QK_DIM + half]
            x2 = pt[hd * RET_QK_DIM + half:(hd + 1) * RET_QK_DIM]
            parts.append(x1 * cos_t - x2 * sin_t)
            parts.append(x2 * cos_t + x1 * sin_t)
        rkt = (jnp.concatenate(parts, axis=0) * (RET_QK_DIM ** -0.5)).astype(BF16)
        c0 = grp * (tm // RET_CHUNK)
        for c in range(tm // RET_CHUNK):
            rkt_ref[0, c0 + c] = rkt[:, c * RET_CHUNK:(c + 1) * RET_CHUNK]

        mqt = (pt[RET_QK_WIDTH:RET_QK_WIDTH + MOBA_WIDTH] * (MOBA_HEAD_DIM ** -0.5 * LOG2E)).astype(BF16)
        mvt = pt[RET_QK_WIDTH + MOBA_WIDTH:].astype(BF16)
        for blk in range(tm // MOBA_BLOCK):
            mqt_ref[0, blk0 + blk] = mqt[:, blk * MOBA_BLOCK:(blk + 1) * MOBA_BLOCK]
            mvt_ref[0, blk0 + blk] = mvt[:, blk * MOBA_BLOCK:(blk + 1) * MOBA_BLOCK]


def _in_proj(x, gain, w_nat, w_tr, cq, sq, ckt, skt):
    b, s, d = x.shape
    tm = IN_TOKENS
    ns = s // tm
    bpt = tm // MOBA_BLOCK
    cpt = tm // RET_CHUNK
    const = lambda si, bi: (0, 0)
    out_shape = (
        jax.ShapeDtypeStruct((b, s, RET_QK_WIDTH), BF16),
        jax.ShapeDtypeStruct((b, s, RET_WIDTH), BF16),
        jax.ShapeDtypeStruct((b, s, RET_WIDTH), BF16),
        jax.ShapeDtypeStruct((b, s, MOBA_WIDTH), BF16),
        jax.ShapeDtypeStruct((b, ns, bpt, MOBA_WIDTH), F32),
        jax.ShapeDtypeStruct((b, s // RET_CHUNK, RET_QK_WIDTH, RET_CHUNK), BF16),
        jax.ShapeDtypeStruct((b, s // MOBA_BLOCK, MOBA_WIDTH, MOBA_BLOCK), BF16),
        jax.ShapeDtypeStruct((b, s // MOBA_BLOCK, MOBA_WIDTH, MOBA_BLOCK), BF16),
    )
    tok = lambda w: pl.BlockSpec((1, tm, w), lambda si, bi: (bi, si, 0))
    return pl.pallas_call(
        _in_proj_kernel,
        grid=(ns, b),
        in_specs=[
            tok(d),
            pl.BlockSpec((1, d), const),
            pl.BlockSpec((d, NAT_WIDTH), const),
            pl.BlockSpec((TR_WIDTH, d), const),
            pl.BlockSpec((tm, RET_QK_WIDTH), lambda si, bi: (si, 0)),
            pl.BlockSpec((tm, RET_QK_WIDTH), lambda si, bi: (si, 0)),
            pl.BlockSpec((RET_QK_DIM // 2, tm), lambda si, bi: (0, si)),
            pl.BlockSpec((RET_QK_DIM // 2, tm), lambda si, bi: (0, si)),
        ],
        out_specs=(
            tok(RET_QK_WIDTH), tok(RET_WIDTH), tok(RET_WIDTH), tok(MOBA_WIDTH),
            pl.BlockSpec((1, 1, bpt, MOBA_WIDTH), lambda si, bi: (bi, si, 0, 0)),
            pl.BlockSpec((1, cpt, RET_QK_WIDTH, RET_CHUNK), lambda si, bi: (bi, si, 0, 0)),
            pl.BlockSpec((1, bpt, MOBA_WIDTH, MOBA_BLOCK), lambda si, bi: (bi, si, 0, 0)),
            pl.BlockSpec((1, bpt, MOBA_WIDTH, MOBA_BLOCK), lambda si, bi: (bi, si, 0, 0)),
        ),
        out_shape=out_shape,
        compiler_params=pltpu.CompilerParams(
            dimension_semantics=("arbitrary", "arbitrary"), vmem_limit_bytes=VMEM_LIMIT),
        name="in_proj",
    )(x, gain, w_nat, w_tr, cq, sq, ckt, skt)


def _retention_kernel(rq_ref, rkt_ref, rv_ref, sg_ref, dmat_ref, qd_ref, kd_ref, cd_ref,
                      out_ref, state_ref):
    @pl.when(pl.program_id(1) == 0)
    def _():
        state_ref[...] = jnp.zeros_like(state_ref)

    n_chunks = rq_ref.shape[1] // RET_CHUNK

    def chunk(c, carry):
        r0 = pl.multiple_of(c * RET_CHUNK, RET_CHUNK)
        rows = pl.ds(r0, RET_CHUNK)
        for hd in range(RET_HEADS):
            q = rq_ref[0, rows, hd * RET_QK_DIM:(hd + 1) * RET_QK_DIM]
            kt = rkt_ref[0, c, hd * RET_QK_DIM:(hd + 1) * RET_QK_DIM, :]
            v = rv_ref[0, rows, hd * RET_V_DIM:(hd + 1) * RET_V_DIM]
            state = state_ref[hd]
            qk_qs = _dot(q, jnp.concatenate([kt, state.astype(BF16)], axis=1))
            scores = qk_qs[:, :RET_CHUNK] * dmat_ref[hd]
            o = _dot(scores.astype(BF16), v) + qd_ref[hd] * qk_qs[:, RET_CHUNK:]
            kts = (kt.astype(F32) * kd_ref[hd]).astype(BF16)
            state_ref[hd] = state * cd_ref[hd] + _dot(kts, v)
            ms = jnp.mean(o * o, axis=-1, keepdims=True)
            gate = sg_ref[0, rows, hd * RET_V_DIM:(hd + 1) * RET_V_DIM].astype(F32)
            out_ref[0, rows, hd * RET_V_DIM:(hd + 1) * RET_V_DIM] = (
                o * lax.rsqrt(ms + EPS) * gate).astype(BF16)
        return carry

    lax.fori_loop(0, n_chunks, chunk, 0, unroll=8)


def _retention(rq, rkt, rv, sg, dmat, qd, kd, cd):
    b, s, _ = rq.shape
    tc = min(RET_TOKENS, s)
    tok = lambda w: pl.BlockSpec((1, tc, w), lambda bi, si: (bi, si, 0))
    tab = lambda a: pl.BlockSpec(a.shape, lambda bi, si: (0,) * a.ndim)
    return pl.pallas_call(
        _retention_kernel,
        grid=(b, s // tc),
        in_specs=[
            tok(RET_QK_WIDTH),
            pl.BlockSpec((1, tc // RET_CHUNK, RET_QK_WIDTH, RET_CHUNK), lambda bi, si: (bi, si, 0, 0)),
            tok(RET_WIDTH), tok(RET_WIDTH),
            tab(dmat), tab(qd), tab(kd), tab(cd),
        ],
        out_specs=tok(RET_WIDTH),
        out_shape=jax.ShapeDtypeStruct((b, s, RET_WIDTH), BF16),
        scratch_shapes=[pltpu.VMEM((RET_HEADS, RET_QK_DIM, RET_V_DIM), F32)],
        compiler_params=pltpu.CompilerParams(
            dimension_semantics=("arbitrary", "arbitrary"), vmem_limit_bytes=VMEM_LIMIT),
        name="retention",
    )(rq, rkt, rv, sg, dmat, qd, kd, cd)


ONES_ROWS = 16
PAIR_WIDTH = 2 * MOBA_HEAD_DIM
GATE_BLOCKS = 4


def _moba_gate_kernel(cfar_ref, qt_ref, km_ref, selfar_ref, selprev_ref):
    d = MOBA_HEAD_DIM
    nb = km_ref.shape[2]
    n_q = qt_ref.shape[1]
    shape = (nb, n_q * MOBA_BLOCK)
    blk = lax.broadcasted_iota(jnp.int32, shape, 0)
    i = pl.program_id(1) * n_q + lax.broadcasted_iota(jnp.int32, shape, 1) // MOBA_BLOCK
    blk_f = blk.astype(F32)
    for h in range(MOBA_HEADS):
        qt = jnp.concatenate([qt_ref[0, c, h * d:(h + 1) * d, :] for c in range(n_q)], axis=1)
        km = km_ref[0, h]
        km_hi = km.astype(BF16)
        km_lo = (km - km_hi.astype(F32)).astype(BF16)
        gate = _dot(km_hi, qt) + _dot(km_lo, qt)
        gate = jnp.where(blk < i, gate, NEG)
        chosen = jnp.zeros(gate.shape, F32)
        for _ in range(MOBA_TOPK):
            best = jnp.max(gate, axis=0, keepdims=True)
            idx = jnp.min(jnp.where(gate == best, blk_f, float(nb)), axis=0, keepdims=True)
            hit = blk_f == idx
            chosen = jnp.where(hit & (best > 0.5 * NEG), 1.0, chosen)
            gate = jnp.where(hit, NEG, gate)
        picked = chosen > 0.5
        selfar_ref[0, h] = jnp.where(picked & (blk < i - 1), cfar_ref[h], NEG).astype(BF16)
        prev_hit = jnp.max(jnp.where(picked & (blk == i - 1), 1.0, 0.0), axis=0, keepdims=True)
        selprev_ref[0, h] = jnp.where(prev_hit > 0.5, 0.0, NEG)


def _moba_gate(cfar, mqt, kmean):
    b, nb, _, _ = mqt.shape
    s = nb * MOBA_BLOCK
    n_q = math.gcd(GATE_BLOCKS, nb)
    width = n_q * MOBA_BLOCK
    return pl.pallas_call(
        _moba_gate_kernel,
        grid=(b, nb // n_q),
        in_specs=[
            pl.BlockSpec(memory_space=pltpu.SMEM),
            pl.BlockSpec((1, n_q, MOBA_WIDTH, MOBA_BLOCK), lambda bi, i: (bi, i, 0, 0)),
            pl.BlockSpec((1, MOBA_HEADS, nb, MOBA_HEAD_DIM), lambda bi, i: (bi, 0, 0, 0)),
        ],
        out_specs=(
            pl.BlockSpec((1, MOBA_HEADS, nb, width), lambda bi, i: (bi, 0, 0, i)),
            pl.BlockSpec((1, MOBA_HEADS, 1, width), lambda bi, i: (bi, 0, 0, i)),
        ),
        out_shape=(
            jax.ShapeDtypeStruct((b, MOBA_HEADS, nb, s), BF16),
            jax.ShapeDtypeStruct((b, MOBA_HEADS, 1, s), F32),
        ),
        compiler_params=pltpu.CompilerParams(
            dimension_semantics=("arbitrary", "arbitrary"), vmem_limit_bytes=VMEM_LIMIT),
        name="moba_gate",
    )(cfar, mqt, kmean)


def _moba_kernel(qt_ref, k_ref, vt_ref, selfar_ref, selprev_ref, tnear_ref,
                 out_ref, kaug_ref, vaug_ref, qaug_ref, qnear_ref, s_ref, m_ref, acc_ref):
    i = pl.program_id(1)
    d = MOBA_HEAD_DIM
    nb = vaug_ref.shape[1]
    s_len = kaug_ref.shape[1]
    blk_rows = MOBA_BLOCK
    n_pairs = MOBA_HEADS // 2

    @pl.when((pl.program_id(0) == 0) & (i == 0))
    def _():
        row_blk = lax.broadcasted_iota(jnp.int32, (s_len, 2 * PAIR_WIDTH), 0) // MOBA_BLOCK
        lane = lax.broadcasted_iota(jnp.int32, (s_len, 2 * PAIR_WIDTH), 1)
        hot_lane = jnp.where(lane < PAIR_WIDTH, lane - d, lane - PAIR_WIDTH)
        in_band = (lane >= d) & (lane < PAIR_WIDTH + d)
        pattern = jnp.where(in_band & (row_blk == hot_lane), 1.0, 0.0).astype(BF16)
        for pe in range(n_pairs):
            kaug_ref[pe] = pattern
        for e in range(MOBA_HEADS):
            vaug_ref[e, :, d:, :] = jnp.ones((nb, ONES_ROWS, MOBA_BLOCK), BF16)

    own_rows = pl.ds(pl.multiple_of(i * blk_rows, blk_rows), blk_rows)
    for pe in range(n_pairs):
        kaug_ref[pe, own_rows, 0:d] = k_ref[0, :, pe * PAIR_WIDTH:pe * PAIR_WIDTH + d]
        kaug_ref[pe, own_rows, PAIR_WIDTH + d:] = k_ref[0, :, pe * PAIR_WIDTH + d:(pe + 1) * PAIR_WIDTH]
    for e in range(MOBA_HEADS):
        vaug_ref[e, i, 0:d, :] = vt_ref[0, 0, e * d:(e + 1) * d, :]

    prow = lax.broadcasted_iota(jnp.int32, (PAIR_WIDTH, MOBA_BLOCK), 0)
    for e in range(MOBA_HEADS):
        pe, he = divmod(e, 2)
        qt2 = qt_ref[0, 0, pe * PAIR_WIDTH:(pe + 1) * PAIR_WIDTH, :]
        q_only = jnp.where((prow >= he * d) & (prow < (he + 1) * d), qt2, jnp.zeros_like(qt2))
        qnear_ref[e] = q_only
        qaug_ref[e] = q_only
        hot_row = d if he == 0 else 0
        qaug_ref[e, hot_row:hot_row + nb, :] = selfar_ref[0, e]

    def qk(e, j, slot, near=False):
        rows = pl.ds(pl.multiple_of(j * blk_rows, blk_rows), blk_rows)
        keys = kaug_ref[e // 2, rows, (e % 2) * PAIR_WIDTH:(e % 2 + 1) * PAIR_WIDTH]
        s_ref[slot, e] = _dot(keys, qnear_ref[e] if near else qaug_ref[e])

    def sm(e, j, slot, one_sweep, bias=None, first=False):
        st = s_ref[slot, e]
        if bias is not None:
            st = st + bias
        if first:
            mj = jnp.max(st, axis=0, keepdims=True)
            m_ref[e] = mj
            acc_ref[e] = _dot(vaug_ref[e, j], jnp.exp2(st - mj).astype(BF16))
        elif one_sweep:
            m_old = m_ref[e]
            p = jnp.exp2(st - m_old).astype(BF16)
            grow = jnp.maximum(jnp.max(p, axis=0, keepdims=True).astype(F32), 1.0)
            acc_ref[e] = (acc_ref[e] + _dot(vaug_ref[e, j], p)) / grow
            m_ref[e] = m_old + jnp.log2(grow)
        else:
            m_old = m_ref[e]
            m_new = jnp.maximum(m_old, jnp.max(st, axis=0, keepdims=True))
            pv = _dot(vaug_ref[e, j], jnp.exp2(st - m_new).astype(BF16))
            acc_ref[e] = acc_ref[e] * jnp.exp2(m_old - m_new) + pv
            m_ref[e] = m_new

    clamp = lambda j: jnp.minimum(j, i)
    heads = range(MOBA_HEADS)
    jp = jnp.maximum(i - 1, 0)
    n_full = lax.shift_right_logical(i, 2)
    has_tail = (i & 3) >= 2

    def attend(one_sweep):
        for blk, slot, near in ((i, 0, True), (jp, 1, True), (clamp(0), 2, False), (clamp(1), 3, False)):
            for e in heads:
                qk(e, blk, slot, near=near)
        for e in heads:
            sm(e, i, 0, one_sweep, bias=tnear_ref[e, blk_rows:, :], first=True)
        for e in heads:
            sm(e, jp, 1, one_sweep, bias=tnear_ref[e, 0:blk_rows, :] + selprev_ref[0, e])

        def far(t, carry):
            j = 4 * t
            for half in (0, 2):
                for e in heads:
                    qk(e, clamp(j + half + 2), half)
                for e in heads:
                    sm(e, clamp(j + half), (half + 2) % 4, one_sweep)
                for e in heads:
                    qk(e, clamp(j + half + 3), half + 1)
                for e in heads:
                    sm(e, clamp(j + half + 1), (half + 3) % 4, one_sweep)
            return carry

        lax.fori_loop(0, n_full, far, 0)

        @pl.when(has_tail)
        def _():
            for e in heads:
                sm(e, clamp(4 * n_full), 2, one_sweep)
            for e in heads:
                sm(e, clamp(4 * n_full + 1), 3, one_sweep)

    attend(one_sweep=True)
    check = jnp.sum(acc_ref[...]) + jnp.sum(m_ref[...])
    pl.when(jnp.logical_not(jnp.abs(check) < jnp.inf))(lambda: attend(one_sweep=False))

    outs = []
    for e in range(MOBA_HEADS):
        acc = acc_ref[e]
        outs.append((acc[:d] / acc[d:d + 1]).T)
    out_ref[0] = jnp.concatenate(outs, axis=1).astype(BF16)


def _moba(mqt, mk, mvt, selfar, selprev, tnear):
    b, s, _ = mk.shape
    nb = s // MOBA_BLOCK
    assert nb <= MOBA_HEAD_DIM
    nh = MOBA_HEADS
    return pl.pallas_call(
        _moba_kernel,
        grid=(b, nb),
        in_specs=[
            pl.BlockSpec((1, 1, MOBA_WIDTH, MOBA_BLOCK), lambda bi, i: (bi, i, 0, 0)),
            pl.BlockSpec((1, MOBA_BLOCK, MOBA_WIDTH), lambda bi, i: (bi, i, 0)),
            pl.BlockSpec((1, 1, MOBA_WIDTH, MOBA_BLOCK), lambda bi, i: (bi, i, 0, 0)),
            pl.BlockSpec((1, nh, nb, MOBA_BLOCK), lambda bi, i: (bi, 0, 0, i)),
            pl.BlockSpec((1, nh, 1, MOBA_BLOCK), lambda bi, i: (bi, 0, 0, i)),
            pl.BlockSpec((nh, 2 * MOBA_BLOCK, MOBA_BLOCK), lambda bi, i: (0, 0, 0),
                         pipeline_mode=pl.Buffered(1)),
        ],
        out_specs=pl.BlockSpec((1, MOBA_BLOCK, MOBA_WIDTH), lambda bi, i: (bi, i, 0)),
        out_shape=jax.ShapeDtypeStruct((b, s, MOBA_WIDTH), BF16),
        scratch_shapes=[
            pltpu.VMEM((nh // 2, s, 2 * PAIR_WIDTH), BF16),
            pltpu.VMEM((nh, nb, MOBA_HEAD_DIM + ONES_ROWS, MOBA_BLOCK), BF16),
            pltpu.VMEM((nh, PAIR_WIDTH, MOBA_BLOCK), BF16),
            pltpu.VMEM((nh, PAIR_WIDTH, MOBA_BLOCK), BF16),
            pltpu.VMEM((4, nh, MOBA_BLOCK, MOBA_BLOCK), F32),
            pltpu.VMEM((nh, 1, MOBA_BLOCK), F32),
            pltpu.VMEM((nh, MOBA_HEAD_DIM + ONES_ROWS, MOBA_BLOCK), F32),
        ],
        compiler_params=pltpu.CompilerParams(
            dimension_semantics=("arbitrary", "arbitrary"), vmem_limit_bytes=VMEM_LIMIT),
        name="moba",
    )(mqt, mk, mvt, selfar, selprev, tnear)


def _rms(x, g):
    return x * lax.rsqrt(jnp.mean(x * x, axis=-1, keepdims=True) + EPS) * g


def _out_ffn_kernel(x_ref, ret_ref, moba_ref, wo_ref, wg_ref, wu_ref, wd_ref,
                    g_post_mix_ref, g_pre_ffn_ref, g_post_ffn_ref, out_ref):
    tm = x_ref.shape[0]
    groups = [slice(r0, r0 + tm // FFN_ROW_GROUPS) for r0 in range(0, tm, tm // FFN_ROW_GROUPS)]
    mixes = [_dot(jnp.concatenate([ret_ref[rows, :], moba_ref[rows, :]], axis=1), wo_ref[...])
             for rows in groups]
    for rows, mix in zip(groups, mixes):
        x1 = x_ref[rows, :] + _rms(mix, g_post_mix_ref[...])
        h = _rms(x1, g_pre_ffn_ref[...]).astype(BF16)
        f = None
        for lo, hi in FFN_CHUNKS:
            cols = slice(lo, hi)
            gate = _dot(h, wg_ref[:, cols])
            up = _dot(h, wu_ref[:, cols])
            act = (gate * jax.nn.sigmoid(gate) * up).astype(BF16)
            part = _dot(act, wd_ref[cols, :])
            f = part if f is None else f + part
        out_ref[rows, :] = x1 + _rms(f, g_post_ffn_ref[...])


def _out_ffn(x2, ret2, moba2, wo, wg, wu, wd, g_post_mix, g_pre_ffn, g_post_ffn):
    n, d = x2.shape
    tm = min(FFN_TOKENS, n)
    tok = lambda w: pl.BlockSpec((tm, w), lambda t: (t, 0))
    resident = lambda a: pl.BlockSpec(a.shape, lambda t: (0, 0), pipeline_mode=pl.Buffered(1))
    return pl.pallas_call(
        _out_ffn_kernel,
        grid=(n // tm,),
        in_specs=[tok(d), tok(RET_WIDTH), tok(MOBA_WIDTH),
                  resident(wo), resident(wg), resident(wu), resident(wd),
                  resident(g_post_mix), resident(g_pre_ffn), resident(g_post_ffn)],
        out_specs=tok(d),
        out_shape=jax.ShapeDtypeStruct((n, d), F32),
        compiler_params=pltpu.CompilerParams(
            dimension_semantics=("arbitrary",), vmem_limit_bytes=VMEM_LIMIT),
        name="out_ffn",
    )(x2, ret2, moba2, wo, wg, wu, wd, g_post_mix, g_pre_ffn, g_post_ffn)


def _rotary_tables(s):
    half = RET_QK_DIM // 2
    inv_freq = ROPE_BASE ** (-np.arange(half, dtype=np.float64) / half)
    ang = np.arange(s, dtype=np.float64)[:, None] * inv_freq[None, :]
    cos, sin = np.cos(ang), np.sin(ang)
    cq = np.tile(np.concatenate([cos, cos], axis=1), (1, RET_HEADS))
    sq = np.tile(np.concatenate([-sin, sin], axis=1), (1, RET_HEADS))
    return tuple(np.ascontiguousarray(t, dtype=np.float32) for t in (cq, sq, cos.T, sin.T))


def _retention_tables():
    c = RET_CHUNK
    log_gamma = np.log1p(-np.exp(np.linspace(math.log(1.0 / 32), math.log(1.0 / 512), RET_HEADS)))
    idx = np.arange(c, dtype=np.float64)
    diff = idx[:, None] - idx[None, :]
    dmat = np.where(diff >= 0, np.exp(np.maximum(diff, 0.0)[None] * log_gamma[:, None, None]), 0.0)
    q_decay = np.exp((idx + 1.0)[None, :] * log_gamma[:, None])
    k_decay = np.exp((c - 1.0 - idx)[None, :] * log_gamma[:, None])
    chunk_decay = np.exp(c * log_gamma)
    qd = np.broadcast_to(q_decay[:, :, None], (RET_HEADS, c, RET_V_DIM))
    kd = k_decay[:, None, :]
    cd = np.broadcast_to(chunk_decay[:, None, None], (RET_HEADS, 1, RET_V_DIM))
    return tuple(np.ascontiguousarray(t, dtype=np.float32) for t in (dmat, qd, kd, cd))


def _t5_bucket(rel):
    n = np.maximum(rel, 0)
    max_exact = REL_BUCKETS // 2
    large = max_exact + (np.log(np.maximum(n, 1) / max_exact) / math.log(REL_MAX_DIST / max_exact)
                         * (REL_BUCKETS - max_exact)).astype(np.int64)
    return np.where(n < max_exact, n, np.minimum(large, REL_BUCKETS - 1))


def _toeplitz(vec):
    h, two_n = vec.shape
    n = two_n // 2
    ext = jnp.concatenate([vec, jnp.zeros((h, 1), vec.dtype)], axis=1)
    skew = jnp.tile(ext, (1, n))[:, :n * two_n].reshape(h, n, two_n)
    return skew[:, :, n:]


def _bias_tables(rel_bias):
    bias_t = rel_bias.T.astype(F32) * LOG2E
    n = MOBA_BLOCK
    one_hot = (_t5_bucket(np.arange(2 * n))[:, None] == np.arange(REL_BUCKETS)).astype(np.float32)
    by_dist = jnp.einsum("rn,hn->hr", one_hot, bias_t, precision=lax.Precision.HIGHEST)
    tprev = _toeplitz(by_dist)
    own_vec = jnp.concatenate([jnp.full((bias_t.shape[0], n), NEG, F32), by_dist[:, :n]], axis=1)
    town = _toeplitz(own_vec)
    cfar = bias_t[:, REL_BUCKETS - 1]
    return jnp.concatenate([tprev, town], axis=1), cfar


def kernel(x, w_in, w_out, pre_mix_norm, post_mix_norm, pre_ffn_norm, post_ffn_norm,
           rel_bias, w_gate, w_up, w_down):
    b, s, d = x.shape
    assert d == D_MODEL and s % IN_TOKENS == 0 and MOBA_BLOCK + 1 >= REL_MAX_DIST
    depth = w_in.shape[0]

    cq, sq, ckt, skt = _rotary_tables(s)
    dmat, qd, kd, cd = _retention_tables()
    tnear, cfar = _bias_tables(rel_bias)

    sizes = [RET_QK_WIDTH, RET_QK_WIDTH, RET_WIDTH, RET_WIDTH, MOBA_WIDTH, MOBA_WIDTH, MOBA_WIDTH]
    o_rq, o_rk, o_rv, o_rg, o_mq, o_mk, o_mv, _ = np.cumsum([0] + sizes).tolist()

    for layer in range(depth):
        w = w_in[layer]
        w_nat = jnp.concatenate(
            [w[:, o_rq:o_rk], w[:, o_rv:o_mq], w[:, o_mk:o_mv]], axis=1).astype(BF16)
        w_tr = jnp.concatenate([w[:, o_rk:o_rv], w[:, o_mq:o_mk], w[:, o_mv:]], axis=1).T.astype(BF16)

        rq, rv, sg, mk, kmean, rkt, mqt, mvt = _in_proj(
            x, pre_mix_norm[layer][None, :], w_nat, w_tr, cq, sq, ckt, skt)

        ret = _retention(rq, rkt, rv, sg, dmat, qd, kd, cd)

        nb = s // MOBA_BLOCK
        kmean = kmean.reshape(b, nb, MOBA_HEADS, MOBA_HEAD_DIM).transpose(0, 2, 1, 3)
        selfar, selprev = _moba_gate(cfar, mqt, kmean)
        moba = _moba(mqt, mk, mvt, selfar, selprev, tnear)

        x = _out_ffn(
            x.reshape(b * s, d), ret.reshape(b * s, RET_WIDTH), moba.reshape(b * s, MOBA_WIDTH),
            w_out[layer].astype(BF16), w_gate[layer].astype(BF16), w_up[layer].astype(BF16),
            w_down[layer].astype(BF16), post_mix_norm[layer][None, :], pre_ffn_norm[layer][None, :],
            post_ffn_norm[layer][None, :]).reshape(b, s, d)
    return x
```

```python
import functools
import math

import jax
import jax.numpy as jnp
import numpy as np
from jax import lax
from jax.experimental import pallas as pl
from jax.experimental.pallas import tpu as pltpu

F32 = jnp.float32
BF16 = jnp.bfloat16

D_MODEL = 1024
RET_HEADS = 4
RET_QK_DIM = 64
RET_V_DIM = 128
RET_CHUNK = 128
RET_QK_WIDTH = RET_HEADS * RET_QK_DIM
RET_WIDTH = RET_HEADS * RET_V_DIM
MOBA_HEADS = 8
MOBA_HEAD_DIM = 64
MOBA_WIDTH = MOBA_HEADS * MOBA_HEAD_DIM
MOBA_BLOCK = 256
MOBA_TOPK = 3
REL_BUCKETS = 32
REL_MAX_DIST = 128
D_FF = 2816
EPS = 1e-6
ROPE_BASE = 10000.0

NEG = -1e30
LOG2E = math.log2(math.e)

IN_TOKENS = 1024
IN_ROW_GROUPS = 2
RET_TOKENS = 1024
FFN_TOKENS = 1024
FFN_ROW_GROUPS = 4
MXU_TILE = 256
FFN_CHUNKS = ((0, 6 * MXU_TILE), (6 * MXU_TILE, D_FF))
VMEM_LIMIT = 56 * 1024 * 1024

NAT_WIDTH = RET_QK_WIDTH + 2 * RET_WIDTH + MOBA_WIDTH
TR_WIDTH = RET_QK_WIDTH + 2 * MOBA_WIDTH


def _nt_dot(a, b):
    return lax.dot_general(a, b, (((1,), (1,)), ((), ())), preferred_element_type=F32)


def _dot(a, b):
    return jnp.dot(a, b, preferred_element_type=F32)


def _in_proj_kernel(x_ref, g_ref, wn_ref, wt_ref, cq_ref, sq_ref, ckt_ref, skt_ref,
                    rq_ref, rv_ref, sg_ref, mk_ref, kmean_ref, rkt_ref, mqt_ref, mvt_ref):
    tm = x_ref.shape[1] // IN_ROW_GROUPS
    hs = []
    for grp in range(IN_ROW_GROUPS):
        x = x_ref[0, grp * tm:(grp + 1) * tm, :]
        ms = jnp.mean(x * x, axis=-1, keepdims=True)
        hs.append((x * lax.rsqrt(ms + EPS) * g_ref[...]).astype(BF16))

    for grp, h in enumerate(hs):
        rows = slice(grp * tm, (grp + 1) * tm)
        pn = _dot(h, wn_ref[...])
        pt = _nt_dot(wt_ref[...], h)

        rq = pn[:, :RET_QK_WIDTH]
        lane = lax.broadcasted_iota(jnp.int32, rq.shape, 1)
        first_half = (lane % RET_QK_DIM) < (RET_QK_DIM // 2)
        partner = jnp.where(first_half,
                            pltpu.roll(rq, RET_QK_WIDTH - RET_QK_DIM // 2, 1),
                            pltpu.roll(rq, RET_QK_DIM // 2, 1))
        rq_ref[0, rows, :] = (rq * cq_ref[rows, :] + partner * sq_ref[rows, :]).astype(BF16)

        rv_ref[0, rows, :] = pn[:, RET_QK_WIDTH:RET_QK_WIDTH + RET_WIDTH].astype(BF16)
        rg = pn[:, RET_QK_WIDTH + RET_WIDTH:RET_QK_WIDTH + 2 * RET_WIDTH]
        sg_ref[0, rows, :] = (rg * jax.nn.sigmoid(rg)).astype(BF16)

        mk = pn[:, RET_QK_WIDTH + 2 * RET_WIDTH:]
        mk_ref[0, rows, :] = mk.astype(BF16)
        blk0 = grp * (tm // MOBA_BLOCK)
        for blk in range(tm // MOBA_BLOCK):
            kmean_ref[0, 0, blk0 + blk:blk0 + blk + 1, :] = jnp.mean(
                mk[blk * MOBA_BLOCK:(blk + 1) * MOBA_BLOCK], axis=0, keepdims=True)

        half = RET_QK_DIM // 2
        cos_t = ckt_ref[:, rows]
        sin_t = skt_ref[:, rows]
        parts = []
        for hd in range(RET_HEADS):
            x1 = pt[hd * RET_QK_DIM:hd * RET_QK_DIM + half]
            x2 = pt[hd * RET_QK_DIM + half:(hd + 1) * RET_QK_DIM]
            parts.append(x1 * cos_t - x2 * sin_t)
            parts.append(x2 * cos_t + x1 * sin_t)
        rkt = (jnp.concatenate(parts, axis=0) * (RET_QK_DIM ** -0.5)).astype(BF16)
        c0 = grp * (tm // RET_CHUNK)
        for c in range(tm // RET_CHUNK):
            rkt_ref[0, c0 + c] = rkt[:, c * RET_CHUNK:(c + 1) * RET_CHUNK]

        mqt = (pt[RET_QK_WIDTH:RET_QK_WIDTH + MOBA_WIDTH] * (MOBA_HEAD_DIM ** -0.5 * LOG2E)).astype(BF16)
        mvt = pt[RET_QK_WIDTH + MOBA_WIDTH:].astype(BF16)
        for blk in range(tm // MOBA_BLOCK):
            mqt_ref[0, blk0 + blk] = mqt[:, blk * MOBA_BLOCK:(blk + 1) * MOBA_BLOCK]
            mvt_ref[0, blk0 + blk] = mvt[:, blk * MOBA_BLOCK:(blk + 1) * MOBA_BLOCK]


def _in_proj(x, gain, w_nat, w_tr, cq, sq, ckt, skt):
    b, s, d = x.shape
    tm = IN_TOKENS
    ns = s // tm
    bpt = tm // MOBA_BLOCK
    cpt = tm // RET_CHUNK
    const = lambda si, bi: (0, 0)
    out_shape = (
        jax.ShapeDtypeStruct((b, s, RET_QK_WIDTH), BF16),
        jax.ShapeDtypeStruct((b, s, RET_WIDTH), BF16),
        jax.ShapeDtypeStruct((b, s, RET_WIDTH), BF16),
        jax.ShapeDtypeStruct((b, s, MOBA_WIDTH), BF16),
        jax.ShapeDtypeStruct((b, ns, bpt, MOBA_WIDTH), F32),
        jax.ShapeDtypeStruct((b, s // RET_CHUNK, RET_QK_WIDTH, RET_CHUNK), BF16),
        jax.ShapeDtypeStruct((b, s // MOBA_BLOCK, MOBA_WIDTH, MOBA_BLOCK), BF16),
        jax.ShapeDtypeStruct((b, s // MOBA_BLOCK, MOBA_WIDTH, MOBA_BLOCK), BF16),
    )
    tok = lambda w: pl.BlockSpec((1, tm, w), lambda si, bi: (bi, si, 0))
    return pl.pallas_call(
        _in_proj_kernel,
        grid=(ns, b),
        in_specs=[
            tok(d),
            pl.BlockSpec((1, d), const),
            pl.BlockSpec((d, NAT_WIDTH), const),
            pl.BlockSpec((TR_WIDTH, d), const),
            pl.BlockSpec((tm, RET_QK_WIDTH), lambda si, bi: (si, 0)),
            pl.BlockSpec((tm, RET_QK_WIDTH), lambda si, bi: (si, 0)),
            pl.BlockSpec((RET_QK_DIM // 2, tm), lambda si, bi: (0, si)),
            pl.BlockSpec((RET_QK_DIM // 2, tm), lambda si, bi: (0, si)),
        ],
        out_specs=(
            tok(RET_QK_WIDTH), tok(RET_WIDTH), tok(RET_WIDTH), tok(MOBA_WIDTH),
            pl.BlockSpec((1, 1, bpt, MOBA_WIDTH), lambda si, bi: (bi, si, 0, 0)),
            pl.BlockSpec((1, cpt, RET_QK_WIDTH, RET_CHUNK), lambda si, bi: (bi, si, 0, 0)),
            pl.BlockSpec((1, bpt, MOBA_WIDTH, MOBA_BLOCK), lambda si, bi: (bi, si, 0, 0)),
            pl.BlockSpec((1, bpt, MOBA_WIDTH, MOBA_BLOCK), lambda si, bi: (bi, si, 0, 0)),
        ),
        out_shape=out_shape,
        compiler_params=pltpu.CompilerParams(
            dimension_semantics=("arbitrary", "arbitrary"), vmem_limit_bytes=VMEM_LIMIT),
        name="in_proj",
    )(x, gain, w_nat, w_tr, cq, sq, ckt, skt)


def _retention_kernel(rq_ref, rkt_ref, rv_ref, sg_ref, dmat_ref, qd_ref, kd_ref, cd_ref,
                      out_ref, state_ref):
    @pl.when(pl.program_id(1) == 0)
    def _():
        state_ref[...] = jnp.zeros_like(state_ref)

    n_chunks = rq_ref.shape[1] // RET_CHUNK

    def chunk(c, carry):
        r0 = pl.multiple_of(c * RET_CHUNK, RET_CHUNK)
        rows = pl.ds(r0, RET_CHUNK)
        for hd in range(RET_HEADS):
            q = rq_ref[0, rows, hd * RET_QK_DIM:(hd + 1) * RET_QK_DIM]
            kt = rkt_ref[0, c, hd * RET_QK_DIM:(hd + 1) * RET_QK_DIM, :]
            v = rv_ref[0, rows, hd * RET_V_DIM:(hd + 1) * RET_V_DIM]
            state = state_ref[hd]
            qk_qs = _dot(q, jnp.concatenate([kt, state.astype(BF16)], axis=1))
            scores = qk_qs[:, :RET_CHUNK] * dmat_ref[hd]
            o = _dot(scores.astype(BF16), v) + qd_ref[hd] * qk_qs[:, RET_CHUNK:]
            kts = (kt.astype(F32) * kd_ref[hd]).astype(BF16)
            state_ref[hd] = state * cd_ref[hd] + _dot(kts, v)
            ms = jnp.mean(o * o, axis=-1, keepdims=True)
            gate = sg_ref[0, rows, hd * RET_V_DIM:(hd + 1) * RET_V_DIM].astype(F32)
            out_ref[0, rows, hd * RET_V_DIM:(hd + 1) * RET_V_DIM] = (
                o * lax.rsqrt(ms + EPS) * gate).astype(BF16)
        return carry

    lax.fori_loop(0, n_chunks, chunk, 0, unroll=8)


def _retention(rq, rkt, rv, sg, dmat, qd, kd, cd):
    b, s, _ = rq.shape
    tc = min(RET_TOKENS, s)
    tok = lambda w: pl.BlockSpec((1, tc, w), lambda bi, si: (bi, si, 0))
    tab = lambda a: pl.BlockSpec(a.shape, lambda bi, si: (0,) * a.ndim)
    return pl.pallas_call(
        _retention_kernel,
        grid=(b, s // tc),
        in_specs=[
            tok(RET_QK_WIDTH),
            pl.BlockSpec((1, tc // RET_CHUNK, RET_QK_WIDTH, RET_CHUNK), lambda bi, si: (bi, si, 0, 0)),
            tok(RET_WIDTH), tok(RET_WIDTH),
            tab(dmat), tab(qd), tab(kd), tab(cd),
        ],
        out_specs=tok(RET_WIDTH),
        out_shape=jax.ShapeDtypeStruct((b, s, RET_WIDTH), BF16),
        scratch_shapes=[pltpu.VMEM((RET_HEADS, RET_QK_DIM, RET_V_DIM), F32)],
        compiler_params=pltpu.CompilerParams(
            dimension_semantics=("arbitrary", "arbitrary"), vmem_limit_bytes=VMEM_LIMIT),
        name="retention",
    )(rq, rkt, rv, sg, dmat, qd, kd, cd)


ONES_ROWS = 16
PAIR_WIDTH = 2 * MOBA_HEAD_DIM
GATE_BLOCKS = 4


def _moba_gate_kernel(cfar_ref, qt_ref, km_ref, selfar_ref, selprev_ref):
    d = MOBA_HEAD_DIM
    nb = km_ref.shape[2]
    n_q = qt_ref.shape[1]
    shape = (nb, n_q * MOBA_BLOCK)
    blk = lax.broadcasted_iota(jnp.int32, shape, 0)
    i = pl.program_id(1) * n_q + lax.broadcasted_iota(jnp.int32, shape, 1) // MOBA_BLOCK
    blk_f = blk.astype(F32)
    for h in range(MOBA_HEADS):
        qt = jnp.concatenate([qt_ref[0, c, h * d:(h + 1) * d, :] for c in range(n_q)], axis=1)
        km = km_ref[0, h]
        km_hi = km.astype(BF16)
        km_lo = (km - km_hi.astype(F32)).astype(BF16)
        gate = _dot(km_hi, qt) + _dot(km_lo, qt)
        gate = jnp.where(blk < i, gate, NEG)
        chosen = jnp.zeros(gate.shape, F32)
        for _ in range(MOBA_TOPK):
            best = jnp.max(gate, axis=0, keepdims=True)
            idx = jnp.min(jnp.where(gate == best, blk_f, float(nb)), axis=0, keepdims=True)
            hit = blk_f == idx
            chosen = jnp.where(hit & (best > 0.5 * NEG), 1.0, chosen)
            gate = jnp.where(hit, NEG, gate)
        picked = chosen > 0.5
        selfar_ref[0, h] = jnp.where(picked & (blk < i - 1), cfar_ref[h], NEG).astype(BF16)
        prev_hit = jnp.max(jnp.where(picked & (blk == i - 1), 1.0, 0.0), axis=0, keepdims=True)
        selprev_ref[0, h] = jnp.where(prev_hit > 0.5, 0.0, NEG)


def _moba_gate(cfar, mqt, kmean):
    b, nb, _, _ = mqt.shape
    s = nb * MOBA_BLOCK
    n_q = math.gcd(GATE_BLOCKS, nb)
    width = n_q * MOBA_BLOCK
    return pl.pallas_call(
        _moba_gate_kernel,
        grid=(b, nb // n_q),
        in_specs=[
            pl.BlockSpec(memory_space=pltpu.SMEM),
            pl.BlockSpec((1, n_q, MOBA_WIDTH, MOBA_BLOCK), lambda bi, i: (bi, i, 0, 0)),
            pl.BlockSpec((1, MOBA_HEADS, nb, MOBA_HEAD_DIM), lambda bi, i: (bi, 0, 0, 0)),
        ],
        out_specs=(
            pl.BlockSpec((1, MOBA_HEADS, nb, width), lambda bi, i: (bi, 0, 0, i)),
            pl.BlockSpec((1, MOBA_HEADS, 1, width), lambda bi, i: (bi, 0, 0, i)),
        ),
        out_shape=(
            jax.ShapeDtypeStruct((b, MOBA_HEADS, nb, s), BF16),
            jax.ShapeDtypeStruct((b, MOBA_HEADS, 1, s), F32),
        ),
        compiler_params=pltpu.CompilerParams(
            dimension_semantics=("arbitrary", "arbitrary"), vmem_limit_bytes=VMEM_LIMIT),
        name="moba_gate",
    )(cfar, mqt, kmean)


def _moba_kernel(qt_ref, k_ref, vt_ref, selfar_ref, selprev_ref, tnear_ref,
                 out_ref, kaug_ref, vaug_ref, qaug_ref, qnear_ref, s_ref, m_ref, acc_ref):
    i = pl.program_id(1)
    d = MOBA_HEAD_DIM
    nb = vaug_ref.shape[1]
    s_len = kaug_ref.shape[1]
    blk_rows = MOBA_BLOCK
    n_pairs = MOBA_HEADS // 2

    @pl.when((pl.program_id(0) == 0) & (i == 0))
    def _():
        row_blk = lax.broadcasted_iota(jnp.int32, (s_len, 2 * PAIR_WIDTH), 0) // MOBA_BLOCK
        lane = lax.broadcasted_iota(jnp.int32, (s_len, 2 * PAIR_WIDTH), 1)
        hot_lane = jnp.where(lane < PAIR_WIDTH, lane - d, lane - PAIR_WIDTH)
        in_band = (lane >= d) & (lane < PAIR_WIDTH + d)
        pattern = jnp.where(in_band & (row_blk == hot_lane), 1.0, 0.0).astype(BF16)
        for pe in range(n_pairs):
            kaug_ref[pe] = pattern
        for e in range(MOBA_HEADS):
            vaug_ref[e, :, d:, :] = jnp.ones((nb, ONES_ROWS, MOBA_BLOCK), BF16)

    own_rows = pl.ds(pl.multiple_of(i * blk_rows, blk_rows), blk_rows)
    for pe in range(n_pairs):
        kaug_ref[pe, own_rows, 0:d] = k_ref[0, :, pe * PAIR_WIDTH:pe * PAIR_WIDTH + d]
        kaug_ref[pe, own_rows, PAIR_WIDTH + d:] = k_ref[0, :, pe * PAIR_WIDTH + d:(pe + 1) * PAIR_WIDTH]
    for e in range(MOBA_HEADS):
        vaug_ref[e, i, 0:d, :] = vt_ref[0, 0, e * d:(e + 1) * d, :]

    prow = lax.broadcasted_iota(jnp.int32, (PAIR_WIDTH, MOBA_BLOCK), 0)
    for e in range(MOBA_HEADS):
        pe, he = divmod(e, 2)
        qt2 = qt_ref[0, 0, pe * PAIR_WIDTH:(pe + 1) * PAIR_WIDTH, :]
        q_only = jnp.where((prow >= he * d) & (prow < (he + 1) * d), qt2, jnp.zeros_like(qt2))
        qnear_ref[e] = q_only
        qaug_ref[e] = q_only
        hot_row = d if he == 0 else 0
        qaug_ref[e, hot_row:hot_row + nb, :] = selfar_ref[0, e]

    def qk(e, j, slot, near=False):
        rows = pl.ds(pl.multiple_of(j * blk_rows, blk_rows), blk_rows)
        keys = kaug_ref[e // 2, rows, (e % 2) * PAIR_WIDTH:(e % 2 + 1) * PAIR_WIDTH]
        s_ref[slot, e] = _dot(keys, qnear_ref[e] if near else qaug_ref[e])

    def sm(e, j, slot, one_sweep, bias=None, first=False):
        st = s_ref[slot, e]
        if bias is not None:
            st = st + bias
        if first:
            mj = jnp.max(st, axis=0, keepdims=True)
            m_ref[e] = mj
            acc_ref[e] = _dot(vaug_ref[e, j], jnp.exp2(st - mj).astype(BF16))
        elif one_sweep:
            m_old = m_ref[e]
            p = jnp.exp2(st - m_old).astype(BF16)
            grow = jnp.maximum(jnp.max(p, axis=0, keepdims=True).astype(F32), 1.0)
            acc_ref[e] = (acc_ref[e] + _dot(vaug_ref[e, j], p)) / grow
            m_ref[e] = m_old + jnp.log2(grow)
        else:
            m_old = m_ref[e]
            m_new = jnp.maximum(m_old, jnp.max(st, axis=0, keepdims=True))
            pv = _dot(vaug_ref[e, j], jnp.exp2(st - m_new).astype(BF16))
            acc_ref[e] = acc_ref[e] * jnp.exp2(m_old - m_new) + pv
            m_ref[e] = m_new

    clamp = lambda j: jnp.minimum(j, i)
    heads = range(MOBA_HEADS)
    jp = jnp.maximum(i - 1, 0)
    n_full = lax.shift_right_logical(i, 2)
    has_tail = (i & 3) >= 2

    def attend(one_sweep):
        for blk, slot, near in ((i, 0, True), (jp, 1, True), (clamp(0), 2, False), (clamp(1), 3, False)):
            for e in heads:
                qk(e, blk, slot, near=near)
        for e in heads:
            sm(e, i, 0, one_sweep, bias=tnear_ref[e, blk_rows:, :], first=True)
        for e in heads:
            sm(e, jp, 1, one_sweep, bias=tnear_ref[e, 0:blk_rows, :] + selprev_ref[0, e])

        def far(t, carry):
            j = 4 * t
            for half in (0, 2):
                for sub in (0, 1):
                    for grp in (heads[:4], heads[4:]):
                        for e in grp:
                            qk(e, clamp(j + half + 2 + sub), half + sub)
                        for e in grp:
                            sm(e, clamp(j + half + sub), (half + 2 + sub) % 4, one_sweep)
            return carry

        lax.fori_loop(0, n_full, far, 0)

        @pl.when(has_tail)
        def _():
            for e in heads:
                sm(e, clamp(4 * n_full), 2, one_sweep)
            for e in heads:
                sm(e, clamp(4 * n_full + 1), 3, one_sweep)

    attend(one_sweep=True)
    check = jnp.sum(acc_ref[...]) + jnp.sum(m_ref[...])
    pl.when(jnp.logical_not(jnp.abs(check) < jnp.inf))(lambda: attend(one_sweep=False))

    outs = []
    for e in range(MOBA_HEADS):
        acc = acc_ref[e]
        outs.append((acc[:d] / acc[d:d + 1]).T)
    out_ref[0] = jnp.concatenate(outs, axis=1).astype(BF16)


def _moba(mqt, mk, mvt, selfar, selprev, tnear):
    b, s, _ = mk.shape
    nb = s // MOBA_BLOCK
    assert nb <= MOBA_HEAD_DIM
    nh = MOBA_HEADS
    return pl.pallas_call(
        _moba_kernel,
        grid=(b, nb),
        in_specs=[
            pl.BlockSpec((1, 1, MOBA_WIDTH, MOBA_BLOCK), lambda bi, i: (bi, i, 0, 0)),
            pl.BlockSpec((1, MOBA_BLOCK, MOBA_WIDTH), lambda bi, i: (bi, i, 0)),
            pl.BlockSpec((1, 1, MOBA_WIDTH, MOBA_BLOCK), lambda bi, i: (bi, i, 0, 0)),
            pl.BlockSpec((1, nh, nb, MOBA_BLOCK), lambda bi, i: (bi, 0, 0, i)),
            pl.BlockSpec((1, nh, 1, MOBA_BLOCK), lambda bi, i: (bi, 0, 0, i)),
            pl.BlockSpec((nh, 2 * MOBA_BLOCK, MOBA_BLOCK), lambda bi, i: (0, 0, 0),
                         pipeline_mode=pl.Buffered(1)),
        ],
        out_specs=pl.BlockSpec((1, MOBA_BLOCK, MOBA_WIDTH), lambda bi, i: (bi, i, 0)),
        out_shape=jax.ShapeDtypeStruct((b, s, MOBA_WIDTH), BF16),
        scratch_shapes=[
            pltpu.VMEM((nh // 2, s, 2 * PAIR_WIDTH), BF16),
            pltpu.VMEM((nh, nb, MOBA_HEAD_DIM + ONES_ROWS, MOBA_BLOCK), BF16),
            pltpu.VMEM((nh, PAIR_WIDTH, MOBA_BLOCK), BF16),
            pltpu.VMEM((nh, PAIR_WIDTH, MOBA_BLOCK), BF16),
            pltpu.VMEM((4, nh, MOBA_BLOCK, MOBA_BLOCK), F32),
            pltpu.VMEM((nh, 1, MOBA_BLOCK), F32),
            pltpu.VMEM((nh, MOBA_HEAD_DIM + ONES_ROWS, MOBA_BLOCK), F32),
        ],
        compiler_params=pltpu.CompilerParams(
            dimension_semantics=("arbitrary", "arbitrary"), vmem_limit_bytes=VMEM_LIMIT),
        name="moba",
    )(mqt, mk, mvt, selfar, selprev, tnear)


def _rms(x, g):
    return x * lax.rsqrt(jnp.mean(x * x, axis=-1, keepdims=True) + EPS) * g


def _out_ffn_kernel(x_ref, ret_ref, moba_ref, wo_ref, wg_ref, wu_ref, wd_ref,
                    g_post_mix_ref, g_pre_ffn_ref, g_post_ffn_ref, out_ref):
    tm = x_ref.shape[0]
    groups = [slice(r0, r0 + tm // FFN_ROW_GROUPS) for r0 in range(0, tm, tm // FFN_ROW_GROUPS)]
    mixes = [_dot(jnp.concatenate([ret_ref[rows, :], moba_ref[rows, :]], axis=1), wo_ref[...])
             for rows in groups]
    for rows, mix in zip(groups, mixes):
        x1 = x_ref[rows, :] + _rms(mix, g_post_mix_ref[...])
        h = _rms(x1, g_pre_ffn_ref[...]).astype(BF16)
        f = None
        for lo, hi in FFN_CHUNKS:
            cols = slice(lo, hi)
            gate = _dot(h, wg_ref[:, cols])
            up = _dot(h, wu_ref[:, cols])
            act = (gate * jax.nn.sigmoid(gate) * up).astype(BF16)
            part = _dot(act, wd_ref[cols, :])
            f = part if f is None else f + part
        out_ref[rows, :] = x1 + _rms(f, g_post_ffn_ref[...])


def _out_ffn(x2, ret2, moba2, wo, wg, wu, wd, g_post_mix, g_pre_ffn, g_post_ffn):
    n, d = x2.shape
    tm = min(FFN_TOKENS, n)
    tok = lambda w: pl.BlockSpec((tm, w), lambda t: (t, 0))
    resident = lambda a: pl.BlockSpec(a.shape, lambda t: (0, 0), pipeline_mode=pl.Buffered(1))
    return pl.pallas_call(
        _out_ffn_kernel,
        grid=(n // tm,),
        in_specs=[tok(d), tok(RET_WIDTH), tok(MOBA_WIDTH),
                  resident(wo), resident(wg), resident(wu), resident(wd),
                  resident(g_post_mix), resident(g_pre_ffn), resident(g_post_ffn)],
        out_specs=tok(d),
        out_shape=jax.ShapeDtypeStruct((n, d), F32),
        compiler_params=pltpu.CompilerParams(
            dimension_semantics=("arbitrary",), vmem_limit_bytes=VMEM_LIMIT),
        name="out_ffn",
    )(x2, ret2, moba2, wo, wg, wu, wd, g_post_mix, g_pre_ffn, g_post_ffn)


def _rotary_tables(s):
    half = RET_QK_DIM // 2
    inv_freq = ROPE_BASE ** (-np.arange(half, dtype=np.float64) / half)
    ang = np.arange(s, dtype=np.float64)[:, None] * inv_freq[None, :]
    cos, sin = np.cos(ang), np.sin(ang)
    cq = np.tile(np.concatenate([cos, cos], axis=1), (1, RET_HEADS))
    sq = np.tile(np.concatenate([-sin, sin], axis=1), (1, RET_HEADS))
    return tuple(np.ascontiguousarray(t, dtype=np.float32) for t in (cq, sq, cos.T, sin.T))


def _retention_tables():
    c = RET_CHUNK
    log_gamma = np.log1p(-np.exp(np.linspace(math.log(1.0 / 32), math.log(1.0 / 512), RET_HEADS)))
    idx = np.arange(c, dtype=np.float64)
    diff = idx[:, None] - idx[None, :]
    dmat = np.where(diff >= 0, np.exp(np.maximum(diff, 0.0)[None] * log_gamma[:, None, None]), 0.0)
    q_decay = np.exp((idx + 1.0)[None, :] * log_gamma[:, None])
    k_decay = np.exp((c - 1.0 - idx)[None, :] * log_gamma[:, None])
    chunk_decay = np.exp(c * log_gamma)
    qd = np.broadcast_to(q_decay[:, :, None], (RET_HEADS, c, RET_V_DIM))
    kd = k_decay[:, None, :]
    cd = np.broadcast_to(chunk_decay[:, None, None], (RET_HEADS, 1, RET_V_DIM))
    return tuple(np.ascontiguousarray(t, dtype=np.float32) for t in (dmat, qd, kd, cd))


def _t5_bucket(rel):
    n = np.maximum(rel, 0)
    max_exact = REL_BUCKETS // 2
    large = max_exact + (np.log(np.maximum(n, 1) / max_exact) / math.log(REL_MAX_DIST / max_exact)
                         * (REL_BUCKETS - max_exact)).astype(np.int64)
    return np.where(n < max_exact, n, np.minimum(large, REL_BUCKETS - 1))


def _toeplitz(vec):
    h, two_n = vec.shape
    n = two_n // 2
    ext = jnp.concatenate([vec, jnp.zeros((h, 1), vec.dtype)], axis=1)
    skew = jnp.tile(ext, (1, n))[:, :n * two_n].reshape(h, n, two_n)
    return skew[:, :, n:]


def _bias_tables(rel_bias):
    bias_t = rel_bias.T.astype(F32) * LOG2E
    n = MOBA_BLOCK
    one_hot = (_t5_bucket(np.arange(2 * n))[:, None] == np.arange(REL_BUCKETS)).astype(np.float32)
    by_dist = jnp.einsum("rn,hn->hr", one_hot, bias_t, precision=lax.Precision.HIGHEST)
    tprev = _toeplitz(by_dist)
    own_vec = jnp.concatenate([jnp.full((bias_t.shape[0], n), NEG, F32), by_dist[:, :n]], axis=1)
    town = _toeplitz(own_vec)
    cfar = bias_t[:, REL_BUCKETS - 1]
    return jnp.concatenate([tprev, town], axis=1), cfar


def kernel(x, w_in, w_out, pre_mix_norm, post_mix_norm, pre_ffn_norm, post_ffn_norm,
           rel_bias, w_gate, w_up, w_down):
    b, s, d = x.shape
    assert d == D_MODEL and s % IN_TOKENS == 0 and MOBA_BLOCK + 1 >= REL_MAX_DIST
    depth = w_in.shape[0]

    cq, sq, ckt, skt = _rotary_tables(s)
    dmat, qd, kd, cd = _retention_tables()
    tnear, cfar = _bias_tables(rel_bias)

    sizes = [RET_QK_WIDTH, RET_QK_WIDTH, RET_WIDTH, RET_WIDTH, MOBA_WIDTH, MOBA_WIDTH, MOBA_WIDTH]
    o_rq, o_rk, o_rv, o_rg, o_mq, o_mk, o_mv, _ = np.cumsum([0] + sizes).tolist()

    for layer in range(depth):
        w = w_in[layer]
        w_nat = jnp.concatenate(
            [w[:, o_rq:o_rk], w[:, o_rv:o_mq], w[:, o_mk:o_mv]], axis=1).astype(BF16)
        w_tr = jnp.concatenate([w[:, o_rk:o_rv], w[:, o_mq:o_mk], w[:, o_mv:]], axis=1).T.astype(BF16)

        rq, rv, sg, mk, kmean, rkt, mqt, mvt = _in_proj(
            x, pre_mix_norm[layer][None, :], w_nat, w_tr, cq, sq, ckt, skt)

        ret = _retention(rq, rkt, rv, sg, dmat, qd, kd, cd)

        nb = s // MOBA_BLOCK
        kmean = kmean.reshape(b, nb, MOBA_HEADS, MOBA_HEAD_DIM).transpose(0, 2, 1, 3)
        selfar, selprev = _moba_gate(cfar, mqt, kmean)
        moba = _moba(mqt, mk, mvt, selfar, selprev, tnear)

        x = _out_ffn(
            x.reshape(b * s, d), ret.reshape(b * s, RET_WIDTH), moba.reshape(b * s, MOBA_WIDTH),
            w_out[layer].astype(BF16), w_gate[layer].astype(BF16), w_up[layer].astype(BF16),
            w_down[layer].astype(BF16), post_mix_norm[layer][None, :], pre_ffn_norm[layer][None, :],
            post_ffn_norm[layer][None, :]).reshape(b, s, d)
    return x
```

```python
import functools
import math

import jax
import jax.numpy as jnp
import numpy as np
from jax import lax
from jax.experimental import pallas as pl
from jax.experimental.pallas import tpu as pltpu

F32 = jnp.float32
BF16 = jnp.bfloat16

D_MODEL = 1024
RET_HEADS = 4
RET_QK_DIM = 64
RET_V_DIM = 128
RET_CHUNK = 128
RET_QK_WIDTH = RET_HEADS * RET_QK_DIM
RET_WIDTH = RET_HEADS * RET_V_DIM
MOBA_HEADS = 8
MOBA_HEAD_DIM = 64
MOBA_WIDTH = MOBA_HEADS * MOBA_HEAD_DIM
MOBA_BLOCK = 256
MOBA_TOPK = 3
REL_BUCKETS = 32
REL_MAX_DIST = 128
D_FF = 2816
EPS = 1e-6
ROPE_BASE = 10000.0

NEG = -1e30
LOG2E = math.log2(math.e)

IN_TOKENS = 1024
IN_ROW_GROUPS = 2
RET_TOKENS = 1024
FFN_TOKENS = 1024
FFN_ROW_GROUPS = 4
MXU_TILE = 256
FFN_CHUNKS = ((0, 6 * MXU_TILE), (6 * MXU_TILE, D_FF))
VMEM_LIMIT = 56 * 1024 * 1024

NAT_WIDTH = RET_QK_WIDTH + 2 * RET_WIDTH + MOBA_WIDTH
TR_WIDTH = RET_QK_WIDTH + 2 * MOBA_WIDTH


def _nt_dot(a, b):
    return lax.dot_general(a, b, (((1,), (1,)), ((), ())), preferred_element_type=F32)


def _dot(a, b):
    return jnp.dot(a, b, preferred_element_type=F32)


def _in_proj_kernel(x_ref, g_ref, wn_ref, wt_ref, cq_ref, sq_ref, ckt_ref, skt_ref,
                    rq_ref, rv_ref, sg_ref, mk_ref, kmean_ref, rkt_ref, mqt_ref, mvt_ref):
    tm = x_ref.shape[1] // IN_ROW_GROUPS
    hs = []
    for grp in range(IN_ROW_GROUPS):
        x = x_ref[0, grp * tm:(grp + 1) * tm, :]
        ms = jnp.mean(x * x, axis=-1, keepdims=True)
        hs.append((x * lax.rsqrt(ms + EPS) * g_ref[...]).astype(BF16))

    for grp, h in enumerate(hs):
        rows = slice(grp * tm, (grp + 1) * tm)
        pn = _dot(h, wn_ref[...])
        pt = _nt_dot(wt_ref[...], h)

        rq = pn[:, :RET_QK_WIDTH]
        lane = lax.broadcasted_iota(jnp.int32, rq.shape, 1)
        first_half = (lane % RET_QK_DIM) < (RET_QK_DIM // 2)
        partner = jnp.where(first_half,
                            pltpu.roll(rq, RET_QK_WIDTH - RET_QK_DIM // 2, 1),
                            pltpu.roll(rq, RET_QK_DIM // 2, 1))
        rq_ref[0, rows, :] = (rq * cq_ref[rows, :] + partner * sq_ref[rows, :]).astype(BF16)

        rv_ref[0, rows, :] = pn[:, RET_QK_WIDTH:RET_QK_WIDTH + RET_WIDTH].astype(BF16)
        rg = pn[:, RET_QK_WIDTH + RET_WIDTH:RET_QK_WIDTH + 2 * RET_WIDTH]
        sg_ref[0, rows, :] = (rg * jax.nn.sigmoid(rg)).astype(BF16)

        mk = pn[:, RET_QK_WIDTH + 2 * RET_WIDTH:]
        mk_ref[0, rows, :] = mk.astype(BF16)
        blk0 = grp * (tm // MOBA_BLOCK)
        for blk in range(tm // MOBA_BLOCK):
            kmean_ref[0, 0, blk0 + blk:blk0 + blk + 1, :] = jnp.mean(
                mk[blk * MOBA_BLOCK:(blk + 1) * MOBA_BLOCK], axis=0, keepdims=True)

        half = RET_QK_DIM // 2
        cos_t = ckt_ref[:, rows]
        sin_t = skt_ref[:, rows]
        parts = []
        for hd in range(RET_HEADS):
            x1 = pt[hd * RET_QK_DIM:hd * RET_QK_DIM + half]
            x2 = pt[hd * RET_QK_DIM + half:(hd + 1) * RET_QK_DIM]
            parts.append(x1 * cos_t - x2 * sin_t)
            parts.append(x2 * cos_t + x1 * sin_t)
        rkt = (jnp.concatenate(parts, axis=0) * (RET_QK_DIM ** -0.5)).astype(BF16)
        c0 = grp * (tm // RET_CHUNK)
        for c in range(tm // RET_CHUNK):
            rkt_ref[0, c0 + c] = rkt[:, c * RET_CHUNK:(c + 1) * RET_CHUNK]

        mqt = (pt[RET_QK_WIDTH:RET_QK_WIDTH + MOBA_WIDTH] * (MOBA_HEAD_DIM ** -0.5 * LOG2E)).astype(BF16)
        mvt = pt[RET_QK_WIDTH + MOBA_WIDTH:].astype(BF16)
        for blk in range(tm // MOBA_BLOCK):
            mqt_ref[0, blk0 + blk] = mqt[:, blk * MOBA_BLOCK:(blk + 1) * MOBA_BLOCK]
            mvt_ref[0, blk0 + blk] = mvt[:, blk * MOBA_BLOCK:(blk + 1) * MOBA_BLOCK]


def _in_proj(x, gain, w_nat, w_tr, cq, sq, ckt, skt):
    b, s, d = x.shape
    tm = IN_TOKENS
    ns = s // tm
    bpt = tm // MOBA_BLOCK
    cpt = tm // RET_CHUNK
    const = lambda si, bi: (0, 0)
    out_shape = (
        jax.ShapeDtypeStruct((b, s, RET_QK_WIDTH), BF16),
        jax.ShapeDtypeStruct((b, s, RET_WIDTH), BF16),
        jax.ShapeDtypeStruct((b, s, RET_WIDTH), BF16),
        jax.ShapeDtypeStruct((b, s, MOBA_WIDTH), BF16),
        jax.ShapeDtypeStruct((b, ns, bpt, MOBA_WIDTH), F32),
        jax.ShapeDtypeStruct((b, s // RET_CHUNK, RET_QK_WIDTH, RET_CHUNK), BF16),
        jax.ShapeDtypeStruct((b, s // MOBA_BLOCK, MOBA_WIDTH, MOBA_BLOCK), BF16),
        jax.ShapeDtypeStruct((b, s // MOBA_BLOCK, MOBA_WIDTH, MOBA_BLOCK), BF16),
    )
    tok = lambda w: pl.BlockSpec((1, tm, w), lambda si, bi: (bi, si, 0))
    return pl.pallas_call(
        _in_proj_kernel,
        grid=(ns, b),
        in_specs=[
            tok(d),
            pl.BlockSpec((1, d), const),
            pl.BlockSpec((d, NAT_WIDTH), const),
            pl.BlockSpec((TR_WIDTH, d), const),
            pl.BlockSpec((tm, RET_QK_WIDTH), lambda si, bi: (si, 0)),
            pl.BlockSpec((tm, RET_QK_WIDTH), lambda si, bi: (si, 0)),
            pl.BlockSpec((RET_QK_DIM // 2, tm), lambda si, bi: (0, si)),
            pl.BlockSpec((RET_QK_DIM // 2, tm), lambda si, bi: (0, si)),
        ],
        out_specs=(
            tok(RET_QK_WIDTH), tok(RET_WIDTH), tok(RET_WIDTH), tok(MOBA_WIDTH),
            pl.BlockSpec((1, 1, bpt, MOBA_WIDTH), lambda si, bi: (bi, si, 0, 0)),
            pl.BlockSpec((1, cpt, RET_QK_WIDTH, RET_CHUNK), lambda si, bi: (bi, si, 0, 0)),
            pl.BlockSpec((1, bpt, MOBA_WIDTH, MOBA_BLOCK), lambda si, bi: (bi, si, 0, 0)),
            pl.BlockSpec((1, bpt, MOBA_WIDTH, MOBA_BLOCK), lambda si, bi: (bi, si, 0, 0)),
        ),
        out_shape=out_shape,
        compiler_params=pltpu.CompilerParams(
            dimension_semantics=("arbitrary", "arbitrary"), vmem_limit_bytes=VMEM_LIMIT),
        name="in_proj",
    )(x, gain, w_nat, w_tr, cq, sq, ckt, skt)


def _retention_kernel(rq_ref, rkt_ref, rv_ref, sg_ref, dmat_ref, qd_ref, kd_ref, cd_ref,
                      out_ref, state_ref):
    @pl.when(pl.program_id(1) == 0)
    def _():
        state_ref[...] = jnp.zeros_like(state_ref)

    n_chunks = rq_ref.shape[1] // RET_CHUNK

    def chunk(c, carry):
        r0 = pl.multiple_of(c * RET_CHUNK, RET_CHUNK)
        rows = pl.ds(r0, RET_CHUNK)
        for hd in range(RET_HEADS):
            q = rq_ref[0, rows, hd * RET_QK_DIM:(hd + 1) * RET_QK_DIM]
            kt = rkt_ref[0, c, hd * RET_QK_DIM:(hd + 1) * RET_QK_DIM, :]
            v = rv_ref[0, rows, hd * RET_V_DIM:(hd + 1) * RET_V_DIM]
            state = state_ref[hd]
            qk_qs = _dot(q, jnp.concatenate([kt, state.astype(BF16)], axis=1))
            scores = qk_qs[:, :RET_CHUNK] * dmat_ref[hd]
            o = _dot(scores.astype(BF16), v) + qd_ref[hd] * qk_qs[:, RET_CHUNK:]
            kts = (kt.astype(F32) * kd_ref[hd]).astype(BF16)
            state_ref[hd] = state * cd_ref[hd] + _dot(kts, v)
            ms = jnp.mean(o * o, axis=-1, keepdims=True)
            gate = sg_ref[0, rows, hd * RET_V_DIM:(hd + 1) * RET_V_DIM].astype(F32)
            out_ref[0, rows, hd * RET_V_DIM:(hd + 1) * RET_V_DIM] = (
                o * lax.rsqrt(ms + EPS) * gate).astype(BF16)
        return carry

    lax.fori_loop(0, n_chunks, chunk, 0, unroll=8)


def _retention(rq, rkt, rv, sg, dmat, qd, kd, cd):
    b, s, _ = rq.shape
    tc = min(RET_TOKENS, s)
    tok = lambda w: pl.BlockSpec((1, tc, w), lambda bi, si: (bi, si, 0))
    tab = lambda a: pl.BlockSpec(a.shape, lambda bi, si: (0,) * a.ndim)
    return pl.pallas_call(
        _retention_kernel,
        grid=(b, s // tc),
        in_specs=[
            tok(RET_QK_WIDTH),
            pl.BlockSpec((1, tc // RET_CHUNK, RET_QK_WIDTH, RET_CHUNK), lambda bi, si: (bi, si, 0, 0)),
            tok(RET_WIDTH), tok(RET_WIDTH),
            tab(dmat), tab(qd), tab(kd), tab(cd),
        ],
        out_specs=tok(RET_WIDTH),
        out_shape=jax.ShapeDtypeStruct((b, s, RET_WIDTH), BF16),
        scratch_shapes=[pltpu.VMEM((RET_HEADS, RET_QK_DIM, RET_V_DIM), F32)],
        compiler_params=pltpu.CompilerParams(
            dimension_semantics=("arbitrary", "arbitrary"), vmem_limit_bytes=VMEM_LIMIT),
        name="retention",
    )(rq, rkt, rv, sg, dmat, qd, kd, cd)


ONES_ROWS = 16
PAIR_WIDTH = 2 * MOBA_HEAD_DIM
GATE_BLOCKS = 4


def _moba_gate_kernel(cfar_ref, qt_ref, km_ref, selfar_ref, selprev_ref):
    d = MOBA_HEAD_DIM
    nb = km_ref.shape[2]
    n_q = qt_ref.shape[1]
    shape = (nb, n_q * MOBA_BLOCK)
    blk = lax.broadcasted_iota(jnp.int32, shape, 0)
    i = pl.program_id(1) * n_q + lax.broadcasted_iota(jnp.int32, shape, 1) // MOBA_BLOCK
    blk_f = blk.astype(F32)
    for h in range(MOBA_HEADS):
        qt = jnp.concatenate([qt_ref[0, c, h * d:(h + 1) * d, :] for c in range(n_q)], axis=1)
        km = km_ref[0, h]
        km_hi = km.astype(BF16)
        km_lo = (km - km_hi.astype(F32)).astype(BF16)
        gate = _dot(km_hi, qt) + _dot(km_lo, qt)
        gate = jnp.where(blk < i, gate, NEG)
        chosen = jnp.zeros(gate.shape, F32)
        for _ in range(MOBA_TOPK):
            best = jnp.max(gate, axis=0, keepdims=True)
            idx = jnp.min(jnp.where(gate == best, blk_f, float(nb)), axis=0, keepdims=True)
            hit = blk_f == idx
            chosen = jnp.where(hit & (best > 0.5 * NEG), 1.0, chosen)
            gate = jnp.where(hit, NEG, gate)
        picked = chosen > 0.5
        selfar_ref[0, h] = jnp.where(picked & (blk < i - 1), cfar_ref[h], NEG).astype(BF16)
        prev_hit = jnp.max(jnp.where(picked & (blk == i - 1), 1.0, 0.0), axis=0, keepdims=True)
        selprev_ref[0, h] = jnp.where(prev_hit > 0.5, 0.0, NEG)


def _moba_gate(cfar, mqt, kmean):
    b, nb, _, _ = mqt.shape
    s = nb * MOBA_BLOCK
    n_q = math.gcd(GATE_BLOCKS, nb)
    width = n_q * MOBA_BLOCK
    return pl.pallas_call(
        _moba_gate_kernel,
        grid=(b, nb // n_q),
        in_specs=[
            pl.BlockSpec(memory_space=pltpu.SMEM),
            pl.BlockSpec((1, n_q, MOBA_WIDTH, MOBA_BLOCK), lambda bi, i: (bi, i, 0, 0)),
            pl.BlockSpec((1, MOBA_HEADS, nb, MOBA_HEAD_DIM), lambda bi, i: (bi, 0, 0, 0)),
        ],
        out_specs=(
            pl.BlockSpec((1, MOBA_HEADS, nb, width), lambda bi, i: (bi, 0, 0, i)),
            pl.BlockSpec((1, MOBA_HEADS, 1, width), lambda bi, i: (bi, 0, 0, i)),
        ),
        out_shape=(
            jax.ShapeDtypeStruct((b, MOBA_HEADS, nb, s), BF16),
            jax.ShapeDtypeStruct((b, MOBA_HEADS, 1, s), F32),
        ),
        compiler_params=pltpu.CompilerParams(
            dimension_semantics=("arbitrary", "arbitrary"), vmem_limit_bytes=VMEM_LIMIT),
        name="moba_gate",
    )(cfar, mqt, kmean)


def _moba_kernel(qt_ref, k_ref, vt_ref, selfar_ref, selprev_ref, tnear_ref,
                 out_ref, kaug_ref, vaug_ref, qaug_ref, qnear_ref, s_ref, m_ref, acc_ref):
    i = pl.program_id(1)
    d = MOBA_HEAD_DIM
    nb = vaug_ref.shape[1]
    s_len = kaug_ref.shape[1]
    blk_rows = MOBA_BLOCK
    n_pairs = MOBA_HEADS // 2

    @pl.when((pl.program_id(0) == 0) & (i == 0))
    def _():
        row_blk = lax.broadcasted_iota(jnp.int32, (s_len, 2 * PAIR_WIDTH), 0) // MOBA_BLOCK
        lane = lax.broadcasted_iota(jnp.int32, (s_len, 2 * PAIR_WIDTH), 1)
        hot_lane = jnp.where(lane < PAIR_WIDTH, lane - d, lane - PAIR_WIDTH)
        in_band = (lane >= d) & (lane < PAIR_WIDTH + d)
        pattern = jnp.where(in_band & (row_blk == hot_lane), 1.0, 0.0).astype(BF16)
        for pe in range(n_pairs):
            kaug_ref[pe] = pattern
        for e in range(MOBA_HEADS):
            vaug_ref[e, :, d:, :] = jnp.ones((nb, ONES_ROWS, MOBA_BLOCK), BF16)

    own_rows = pl.ds(pl.multiple_of(i * blk_rows, blk_rows), blk_rows)
    for pe in range(n_pairs):
        kaug_ref[pe, own_rows, 0:d] = k_ref[0, :, pe * PAIR_WIDTH:pe * PAIR_WIDTH + d]
        kaug_ref[pe, own_rows, PAIR_WIDTH + d:] = k_ref[0, :, pe * PAIR_WIDTH + d:(pe + 1) * PAIR_WIDTH]
    for e in range(MOBA_HEADS):
        vaug_ref[e, i, 0:d, :] = vt_ref[0, 0, e * d:(e + 1) * d, :]

    prow = lax.broadcasted_iota(jnp.int32, (PAIR_WIDTH, MOBA_BLOCK), 0)
    for e in range(MOBA_HEADS):
        pe, he = divmod(e, 2)
        qt2 = qt_ref[0, 0, pe * PAIR_WIDTH:(pe + 1) * PAIR_WIDTH, :]
        q_only = jnp.where((prow >= he * d) & (prow < (he + 1) * d), qt2, jnp.zeros_like(qt2))
        qnear_ref[e] = q_only
        qaug_ref[e] = q_only
        hot_row = d if he == 0 else 0
        qaug_ref[e, hot_row:hot_row + nb, :] = selfar_ref[0, e]

    def qk(e, j, slot, near=False):
        rows = pl.ds(pl.multiple_of(j * blk_rows, blk_rows), blk_rows)
        keys = kaug_ref[e // 2, rows, (e % 2) * PAIR_WIDTH:(e % 2 + 1) * PAIR_WIDTH]
        s_ref[slot, e] = _dot(keys, qnear_ref[e] if near else qaug_ref[e])

    def sm(e, j, slot, one_sweep, bias=None, first=False):
        st = s_ref[slot, e]
        if bias is not None:
            st = st + bias
        if first:
            mj = jnp.max(st, axis=0, keepdims=True)
            m_ref[e] = mj
            acc_ref[e] = _dot(vaug_ref[e, j], jnp.exp2(st - mj).astype(BF16))
        elif one_sweep:
            m_old = m_ref[e]
            p = jnp.exp2(st - m_old).astype(BF16)
            grow = jnp.maximum(jnp.max(p, axis=0, keepdims=True).astype(F32), 1.0)
            acc_ref[e] = (acc_ref[e] + _dot(vaug_ref[e, j], p)) / grow
            m_ref[e] = m_old + jnp.log2(grow)
        else:
            m_old = m_ref[e]
            m_new = jnp.maximum(m_old, jnp.max(st, axis=0, keepdims=True))
            pv = _dot(vaug_ref[e, j], jnp.exp2(st - m_new).astype(BF16))
            acc_ref[e] = acc_ref[e] * jnp.exp2(m_old - m_new) + pv
            m_ref[e] = m_new

    clamp = lambda j: jnp.minimum(j, i)
    heads = range(MOBA_HEADS)
    jp = jnp.maximum(i - 1, 0)
    n_full = lax.shift_right_logical(i, 2)
    has_tail = (i & 3) >= 2

    def attend(one_sweep):
        for blk, slot, near in ((i, 0, True), (jp, 1, True), (clamp(0), 2, False), (clamp(1), 3, False)):
            for e in heads:
                qk(e, blk, slot, near=near)
        for e in heads:
            sm(e, i, 0, one_sweep, bias=tnear_ref[e, blk_rows:, :], first=True)
        for e in heads:
            sm(e, jp, 1, one_sweep, bias=tnear_ref[e, 0:blk_rows, :] + selprev_ref[0, e])

        def far(t, carry):
            j = 4 * t
            for half in (0, 2):
                for sub in (0, 1):
                    for grp in (heads[0:2], heads[2:4], heads[4:6], heads[6:8]):
                        for e in grp:
                            qk(e, clamp(j + half + 2 + sub), half + sub)
                        for e in grp:
                            sm(e, clamp(j + half + sub), (half + 2 + sub) % 4, one_sweep)
            return carry

        lax.fori_loop(0, n_full, far, 0)

        @pl.when(has_tail)
        def _():
            for e in heads:
                sm(e, clamp(4 * n_full), 2, one_sweep)
            for e in heads:
                sm(e, clamp(4 * n_full + 1), 3, one_sweep)

    attend(one_sweep=True)
    check = jnp.sum(acc_ref[...]) + jnp.sum(m_ref[...])
    pl.when(jnp.logical_not(jnp.abs(check) < jnp.inf))(lambda: attend(one_sweep=False))

    outs = []
    for e in range(MOBA_HEADS):
        acc = acc_ref[e]
        outs.append((acc[:d] / acc[d:d + 1]).T)
    out_ref[0] = jnp.concatenate(outs, axis=1).astype(BF16)


def _moba(mqt, mk, mvt, selfar, selprev, tnear):
    b, s, _ = mk.shape
    nb = s // MOBA_BLOCK
    assert nb <= MOBA_HEAD_DIM
    nh = MOBA_HEADS
    return pl.pallas_call(
        _moba_kernel,
        grid=(b, nb),
        in_specs=[
            pl.BlockSpec((1, 1, MOBA_WIDTH, MOBA_BLOCK), lambda bi, i: (bi, i, 0, 0)),
            pl.BlockSpec((1, MOBA_BLOCK, MOBA_WIDTH), lambda bi, i: (bi, i, 0)),
            pl.BlockSpec((1, 1, MOBA_WIDTH, MOBA_BLOCK), lambda bi, i: (bi, i, 0, 0)),
            pl.BlockSpec((1, nh, nb, MOBA_BLOCK), lambda bi, i: (bi, 0, 0, i)),
            pl.BlockSpec((1, nh, 1, MOBA_BLOCK), lambda bi, i: (bi, 0, 0, i)),
            pl.BlockSpec((nh, 2 * MOBA_BLOCK, MOBA_BLOCK), lambda bi, i: (0, 0, 0),
                         pipeline_mode=pl.Buffered(1)),
        ],
        out_specs=pl.BlockSpec((1, MOBA_BLOCK, MOBA_WIDTH), lambda bi, i: (bi, i, 0)),
        out_shape=jax.ShapeDtypeStruct((b, s, MOBA_WIDTH), BF16),
        scratch_shapes=[
            pltpu.VMEM((nh // 2, s, 2 * PAIR_WIDTH), BF16),
            pltpu.VMEM((nh, nb, MOBA_HEAD_DIM + ONES_ROWS, MOBA_BLOCK), BF16),
            pltpu.VMEM((nh, PAIR_WIDTH, MOBA_BLOCK), BF16),
            pltpu.VMEM((nh, PAIR_WIDTH, MOBA_BLOCK), BF16),
            pltpu.VMEM((4, nh, MOBA_BLOCK, MOBA_BLOCK), F32),
            pltpu.VMEM((nh, 1, MOBA_BLOCK), F32),
            pltpu.VMEM((nh, MOBA_HEAD_DIM + ONES_ROWS, MOBA_BLOCK), F32),
        ],
        compiler_params=pltpu.CompilerParams(
            dimension_semantics=("arbitrary", "arbitrary"), vmem_limit_bytes=VMEM_LIMIT),
        name="moba",
    )(mqt, mk, mvt, selfar, selprev, tnear)


def _rms(x, g):
    return x * lax.rsqrt(jnp.mean(x * x, axis=-1, keepdims=True) + EPS) * g


def _out_ffn_kernel(x_ref, ret_ref, moba_ref, wo_ref, wg_ref, wu_ref, wd_ref,
                    g_post_mix_ref, g_pre_ffn_ref, g_post_ffn_ref, out_ref):
    tm = x_ref.shape[0]
    groups = [slice(r0, r0 + tm // FFN_ROW_GROUPS) for r0 in range(0, tm, tm // FFN_ROW_GROUPS)]
    mixes = [_dot(jnp.concatenate([ret_ref[rows, :], moba_ref[rows, :]], axis=1), wo_ref[...])
             for rows in groups]
    for rows, mix in zip(groups, mixes):
        x1 = x_ref[rows, :] + _rms(mix, g_post_mix_ref[...])
        h = _rms(x1, g_pre_ffn_ref[...]).astype(BF16)
        f = None
        for lo, hi in FFN_CHUNKS:
            cols = slice(lo, hi)
            gate = _dot(h, wg_ref[:, cols])
            up = _dot(h, wu_ref[:, cols])
            act = (gate * jax.nn.sigmoid(gate) * up).astype(BF16)
            part = _dot(act, wd_ref[cols, :])
            f = part if f is None else f + part
        out_ref[rows, :] = x1 + _rms(f, g_post_ffn_ref[...])


def _out_ffn(x2, ret2, moba2, wo, wg, wu, wd, g_post_mix, g_pre_ffn, g_post_ffn):
    n, d = x2.shape
    tm = min(FFN_TOKENS, n)
    tok = lambda w: pl.BlockSpec((tm, w), lambda t: (t, 0))
    resident = lambda a: pl.BlockSpec(a.shape, lambda t: (0, 0), pipeline_mode=pl.Buffered(1))
    return pl.pallas_call(
        _out_ffn_kernel,
        grid=(n // tm,),
        in_specs=[tok(d), tok(RET_WIDTH), tok(MOBA_WIDTH),
                  resident(wo), resident(wg), resident(wu), resident(wd),
                  resident(g_post_mix), resident(g_pre_ffn), resident(g_post_ffn)],
        out_specs=tok(d),
        out_shape=jax.ShapeDtypeStruct((n, d), F32),
        compiler_params=pltpu.CompilerParams(
            dimension_semantics=("arbitrary",), vmem_limit_bytes=VMEM_LIMIT),
        name="out_ffn",
    )(x2, ret2, moba2, wo, wg, wu, wd, g_post_mix, g_pre_ffn, g_post_ffn)


def _rotary_tables(s):
    half = RET_QK_DIM // 2
    inv_freq = ROPE_BASE ** (-np.arange(half, dtype=np.float64) / half)
    ang = np.arange(s, dtype=np.float64)[:, None] * inv_freq[None, :]
    cos, sin = np.cos(ang), np.sin(ang)
    cq = np.tile(np.concatenate([cos, cos], axis=1), (1, RET_HEADS))
    sq = np.tile(np.concatenate([-sin, sin], axis=1), (1, RET_HEADS))
    return tuple(np.ascontiguousarray(t, dtype=np.float32) for t in (cq, sq, cos.T, sin.T))


def _retention_tables():
    c = RET_CHUNK
    log_gamma = np.log1p(-np.exp(np.linspace(math.log(1.0 / 32), math.log(1.0 / 512), RET_HEADS)))
    idx = np.arange(c, dtype=np.float64)
    diff = idx[:, None] - idx[None, :]
    dmat = np.where(diff >= 0, np.exp(np.maximum(diff, 0.0)[None] * log_gamma[:, None, None]), 0.0)
    q_decay = np.exp((idx + 1.0)[None, :] * log_gamma[:, None])
    k_decay = np.exp((c - 1.0 - idx)[None, :] * log_gamma[:, None])
    chunk_decay = np.exp(c * log_gamma)
    qd = np.broadcast_to(q_decay[:, :, None], (RET_HEADS, c, RET_V_DIM))
    kd = k_decay[:, None, :]
    cd = np.broadcast_to(chunk_decay[:, None, None], (RET_HEADS, 1, RET_V_DIM))
    return tuple(np.ascontiguousarray(t, dtype=np.float32) for t in (dmat, qd, kd, cd))


def _t5_bucket(rel):
    n = np.maximum(rel, 0)
    max_exact = REL_BUCKETS // 2
    large = max_exact + (np.log(np.maximum(n, 1) / max_exact) / math.log(REL_MAX_DIST / max_exact)
                         * (REL_BUCKETS - max_exact)).astype(np.int64)
    return np.where(n < max_exact, n, np.minimum(large, REL_BUCKETS - 1))


def _toeplitz(vec):
    h, two_n = vec.shape
    n = two_n // 2
    ext = jnp.concatenate([vec, jnp.zeros((h, 1), vec.dtype)], axis=1)
    skew = jnp.tile(ext, (1, n))[:, :n * two_n].reshape(h, n, two_n)
    return skew[:, :, n:]


def _bias_tables(rel_bias):
    bias_t = rel_bias.T.astype(F32) * LOG2E
    n = MOBA_BLOCK
    one_hot = (_t5_bucket(np.arange(2 * n))[:, None] == np.arange(REL_BUCKETS)).astype(np.float32)
    by_dist = jnp.einsum("rn,hn->hr", one_hot, bias_t, precision=lax.Precision.HIGHEST)
    tprev = _toeplitz(by_dist)
    own_vec = jnp.concatenate([jnp.full((bias_t.shape[0], n), NEG, F32), by_dist[:, :n]], axis=1)
    town = _toeplitz(own_vec)
    cfar = bias_t[:, REL_BUCKETS - 1]
    return jnp.concatenate([tprev, town], axis=1), cfar


def kernel(x, w_in, w_out, pre_mix_norm, post_mix_norm, pre_ffn_norm, post_ffn_norm,
           rel_bias, w_gate, w_up, w_down):
    b, s, d = x.shape
    assert d == D_MODEL and s % IN_TOKENS == 0 and MOBA_BLOCK + 1 >= REL_MAX_DIST
    depth = w_in.shape[0]

    cq, sq, ckt, skt = _rotary_tables(s)
    dmat, qd, kd, cd = _retention_tables()
    tnear, cfar = _bias_tables(rel_bias)

    sizes = [RET_QK_WIDTH, RET_QK_WIDTH, RET_WIDTH, RET_WIDTH, MOBA_WIDTH, MOBA_WIDTH, MOBA_WIDTH]
    o_rq, o_rk, o_rv, o_rg, o_mq, o_mk, o_mv, _ = np.cumsum([0] + sizes).tolist()

    for layer in range(depth):
        w = w_in[layer]
        w_nat = jnp.concatenate(
            [w[:, o_rq:o_rk], w[:, o_rv:o_mq], w[:, o_mk:o_mv]], axis=1).astype(BF16)
        w_tr = jnp.concatenate([w[:, o_rk:o_rv], w[:, o_mq:o_mk], w[:, o_mv:]], axis=1).T.astype(BF16)

        rq, rv, sg, mk, kmean, rkt, mqt, mvt = _in_proj(
            x, pre_mix_norm[layer][None, :], w_nat, w_tr, cq, sq, ckt, skt)

        ret = _retention(rq, rkt, rv, sg, dmat, qd, kd, cd)

        nb = s // MOBA_BLOCK
        kmean = kmean.reshape(b, nb, MOBA_HEADS, MOBA_HEAD_DIM).transpose(0, 2, 1, 3)
        selfar, selprev = _moba_gate(cfar, mqt, kmean)
        moba = _moba(mqt, mk, mvt, selfar, selprev, tnear)

        x = _out_ffn(
            x.reshape(b * s, d), ret.reshape(b * s, RET_WIDTH), moba.reshape(b * s, MOBA_WIDTH),
            w_out[layer].astype(BF16), w_gate[layer].astype(BF16), w_up[layer].astype(BF16),
            w_down[layer].astype(BF16), post_mix_norm[layer][None, :], pre_ffn_norm[layer][None, :],
            post_ffn_norm[layer][None, :]).reshape(b, s, d)
    return x
```
